```python
import jax, jax.numpy as jnp
from jax import lax
import numpy as np

D_MODEL = 1024
BATCH = 8
SEQ = 2048
DEPTH = 1

CHUNK = 64
HEAD_DIM = 64
SB_HEADS = 8
FOX_HEADS = 8
SB_WIDTH = SB_HEADS * HEAD_DIM
FOX_WIDTH = FOX_HEADS * HEAD_DIM
MIX_WIDTH = SB_WIDTH + FOX_WIDTH
FFN_HIDDEN = 2816
Q_BLOCK = 128
N_SUBLAYERS = 3
EPS = 1e-6
IN_COLS = 3 * SB_WIDTH + 4 * FOX_WIDTH + FOX_HEADS

kernel_name = 'hybrid_stickbreak_forgetting_macaron_block'


def rmsnorm(x, g):
    xf = x.astype(jnp.float32)
    y = xf * lax.rsqrt(jnp.mean(xf * xf, axis=-1, keepdims=True) + EPS)
    return (y * g.astype(jnp.float32)).astype(x.dtype)


def modulate(x, g, shift, scale):
    return rmsnorm(x, g) * (1.0 + scale[:, None, :]) + shift[:, None, :]


def swiglu(h, w_up, w_down):
    gate, up = jnp.split(h @ w_up, 2, axis=-1)
    return (jax.nn.silu(gate) * up) @ w_down


def _to_blocks(a):
    b, s = a.shape[:2]
    return jnp.moveaxis(a.reshape(b, s // Q_BLOCK, Q_BLOCK, *a.shape[2:]), 1, 0)


def _from_blocks(o):
    o = jnp.moveaxis(o, 0, 1)
    b, nb, qb, h, dh = o.shape
    return o.reshape(b, nb * qb, h, dh)


def stick_breaking_attention(q, k, v):
    s_len = q.shape[1]
    scale = HEAD_DIM ** -0.5
    kf = k.astype(jnp.float32)
    vf = v.astype(jnp.float32)
    key_pos = jnp.arange(s_len)

    def block(args):
        q_blk, t0 = args
        z = jnp.einsum('bqhd,bkhd->bhqk', q_blk.astype(jnp.float32), kf) * scale
        q_pos = t0 + jnp.arange(Q_BLOCK)
        strict = key_pos[None, :] < q_pos[:, None]
        log_beta = jax.nn.log_sigmoid(z)
        log_keep = jnp.where(strict, log_beta - z, 0.0)
        suffix = lax.cumsum(log_keep, axis=3, reverse=True) - log_keep
        w = jnp.where(strict, jnp.exp(log_beta + suffix), 0.0)
        return jnp.einsum('bhqk,bkhd->bqhd', w, vf)

    t0s = jnp.arange(s_len // Q_BLOCK, dtype=jnp.int32) * Q_BLOCK
    o = lax.map(block, (_to_blocks(q), t0s))
    return _from_blocks(o).astype(v.dtype)


def forgetting_attention(q, k, v, log_f_cum):
    s_len = q.shape[1]
    scale = HEAD_DIM ** -0.5
    kf = k.astype(jnp.float32)
    vf = v.astype(jnp.float32)
    key_pos = jnp.arange(s_len)
    f_keys = jnp.moveaxis(log_f_cum, 1, 2)

    def block(args):
        q_blk, f_q, t0 = args
        z = jnp.einsum('bqhd,bkhd->bhqk', q_blk.astype(jnp.float32), kf) * scale
        z = z + jnp.moveaxis(f_q, 1, 2)[..., None] - f_keys[:, :, None, :]
        q_pos = t0 + jnp.arange(Q_BLOCK)
        causal = key_pos[None, :] <= q_pos[:, None]
        p = jax.nn.softmax(jnp.where(causal, z, -jnp.inf), axis=-1)
        return jnp.einsum('bhqk,bkhd->bqhd', p, vf)

    t0s = jnp.arange(s_len // Q_BLOCK, dtype=jnp.int32) * Q_BLOCK
    o = lax.map(block, (_to_blocks(q), _to_blocks(log_f_cum), t0s))
    return _from_blocks(o).astype(v.dtype)


def setup_inputs(seed: int = 0) -> dict:
    key = jax.random.key(seed)
    ks = jax.random.split(key, 24)
    f32 = jnp.float32

    def nrm(k, shape, s):
        return jax.random.normal(k, shape, f32) * s

    def gain(k, shape):
        return 1.0 + 0.05 * jax.random.normal(k, shape, f32)

    d = D_MODEL
    return {
        'x': jax.random.normal(ks[0], (BATCH, SEQ, d), f32),
        'c': jax.random.normal(ks[1], (BATCH, d), f32),
        'w_ada': nrm(ks[2], (DEPTH, d, 3 * N_SUBLAYERS * d), 0.5 * d ** -0.5),
        'b_ada': nrm(ks[3], (DEPTH, 3 * N_SUBLAYERS * d), 0.02),
        'g_ffn1': gain(ks[4], (DEPTH, d)),
        'w_ffn1_up': nrm(ks[5], (DEPTH, d, 2 * FFN_HIDDEN), d ** -0.5),
        'w_ffn1_down': nrm(ks[6], (DEPTH, FFN_HIDDEN, d), FFN_HIDDEN ** -0.5),
        'g_mix': gain(ks[7], (DEPTH, d)),
        'w_in': nrm(ks[8], (DEPTH, d, IN_COLS), d ** -0.5),
        'b_forget': 2.0 + 0.5 * jax.random.normal(ks[9], (DEPTH, FOX_HEADS), f32),
        'g_fox_q': gain(ks[10], (DEPTH, HEAD_DIM)),
        'g_fox_k': gain(ks[11], (DEPTH, HEAD_DIM)),
        'g_sb_out': gain(ks[12], (DEPTH, SB_WIDTH)),
        'g_fox_out': gain(ks[13], (DEPTH, FOX_WIDTH)),
        'w_out': nrm(ks[14], (DEPTH, MIX_WIDTH, d), MIX_WIDTH ** -0.5),
        'g_ffn2': gain(ks[15], (DEPTH, d)),
        'w_ffn2_up': nrm(ks[16], (DEPTH, d, 2 * FFN_HIDDEN), d ** -0.5),
        'w_ffn2_down': nrm(ks[17], (DEPTH, FFN_HIDDEN, d), FFN_HIDDEN ** -0.5),
        'g_final': gain(ks[18], (d,)),
    }


def reference(x, c, w_ada, b_ada, g_ffn1, w_ffn1_up, w_ffn1_down, g_mix, w_in, b_forget,
              g_fox_q, g_fox_k, g_sb_out, g_fox_out, w_out, g_ffn2, w_ffn2_up, w_ffn2_down,
              g_final):
    b, s, _ = x.shape
    cond = jax.nn.silu(c)
    split_pts = [3 * SB_WIDTH, 3 * SB_WIDTH + 3 * FOX_WIDTH, 3 * SB_WIDTH + 4 * FOX_WIDTH]
    for l in range(DEPTH):
        ada = (cond @ w_ada[l] + b_ada[l]).reshape(b, N_SUBLAYERS, 3, D_MODEL)
        shift, scale, gate = ada[:, :, 0], ada[:, :, 1], ada[:, :, 2]

        h = modulate(x, g_ffn1[l], shift[:, 0], scale[:, 0])
        x = x + 0.5 * gate[:, 0][:, None, :] * swiglu(h, w_ffn1_up[l], w_ffn1_down[l])

        h = modulate(x, g_mix[l], shift[:, 1], scale[:, 1])
        proj = h @ w_in[l]
        sb_qkv, fox_qkv, fox_gate, f_logit = jnp.split(proj, split_pts, axis=-1)
        sb_q, sb_k, sb_v = [t.reshape(b, s, SB_HEADS, HEAD_DIM) for t in jnp.split(sb_qkv, 3, axis=-1)]
        fx_q, fx_k, fx_v = [t.reshape(b, s, FOX_HEADS, HEAD_DIM) for t in jnp.split(fox_qkv, 3, axis=-1)]
        fx_q = rmsnorm(fx_q, g_fox_q[l])
        fx_k = rmsnorm(fx_k, g_fox_k[l])
        log_f = jax.nn.log_sigmoid(f_logit.astype(jnp.float32) + b_forget[l].astype(jnp.float32))
        log_f_cum = jnp.cumsum(log_f, axis=1)

        o_sb = stick_breaking_attention(sb_q, sb_k, sb_v)
        o_fx = forgetting_attention(fx_q, fx_k, fx_v, log_f_cum)
        o_sb = rmsnorm(o_sb, g_sb_out[l].reshape(SB_HEADS, HEAD_DIM)).reshape(b, s, SB_WIDTH)
        o_fx = rmsnorm(o_fx, g_fox_out[l].reshape(FOX_HEADS, HEAD_DIM)).reshape(b, s, FOX_WIDTH)
        o_fx = o_fx * jax.nn.sigmoid(fox_gate)
        mix = jnp.concatenate([o_sb, o_fx], axis=-1) @ w_out[l]
        x = x + gate[:, 1][:, None, :] * mix

        h = modulate(x, g_ffn2[l], shift[:, 2], scale[:, 2])
        x = x + 0.5 * gate[:, 2][:, None, :] * swiglu(h, w_ffn2_up[l], w_ffn2_down[l])
    return rmsnorm(x, g_final)
```

```python
import functools

import jax
import jax.numpy as jnp
from jax import lax
from jax.experimental import pallas as pl
from jax.experimental.pallas import tpu as pltpu

D_MODEL = 1024
HEAD_DIM = 64
SB_HEADS = 8
FOX_HEADS = 8
SB_WIDTH = SB_HEADS * HEAD_DIM
FOX_WIDTH = FOX_HEADS * HEAD_DIM
FFN_HIDDEN = 2816
N_SUBLAYERS = 3
EPS = 1e-6
QK_SCALE = HEAD_DIM ** -0.5

LANES = 128
HEADS_PER_TILE = LANES // HEAD_DIM
VMEM_LIMIT_BYTES = 56 * 1024 * 1024

ADA_TN = 1152
FFN_TM = 512
FFN_HC = 256
PROJ_TM = 512
ATT_T = 256

F32_EXP_ZERO_BELOW = -104.0

BF16 = jnp.bfloat16
F32 = jnp.float32


def _dot(a, b):
    return jnp.dot(a, b, preferred_element_type=F32)


def _dot_nt(a, b):
    return lax.dot_general(a, b, (((1,), (1,)), ((), ())), preferred_element_type=F32)


def _sigmoid(x):
    return 1.0 / (1.0 + jnp.exp(-x))


def _softplus(x):
    return jnp.maximum(x, 0.0) + jnp.log(1.0 + jnp.exp(-jnp.abs(x)))


def _modulate(x, g, shift, scale):
    ms = jnp.mean(x * x, axis=-1, keepdims=True)
    return (x * lax.rsqrt(ms + EPS)) * (g * (1.0 + scale)) + shift


def _const_spec(shape):
    return pl.BlockSpec(shape, lambda *_: (0,) * len(shape), pipeline_mode=pl.Buffered(1))


def _ada_kernel(c_ref, w_ref, b_ref, o_ref):
    c = c_ref[...]
    cond = (c * _sigmoid(c)).astype(BF16)
    o_ref[...] = _dot(cond, w_ref[...].astype(BF16)) + b_ref[...]


def _ada(c, w, b):
    bsz, d = c.shape
    n = w.shape[1]
    return pl.pallas_call(
        _ada_kernel,
        grid=(n // ADA_TN,),
        in_specs=[
            pl.BlockSpec((bsz, d), lambda j: (0, 0)),
            pl.BlockSpec((d, ADA_TN), lambda j: (0, j)),
            pl.BlockSpec((1, ADA_TN), lambda j: (0, j)),
        ],
        out_specs=pl.BlockSpec((bsz, ADA_TN), lambda j: (0, j)),
        out_shape=jax.ShapeDtypeStruct((bsz, n), F32),
        compiler_params=pltpu.CompilerParams(
            dimension_semantics=("arbitrary",), vmem_limit_bytes=VMEM_LIMIT_BYTES),
        name="ada",
    )(c, w, b.reshape(1, n))


def _ffn_kernel(*refs, pre_mix, final_norm, n_chunks):
    it = iter(refs)
    x_ref = next(it)
    if pre_mix:
        osb_ref, ofx_ref, wout_ref, gate_mix_ref = next(it), next(it), next(it), next(it)
    shift_ref, scale_ref, gate_ref, g_ref = next(it), next(it), next(it), next(it)
    wg_ref, wu_ref, wd_ref = next(it), next(it), next(it)
    if final_norm:
        gf_ref = next(it)
    o_ref, h_scr, acc_scr = next(it), next(it), next(it)

    x = x_ref[...]
    if pre_mix:
        mix = (_dot(osb_ref[...], wout_ref[0:SB_WIDTH, :])
               + _dot(ofx_ref[...], wout_ref[SB_WIDTH:SB_WIDTH + FOX_WIDTH, :]))
        x = x + gate_mix_ref[...] * mix
    o_ref[...] = x
    h_scr[...] = _modulate(x, g_ref[...], shift_ref[...], scale_ref[...]).astype(BF16)
    acc_scr[...] = jnp.zeros_like(acc_scr)

    def body(ci, carry):
        h = h_scr[...]
        gt = _dot(h, wg_ref[ci])
        up = _dot(h, wu_ref[ci])
        a = (gt * _sigmoid(gt) * up).astype(BF16)
        acc_scr[...] += _dot(a, wd_ref[ci])
        return carry

    lax.fori_loop(0, n_chunks, body, 0)
    y = o_ref[...] + (0.5 * gate_ref[...]) * acc_scr[...]
    if final_norm:
        ms = jnp.mean(y * y, axis=-1, keepdims=True)
        y = (y * lax.rsqrt(ms + EPS)) * gf_ref[...]
    o_ref[...] = y


def _ffn(x, shift, scale, gate, g, wg, wu, wd, mix=None, g_final=None):
    bsz, s, d = x.shape
    n_chunks = wg.shape[0]
    tm = FFN_TM
    tok = pl.BlockSpec((None, tm, d), lambda b, i: (b, i, 0))
    per_batch = pl.BlockSpec((None, 1, d), lambda b, i: (b, 0, 0))
    args, specs = [x], [tok]
    if mix is not None:
        o_sb, o_fx, w_out, gate_mix = mix
        args += [o_sb, o_fx, w_out, gate_mix]
        specs += [pl.BlockSpec((None, tm, SB_WIDTH), lambda b, i: (b, i, 0)),
                  pl.BlockSpec((None, tm, FOX_WIDTH), lambda b, i: (b, i, 0)),
                  _const_spec(w_out.shape), per_batch]
    args += [shift, scale, gate, g.reshape(1, d), wg, wu, wd]
    specs += [per_batch, per_batch, per_batch, _const_spec((1, d)),
              _const_spec(wg.shape), _const_spec(wu.shape), _const_spec(wd.shape)]
    if g_final is not None:
        args.append(g_final.reshape(1, d))
        specs.append(_const_spec((1, d)))
    return pl.pallas_call(
        functools.partial(_ffn_kernel, pre_mix=mix is not None,
                          final_norm=g_final is not None, n_chunks=n_chunks),
        grid=(bsz, s // tm),
        in_specs=specs,
        out_specs=tok,
        out_shape=jax.ShapeDtypeStruct((bsz, s, d), F32),
        scratch_shapes=[pltpu.VMEM((tm, d), BF16), pltpu.VMEM((tm, d), F32)],
        compiler_params=pltpu.CompilerParams(
            dimension_semantics=("arbitrary", "arbitrary"), vmem_limit_bytes=VMEM_LIMIT_BYTES),
        name="ffn_mix" if mix is not None else "ffn",
    )(*args)


def _lane_cumsum(y):
    n = y.shape[-1]
    lane = lax.broadcasted_iota(jnp.int32, y.shape, y.ndim - 1)
    shift = 1
    while shift < n:
        y = y + jnp.where(lane >= shift, pltpu.roll(y, shift, axis=y.ndim - 1), 0.0)
        shift *= 2
    return y


def _proj_kernel(x_ref, shift_ref, scale_ref, g_ref, w_ref, wf_ref, bf_ref, gq_ref, gk_ref,
                 grp_ref, sbq_ref, sbk_ref, sbv_ref, fxq_ref, fxk_ref, fxv_ref, fxg_ref,
                 fcum_ref, h_scr, carry_scr):
    i = pl.program_id(1)
    x = x_ref[...]
    h_scr[...] = _modulate(x, g_ref[...], shift_ref[...], scale_ref[...]).astype(BF16)
    h = h_scr[...]
    w = SB_WIDTH

    def proj(idx):
        return _dot(h, w_ref[:, idx * w:(idx + 1) * w])

    def head_rms(p):
        ms = _dot((p * p).astype(BF16), grp_ref[...])
        return p * lax.rsqrt(ms + EPS)

    sbq_ref[...] = (proj(0) * QK_SCALE).astype(BF16)
    sbk_ref[...] = proj(1).astype(BF16)
    sbv_ref[...] = proj(2).astype(BF16)
    fxq_ref[...] = (head_rms(proj(3)) * (gq_ref[...] * QK_SCALE)).astype(BF16)
    fxk_ref[...] = (head_rms(proj(4)) * gk_ref[...]).astype(BF16)
    fxv_ref[...] = proj(5).astype(BF16)
    fxg_ref[...] = _sigmoid(proj(6)).astype(BF16)

    f = _dot(h, wf_ref[...])
    ft = jnp.transpose(f)[0:FOX_HEADS, :] + bf_ref[...]
    log_f = -_softplus(-ft)

    @pl.when(i == 0)
    def _():
        carry_scr[...] = jnp.zeros_like(carry_scr)

    fcum = _lane_cumsum(log_f) + carry_scr[:, 0:1]
    fcum_ref[...] = fcum
    carry_scr[...] = jnp.broadcast_to(fcum[:, fcum.shape[1] - 1:], carry_scr.shape)


def _in_proj(x, shift, scale, g, w_main, w_f, b_f, gq, gk, grp):
    bsz, s, d = x.shape
    tm = PROJ_TM
    tok = pl.BlockSpec((None, tm, d), lambda b, i: (b, i, 0))
    per_batch = pl.BlockSpec((None, 1, d), lambda b, i: (b, 0, 0))
    head_out = pl.BlockSpec((None, tm, SB_WIDTH), lambda b, i: (b, i, 0))
    head_shape = jax.ShapeDtypeStruct((bsz, s, SB_WIDTH), BF16)
    return pl.pallas_call(
        _proj_kernel,
        grid=(bsz, s // tm),
        in_specs=[tok, per_batch, per_batch, _const_spec((1, d)), _const_spec(w_main.shape),
                  _const_spec(w_f.shape), _const_spec(b_f.shape), _const_spec(gq.shape),
                  _const_spec(gk.shape), _const_spec(grp.shape)],
        out_specs=[head_out] * 7 + [pl.BlockSpec((None, FOX_HEADS, tm), lambda b, i: (b, 0, i))],
        out_shape=[head_shape] * 7 + [jax.ShapeDtypeStruct((bsz, FOX_HEADS, s), F32)],
        scratch_shapes=[pltpu.VMEM((tm, d), BF16), pltpu.VMEM((FOX_HEADS, LANES), F32)],
        compiler_params=pltpu.CompilerParams(
            dimension_semantics=("arbitrary", "arbitrary"), vmem_limit_bytes=VMEM_LIMIT_BYTES),
        name="in_proj",
    )(x, shift, scale, g.reshape(1, d), w_main, w_f, b_f, gq, gk, grp)


def _split_heads(x):
    lane = lax.broadcasted_iota(jnp.int32, x.shape, 1)
    zero = jnp.zeros_like(x)
    return jnp.where(lane < HEAD_DIM, x, zero), jnp.where(lane >= HEAD_DIM, x, zero)


def _fill_vexp(v_ref, vexp_ref, n_blocks, t):
    def body(j, carry):
        v0, v1 = _split_heads(v_ref[pl.ds(pl.multiple_of(j * t, t), t), :])
        vexp_ref[j, 0:t, :] = v0
        vexp_ref[j, t:2 * t, :] = v1
        return carry

    lax.fori_loop(0, n_blocks, body, 0)


def _per_head(lane, a0, a1):
    return jnp.where(lane < HEAD_DIM, a0, a1)


def _head_rmsnorm(o, lane):
    sq = o * o
    ss0 = jnp.sum(jnp.where(lane < HEAD_DIM, sq, 0.0), axis=1, keepdims=True)
    ss1 = jnp.sum(jnp.where(lane >= HEAD_DIM, sq, 0.0), axis=1, keepdims=True)
    inv = _per_head(lane, lax.rsqrt(ss0 * (1.0 / HEAD_DIM) + EPS),
                    lax.rsqrt(ss1 * (1.0 / HEAD_DIM) + EPS))
    return o * inv


def _sb_kernel(q_ref, k_ref, v_ref, g_ref, tri_ref, o_ref, vexp_scr, acc_scr, carry_scr):
    qi = pl.program_id(2)
    t = ATT_T

    @pl.when(qi == 0)
    def _():
        _fill_vexp(v_ref, vexp_scr, v_ref.shape[0] // t, t)

    q0, q1 = _split_heads(q_ref[...])
    row = lax.broadcasted_iota(jnp.int32, (t, t), 0)
    col = lax.broadcasted_iota(jnp.int32, (t, t), 1)
    strict = col < row

    def block(j, diagonal):
        k = k_ref[pl.ds(pl.multiple_of(j * t, t), t), :]
        ws = []
        for hd, qh in enumerate((q0, q1)):
            z = _dot_nt(qh, k)
            log_keep = -_softplus(z)
            if diagonal:
                log_keep = jnp.where(strict, log_keep, 0.0)
            hi = log_keep.astype(BF16)
            lo = (log_keep - hi.astype(F32)).astype(BF16)
            suffix = _dot(jnp.concatenate([hi, lo], axis=1), tri_ref[...])
            carry = carry_scr[hd]
            wgt = jnp.exp(z + suffix + carry)
            if diagonal:
                wgt = jnp.where(strict, wgt, 0.0)
            carry_scr[hd] = carry + suffix[:, 0:1]
            ws.append(wgt.astype(BF16))
        acc_scr[...] += _dot(jnp.concatenate(ws, axis=1), vexp_scr[j])

    acc_scr[...] = jnp.zeros_like(acc_scr)
    carry_scr[...] = jnp.zeros_like(carry_scr)
    block(qi, True)

    def live():
        return (jnp.max(carry_scr[...]) >= F32_EXP_ZERO_BELOW).astype(jnp.int32)

    def cond(state):
        j, go = state
        return jnp.logical_and(j >= 0, go > 0)

    def body(state):
        j, _ = state
        block(j, False)
        return j - 1, live()

    lax.while_loop(cond, body, (qi - 1, live()))

    lane = lax.broadcasted_iota(jnp.int32, (t, LANES), 1)
    o_ref[...] = (_head_rmsnorm(acc_scr[...], lane) * g_ref[...]).astype(o_ref.dtype)


def _sb_attention(q, k, v, g, tri):
    bsz, s, width = q.shape
    t = ATT_T
    n_tiles = width // LANES
    q_spec = pl.BlockSpec((None, t, LANES), lambda b, hp, i: (b, i, hp))
    kv_spec = pl.BlockSpec((None, s, LANES), lambda b, hp, i: (b, 0, hp))
    return pl.pallas_call(
        _sb_kernel,
        grid=(bsz, n_tiles, s // t),
        in_specs=[q_spec, kv_spec, kv_spec,
                  pl.BlockSpec((1, LANES), lambda b, hp, i: (0, hp)), _const_spec(tri.shape)],
        out_specs=q_spec,
        out_shape=jax.ShapeDtypeStruct((bsz, s, width), BF16),
        scratch_shapes=[pltpu.VMEM((s // t, 2 * t, LANES), BF16), pltpu.VMEM((t, LANES), F32),
                        pltpu.VMEM((HEADS_PER_TILE, t, 1), F32)],
        compiler_params=pltpu.CompilerParams(
            dimension_semantics=("arbitrary", "arbitrary", "arbitrary"),
            vmem_limit_bytes=VMEM_LIMIT_BYTES),
        name="sb_attn",
    )(q, k, v, g.reshape(1, width), tri)


def _fox_kernel(q_ref, k_ref, v_ref, f_ref, gate_ref, g_ref, o_ref, vexp_scr, acc_scr, m_scr,
                l_scr):
    qi = pl.program_id(2)
    t = ATT_T

    @pl.when(qi == 0)
    def _():
        _fill_vexp(v_ref, vexp_scr, v_ref.shape[0] // t, t)

    q0, q1 = _split_heads(q_ref[...])
    row = lax.broadcasted_iota(jnp.int32, (t, t), 0)
    col = lax.broadcasted_iota(jnp.int32, (t, t), 1)
    causal = col <= row
    lane = lax.broadcasted_iota(jnp.int32, (t, LANES), 1)

    acc_scr[...] = jnp.zeros_like(acc_scr)
    m_scr[...] = jnp.full_like(m_scr, -jnp.inf)
    l_scr[...] = jnp.zeros_like(l_scr)

    def block(j, diagonal):
        start = pl.multiple_of(j * t, t)
        k = k_ref[pl.ds(start, t), :]
        f_keys = f_ref[:, pl.ds(start, t)]
        ps, alphas = [], []
        for hd, qh in enumerate((q0, q1)):
            sc = _dot_nt(qh, k) - f_keys[hd:hd + 1, :]
            if diagonal:
                sc = jnp.where(causal, sc, -jnp.inf)
            m_old = m_scr[hd]
            m_new = jnp.maximum(m_old, jnp.max(sc, axis=1, keepdims=True))
            alpha = jnp.exp(m_old - m_new)
            p = jnp.exp(sc - m_new)
            l_scr[hd] = alpha * l_scr[hd] + jnp.sum(p, axis=1, keepdims=True)
            m_scr[hd] = m_new
            ps.append(p.astype(BF16))
            alphas.append(alpha)
        pv = _dot(jnp.concatenate(ps, axis=1), vexp_scr[j])
        acc_scr[...] = acc_scr[...] * _per_head(lane, alphas[0], alphas[1]) + pv

    def body(j, carry):
        block(j, False)
        return carry

    lax.fori_loop(0, qi, body, 0)
    block(qi, True)

    o = acc_scr[...] * _per_head(lane, 1.0 / l_scr[0], 1.0 / l_scr[1])
    o = _head_rmsnorm(o, lane) * g_ref[...] * gate_ref[...].astype(F32)
    o_ref[...] = o.astype(o_ref.dtype)


def _fox_attention(q, k, v, fcum, gate, g):
    bsz, s, width = q.shape
    t = ATT_T
    n_tiles = width // LANES
    q_spec = pl.BlockSpec((None, t, LANES), lambda b, hp, i: (b, i, hp))
    kv_spec = pl.BlockSpec((None, s, LANES), lambda b, hp, i: (b, 0, hp))
    f_tiles = fcum.reshape(bsz, n_tiles, HEADS_PER_TILE, s)
    return pl.pallas_call(
        _fox_kernel,
        grid=(bsz, n_tiles, s // t),
        in_specs=[q_spec, kv_spec, kv_spec,
                  pl.BlockSpec((None, None, HEADS_PER_TILE, s), lambda b, hp, i: (b, hp, 0, 0)),
                  q_spec, pl.BlockSpec((1, LANES), lambda b, hp, i: (0, hp))],
        out_specs=q_spec,
        out_shape=jax.ShapeDtypeStruct((bsz, s, width), BF16),
        scratch_shapes=[pltpu.VMEM((s // t, 2 * t, LANES), BF16), pltpu.VMEM((t, LANES), F32),
                        pltpu.VMEM((HEADS_PER_TILE, t, 1), F32),
                        pltpu.VMEM((HEADS_PER_TILE, t, 1), F32)],
        compiler_params=pltpu.CompilerParams(
            dimension_semantics=("arbitrary", "arbitrary", "arbitrary"),
            vmem_limit_bytes=VMEM_LIMIT_BYTES),
        name="fox_attn",
    )(q, k, v, f_tiles, gate, g.reshape(1, width))


def _ffn_weights(w_up, w_down):
    d = w_up.shape[0]
    nc = FFN_HIDDEN // FFN_HC
    wg = w_up[:, :FFN_HIDDEN].astype(BF16).reshape(d, nc, FFN_HC).transpose(1, 0, 2)
    wu = w_up[:, FFN_HIDDEN:].astype(BF16).reshape(d, nc, FFN_HC).transpose(1, 0, 2)
    wd = w_down.astype(BF16).reshape(nc, FFN_HC, d)
    return wg, wu, wd


def kernel(x, c, w_ada, b_ada, g_ffn1, w_ffn1_up, w_ffn1_down, g_mix, w_in, b_forget, g_fox_q,
           g_fox_k, g_sb_out, g_fox_out, w_out, g_ffn2, w_ffn2_up, w_ffn2_down, g_final):
    bsz, s, d = x.shape
    depth = w_ada.shape[0]
    t = ATT_T

    head_of = jnp.arange(FOX_WIDTH) // HEAD_DIM
    grp = jnp.where(head_of[:, None] == head_of[None, :], 1.0 / HEAD_DIM, 0.0).astype(BF16)
    key = jnp.arange(t)
    tri1 = (key[:, None] >= key[None, :]).astype(BF16)
    tri = jnp.concatenate([tri1, tri1], axis=0)

    n_main = 3 * SB_WIDTH + 4 * FOX_WIDTH
    for l in range(depth):
        ada = _ada(c, w_ada[l], b_ada[l]).reshape(bsz, N_SUBLAYERS, 3, 1, d)
        shift, scale, gate = ada[:, :, 0], ada[:, :, 1], ada[:, :, 2]

        x = _ffn(x, shift[:, 0], scale[:, 0], gate[:, 0], g_ffn1[l],
                 *_ffn_weights(w_ffn1_up[l], w_ffn1_down[l]))

        w_main = w_in[l][:, :n_main].astype(BF16)
        w_f = jnp.pad(w_in[l][:, n_main:], ((0, 0), (0, LANES - FOX_HEADS))).astype(BF16)
        sbq, sbk, sbv, fxq, fxk, fxv, fxg, fcum = _in_proj(
            x, shift[:, 1], scale[:, 1], g_mix[l], w_main, w_f,
            b_forget[l].reshape(FOX_HEADS, 1),
            jnp.tile(g_fox_q[l], FOX_HEADS).reshape(1, FOX_WIDTH),
            jnp.tile(g_fox_k[l], FOX_HEADS).reshape(1, FOX_WIDTH), grp)

        o_sb = _sb_attention(sbq, sbk, sbv, g_sb_out[l], tri)
        o_fx = _fox_attention(fxq, fxk, fxv, fcum, fxg, g_fox_out[l])

        x = _ffn(x, shift[:, 2], scale[:, 2], gate[:, 2], g_ffn2[l],
                 *_ffn_weights(w_ffn2_up[l], w_ffn2_down[l]),
                 mix=(o_sb, o_fx, w_out[l].astype(BF16), gate[:, 1]),
                 g_final=g_final if l == depth - 1 else None)
    return x
```

```python
import functools

import numpy as np

import jax
import jax.numpy as jnp
from jax import lax
from jax.experimental import pallas as pl
from jax.experimental.pallas import tpu as pltpu

D_MODEL = 1024
HEAD_DIM = 64
SB_HEADS = 8
FOX_HEADS = 8
SB_WIDTH = SB_HEADS * HEAD_DIM
FOX_WIDTH = FOX_HEADS * HEAD_DIM
FFN_HIDDEN = 2816
N_SUBLAYERS = 3
EPS = 1e-6
QK_SCALE = HEAD_DIM ** -0.5

LANES = 128
HEADS_PER_TILE = LANES // HEAD_DIM
VMEM_LIMIT_BYTES = 56 * 1024 * 1024

ADA_TN = 1152
FFN_TM = 512
FFN_HC = 256
PROJ_TM = 512
ATT_T = 256
PIPE_LAG = 2

F32_EXP_ZERO_BELOW = -104.0
MASKED = -1e30

BF16 = jnp.bfloat16
F32 = jnp.float32


def _dot(a, b):
    return jnp.dot(a, b, preferred_element_type=F32)


def _dot_nt(a, b):
    return lax.dot_general(a, b, (((1,), (1,)), ((), ())), preferred_element_type=F32)


def _sigmoid(x):
    return 1.0 / (1.0 + jnp.exp(-x))


def _softplus(x):
    return jnp.maximum(x, 0.0) + jnp.log(1.0 + jnp.exp(-jnp.abs(x)))


def _modulate(x, g, shift, scale):
    ms = jnp.mean(x * x, axis=-1, keepdims=True)
    return (x * lax.rsqrt(ms + EPS)) * (g * (1.0 + scale)) + shift


def _const_spec(shape):
    return pl.BlockSpec(shape, lambda *_: (0,) * len(shape), pipeline_mode=pl.Buffered(1))


def _ada_kernel(c_ref, w_ref, b_ref, o_ref):
    c = c_ref[...]
    cond = (c * _sigmoid(c)).astype(BF16)
    o_ref[...] = _dot(cond, w_ref[...].astype(BF16)) + b_ref[...]


def _ada(c, w, b):
    bsz, d = c.shape
    n = w.shape[1]
    return pl.pallas_call(
        _ada_kernel,
        grid=(n // ADA_TN,),
        in_specs=[
            pl.BlockSpec((bsz, d), lambda j: (0, 0)),
            pl.BlockSpec((d, ADA_TN), lambda j: (0, j)),
            pl.BlockSpec((1, ADA_TN), lambda j: (0, j)),
        ],
        out_specs=pl.BlockSpec((bsz, ADA_TN), lambda j: (0, j)),
        out_shape=jax.ShapeDtypeStruct((bsz, n), F32),
        compiler_params=pltpu.CompilerParams(
            dimension_semantics=("arbitrary",), vmem_limit_bytes=VMEM_LIMIT_BYTES),
        name="ada",
    )(c, w, b.reshape(1, n))


def _ffn_kernel(*refs, pre_mix, final_norm, n_chunks):
    it = iter(refs)
    x_ref = next(it)
    if pre_mix:
        osb_ref, ofx_ref, wout_ref, gate_mix_ref = next(it), next(it), next(it), next(it)
    shift_ref, scale_ref, gate_ref, g_ref = next(it), next(it), next(it), next(it)
    wg_ref, wu_ref, wd_ref = next(it), next(it), next(it)
    if final_norm:
        gf_ref = next(it)
    o_ref, h_scr, acc_scr = next(it), next(it), next(it)

    x = x_ref[...]
    if pre_mix:
        mix = (_dot(osb_ref[...], wout_ref[0:SB_WIDTH, :])
               + _dot(ofx_ref[...], wout_ref[SB_WIDTH:SB_WIDTH + FOX_WIDTH, :]))
        x = x + gate_mix_ref[...] * mix
    o_ref[...] = x
    h_scr[...] = _modulate(x, g_ref[...], shift_ref[...], scale_ref[...]).astype(BF16)
    acc_scr[...] = jnp.zeros_like(acc_scr)

    def body(ci, carry):
        h = h_scr[...]
        gt = _dot(h, wg_ref[ci])
        up = _dot(h, wu_ref[ci])
        a = (gt * _sigmoid(gt) * up).astype(BF16)
        acc_scr[...] += _dot(a, wd_ref[ci])
        return carry

    lax.fori_loop(0, n_chunks, body, 0)
    y = o_ref[...] + (0.5 * gate_ref[...]) * acc_scr[...]
    if final_norm:
        ms = jnp.mean(y * y, axis=-1, keepdims=True)
        y = (y * lax.rsqrt(ms + EPS)) * gf_ref[...]
    o_ref[...] = y


def _ffn(x, shift, scale, gate, g, wg, wu, wd, mix=None, g_final=None):
    bsz, s, d = x.shape
    n_chunks = wg.shape[0]
    tm = FFN_TM
    tok = pl.BlockSpec((None, tm, d), lambda b, i: (b, i, 0))
    per_batch = pl.BlockSpec((None, 1, d), lambda b, i: (b, 0, 0))
    args, specs = [x], [tok]
    if mix is not None:
        o_sb, o_fx, w_out, gate_mix = mix
        args += [o_sb, o_fx, w_out, gate_mix]
        specs += [pl.BlockSpec((None, tm, SB_WIDTH), lambda b, i: (b, i, 0)),
                  pl.BlockSpec((None, tm, FOX_WIDTH), lambda b, i: (b, i, 0)),
                  _const_spec(w_out.shape), per_batch]
    args += [shift, scale, gate, g.reshape(1, d), wg, wu, wd]
    specs += [per_batch, per_batch, per_batch, _const_spec((1, d)),
              _const_spec(wg.shape), _const_spec(wu.shape), _const_spec(wd.shape)]
    if g_final is not None:
        args.append(g_final.reshape(1, d))
        specs.append(_const_spec((1, d)))
    return pl.pallas_call(
        functools.partial(_ffn_kernel, pre_mix=mix is not None,
                          final_norm=g_final is not None, n_chunks=n_chunks),
        grid=(bsz, s // tm),
        in_specs=specs,
        out_specs=tok,
        out_shape=jax.ShapeDtypeStruct((bsz, s, d), F32),
        scratch_shapes=[pltpu.VMEM((tm, d), BF16), pltpu.VMEM((tm, d), F32)],
        compiler_params=pltpu.CompilerParams(
            dimension_semantics=("arbitrary", "arbitrary"), vmem_limit_bytes=VMEM_LIMIT_BYTES),
        name="ffn_mix" if mix is not None else "ffn",
    )(*args)


def _lane_cumsum(y):
    n = y.shape[-1]
    lane = lax.broadcasted_iota(jnp.int32, y.shape, y.ndim - 1)
    shift = 1
    while shift < n:
        y = y + jnp.where(lane >= shift, pltpu.roll(y, shift, axis=y.ndim - 1), 0.0)
        shift *= 2
    return y


def _proj_kernel(x_ref, shift_ref, scale_ref, g_ref, w_ref, wf_ref, bf_ref, gq_ref, gk_ref,
                 grp_ref, sbq_ref, sbk_ref, sbv_ref, fxq_ref, fxk_ref, fxv_ref, fxg_ref,
                 fcum_ref, h_scr, carry_scr):
    i = pl.program_id(1)
    x = x_ref[...]
    h_scr[...] = _modulate(x, g_ref[...], shift_ref[...], scale_ref[...]).astype(BF16)
    h = h_scr[...]
    w = SB_WIDTH

    def proj(idx):
        return _dot(h, w_ref[:, idx * w:(idx + 1) * w])

    def head_rms(p):
        ms = _dot((p * p).astype(BF16), grp_ref[...])
        return p * lax.rsqrt(ms + EPS)

    sbq_ref[...] = (proj(0) * QK_SCALE).astype(BF16)
    sbk_ref[...] = proj(1).astype(BF16)
    sbv_ref[...] = proj(2).astype(BF16)
    fxq_ref[...] = (head_rms(proj(3)) * (gq_ref[...] * QK_SCALE)).astype(BF16)
    fxk_ref[...] = (head_rms(proj(4)) * gk_ref[...]).astype(BF16)
    fxv_ref[...] = proj(5).astype(BF16)
    fxg_ref[...] = _sigmoid(proj(6)).astype(BF16)

    f = _dot(h, wf_ref[...])
    ft = jnp.transpose(f)[0:FOX_HEADS, :] + bf_ref[...]
    log_f = -_softplus(-ft)

    @pl.when(i == 0)
    def _():
        carry_scr[...] = jnp.zeros_like(carry_scr)

    fcum = _lane_cumsum(log_f) + carry_scr[:, 0:1]
    fcum_ref[...] = fcum
    carry_scr[...] = jnp.broadcast_to(fcum[:, fcum.shape[1] - 1:], carry_scr.shape)


def _in_proj(x, shift, scale, g, w_main, w_f, b_f, gq, gk, grp):
    bsz, s, d = x.shape
    tm = PROJ_TM
    tok = pl.BlockSpec((None, tm, d), lambda b, i: (b, i, 0))
    per_batch = pl.BlockSpec((None, 1, d), lambda b, i: (b, 0, 0))
    head_out = pl.BlockSpec((None, tm, SB_WIDTH), lambda b, i: (b, i, 0))
    head_shape = jax.ShapeDtypeStruct((bsz, s, SB_WIDTH), BF16)
    return pl.pallas_call(
        _proj_kernel,
        grid=(bsz, s // tm),
        in_specs=[tok, per_batch, per_batch, _const_spec((1, d)), _const_spec(w_main.shape),
                  _const_spec(w_f.shape), _const_spec(b_f.shape), _const_spec(gq.shape),
                  _const_spec(gk.shape), _const_spec(grp.shape)],
        out_specs=[head_out] * 7 + [pl.BlockSpec((None, FOX_HEADS, tm), lambda b, i: (b, 0, i))],
        out_shape=[head_shape] * 7 + [jax.ShapeDtypeStruct((bsz, FOX_HEADS, s), F32)],
        scratch_shapes=[pltpu.VMEM((tm, d), BF16), pltpu.VMEM((FOX_HEADS, LANES), F32)],
        compiler_params=pltpu.CompilerParams(
            dimension_semantics=("arbitrary", "arbitrary"), vmem_limit_bytes=VMEM_LIMIT_BYTES),
        name="in_proj",
    )(x, shift, scale, g.reshape(1, d), w_main, w_f, b_f, gq, gk, grp)


def _split_heads(x):
    lane = lax.broadcasted_iota(jnp.int32, x.shape, 1)
    zero = jnp.zeros_like(x)
    return jnp.where(lane < HEAD_DIM, x, zero), jnp.where(lane >= HEAD_DIM, x, zero)


def _fill_vexp(v_ref, vexp_ref, n_blocks, t, with_ones):
    def body(j, carry):
        v0, v1 = _split_heads(v_ref[pl.ds(pl.multiple_of(j * t, t), t), :])
        vexp_ref[j, 0:t, 0:LANES] = v0
        vexp_ref[j, t:2 * t, 0:LANES] = v1
        if with_ones:
            lane = lax.broadcasted_iota(jnp.int32, (t, LANES), 1)
            vexp_ref[j, 0:t, LANES:2 * LANES] = (lane < HEAD_DIM).astype(F32).astype(vexp_ref.dtype)
            vexp_ref[j, t:2 * t, LANES:2 * LANES] = (lane >= HEAD_DIM).astype(F32).astype(
                vexp_ref.dtype)
        return carry

    lax.fori_loop(0, n_blocks, body, 0)


def _per_head(lane, a0, a1):
    return jnp.where(lane < HEAD_DIM, a0, a1)


def _lane_tile(a, width):
    return jnp.concatenate([a] * (width // LANES), axis=1)


def _head_rmsnorm(o, lane):
    sq = o * o
    ss0 = jnp.sum(jnp.where(lane < HEAD_DIM, sq, 0.0), axis=1, keepdims=True)
    ss1 = jnp.sum(jnp.where(lane >= HEAD_DIM, sq, 0.0), axis=1, keepdims=True)
    inv = _per_head(lane, lax.rsqrt(ss0 * (1.0 / HEAD_DIM) + EPS),
                    lax.rsqrt(ss1 * (1.0 / HEAD_DIM) + EPS))
    return o * inv


def _task_table(tasks, n_q):
    idle = [(0, 0, n_q, 0)] * PIPE_LAG
    return np.asarray(idle + list(tasks) + idle, dtype=np.int32).T.copy()


def _score_stage(tab_ref, i, q_ref, k_ref, s_scr, t):
    q_blk, k_blk = tab_ref[0, i], tab_ref[1, i]
    q0, q1 = _split_heads(q_ref[pl.ds(pl.multiple_of(q_blk * t, t), t), :])
    k = k_ref[pl.ds(pl.multiple_of(k_blk * t, t), t), :]
    s_scr[0] = _dot_nt(q0, k)
    s_scr[1] = _dot_nt(q1, k)


def _causal_bias(t, strict):
    row = lax.broadcasted_iota(jnp.int32, (t, t), 0)
    col = lax.broadcasted_iota(jnp.int32, (t, t), 1)
    keep = col < row if strict else col <= row
    return jnp.where(keep, 0.0, MASKED).astype(F32)


def _sb_kernel(tab_ref, q_ref, k_ref, v_ref, g_ref, tri_ref, o_ref, vexp_scr, bias_scr, acc_scr,
               carry_scr, z_scr, zc_scr, suf_scr):
    t = ATT_T
    n_q = q_ref.shape[0] // t
    n_iter = tab_ref.shape[1] - PIPE_LAG

    _fill_vexp(v_ref, vexp_scr, n_q, t, with_ones=False)
    bias_scr[0] = jnp.zeros((t, t), F32)
    bias_scr[1] = _causal_bias(t, strict=True)
    acc_scr[...] = jnp.zeros_like(acc_scr)
    carry_scr[...] = jnp.zeros_like(carry_scr)
    z_scr[...] = jnp.zeros_like(z_scr)
    zc_scr[...] = jnp.zeros_like(zc_scr)
    suf_scr[...] = jnp.zeros_like(suf_scr)

    def keep_stage(slot, mask_idx):
        bias = bias_scr[mask_idx]
        for hd in range(HEADS_PER_TILE):
            z = z_scr[hd] + bias
            log_keep = -_softplus(z)
            hi = log_keep.astype(BF16)
            lo = (log_keep - hi.astype(F32)).astype(BF16)
            suf_scr[hd] = _dot(jnp.concatenate([hi, lo], axis=1), tri_ref[...])
            carry = carry_scr[slot, hd]
            zc_scr[hd] = z + _lane_tile(carry, t)
            carry_scr[slot, hd] = carry + jnp.sum(log_keep, axis=1, keepdims=True)

    def value_stage(k_blk, slot):
        ws = [jnp.exp(zc_scr[hd] + suf_scr[hd]).astype(BF16) for hd in range(HEADS_PER_TILE)]
        acc_scr[slot] += _dot(jnp.concatenate(ws, axis=1), vexp_scr[k_blk])

    def body(i, carry):
        value_stage(tab_ref[1, i], tab_ref[2, i])
        keep_stage(tab_ref[2, i + 1], tab_ref[3, i + 1])
        _score_stage(tab_ref, i + 2, q_ref, k_ref, z_scr, t)
        return carry

    lax.fori_loop(0, n_iter, body, 0)

    def tail(q_blk, carry):
        def live():
            return (jnp.max(carry_scr[q_blk]) >= F32_EXP_ZERO_BELOW).astype(jnp.int32)

        def cond(state):
            k_blk, go = state
            return jnp.logical_and(k_blk >= 0, go > 0)

        def step(state):
            k_blk, _ = state
            q0, q1 = _split_heads(q_ref[pl.ds(pl.multiple_of(q_blk * t, t), t), :])
            k = k_ref[pl.ds(pl.multiple_of(k_blk * t, t), t), :]
            z_scr[0] = _dot_nt(q0, k)
            z_scr[1] = _dot_nt(q1, k)
            keep_stage(q_blk, 0)
            value_stage(k_blk, q_blk)
            return k_blk - 1, live()

        lax.while_loop(cond, step, (q_blk - 2, live()))
        return carry

    lax.fori_loop(2, n_q, tail, 0)

    lane = lax.broadcasted_iota(jnp.int32, (t, LANES), 1)
    for q_blk in range(n_q):
        o = _head_rmsnorm(acc_scr[q_blk], lane) * g_ref[...]
        o_ref[q_blk * t:(q_blk + 1) * t, :] = o.astype(o_ref.dtype)


def _sb_attention(q, k, v, g, tri):
    bsz, s, width = q.shape
    t = ATT_T
    n_q = s // t
    n_tiles = width // LANES
    tasks = []
    for qb in range(n_q):
        tasks.append((qb, qb, qb, 1))
        if qb >= 1:
            tasks.append((qb, qb - 1, qb, 0))
    table = _task_table(tasks, n_q)
    seq_spec = pl.BlockSpec((None, s, LANES), lambda b, hp: (b, 0, hp))
    return pl.pallas_call(
        _sb_kernel,
        grid=(bsz, n_tiles),
        in_specs=[pl.BlockSpec(memory_space=pltpu.SMEM), seq_spec, seq_spec, seq_spec,
                  pl.BlockSpec((1, LANES), lambda b, hp: (0, hp)), _const_spec(tri.shape)],
        out_specs=seq_spec,
        out_shape=jax.ShapeDtypeStruct((bsz, s, width), BF16),
        scratch_shapes=[pltpu.VMEM((n_q, 2 * t, LANES), BF16),
                        pltpu.VMEM((2, t, t), F32),
                        pltpu.VMEM((n_q + 1, t, LANES), F32),
                        pltpu.VMEM((n_q + 1, HEADS_PER_TILE, t, LANES), F32),
                        pltpu.VMEM((HEADS_PER_TILE, t, t), F32),
                        pltpu.VMEM((HEADS_PER_TILE, t, t), F32),
                        pltpu.VMEM((HEADS_PER_TILE, t, t), F32)],
        compiler_params=pltpu.CompilerParams(
            dimension_semantics=("arbitrary", "arbitrary"), vmem_limit_bytes=VMEM_LIMIT_BYTES),
        name="sb_attn",
    )(jnp.asarray(table), q, k, v, g.reshape(1, width), tri)


def _fox_kernel(tab_ref, q_ref, k_ref, v_ref, f_ref, gate_ref, g_ref, o_ref, vexp_scr, bias_scr,
                acc_scr, m_scr, s_scr, p_scr, alpha_scr):
    t = ATT_T
    n_q = q_ref.shape[0] // t
    n_iter = tab_ref.shape[1] - PIPE_LAG
    lane = lax.broadcasted_iota(jnp.int32, (t, LANES), 1)

    _fill_vexp(v_ref, vexp_scr, n_q, t, with_ones=True)
    bias_scr[0] = jnp.zeros((t, t), F32)
    bias_scr[1] = _causal_bias(t, strict=False)
    acc_scr[...] = jnp.zeros_like(acc_scr)
    m_scr[...] = jnp.full_like(m_scr, -jnp.inf)
    s_scr[...] = jnp.zeros_like(s_scr)
    p_scr[...] = jnp.zeros_like(p_scr)
    alpha_scr[...] = jnp.zeros_like(alpha_scr)

    def softmax_stage(k_blk, slot, mask_idx):
        f_keys = f_ref[:, pl.ds(pl.multiple_of(k_blk * t, t), t)]
        bias = bias_scr[mask_idx]
        alphas = []
        for hd in range(HEADS_PER_TILE):
            sc = s_scr[hd] - f_keys[hd:hd + 1, :] + bias
            m_old = m_scr[slot, hd]
            m_new = jnp.maximum(m_old, jnp.max(sc, axis=1, keepdims=True))
            alphas.append(jnp.exp(m_old - m_new))
            m_scr[slot, hd] = m_new
            p_scr[:, hd * t:(hd + 1) * t] = jnp.exp(sc - _lane_tile(m_new, t)).astype(BF16)
        alpha_scr[...] = _per_head(lane, alphas[0], alphas[1])

    def value_stage(k_blk, slot):
        pv = _dot(p_scr[...], vexp_scr[k_blk])
        acc_scr[slot] = acc_scr[slot] * _lane_tile(alpha_scr[...], 2 * LANES) + pv

    def body(i, carry):
        value_stage(tab_ref[1, i], tab_ref[2, i])
        softmax_stage(tab_ref[1, i + 1], tab_ref[2, i + 1], tab_ref[3, i + 1])
        _score_stage(tab_ref, i + 2, q_ref, k_ref, s_scr, t)
        return carry

    lax.fori_loop(0, n_iter, body, 0)

    for q_blk in range(n_q):
        rows = slice(q_blk * t, (q_blk + 1) * t)
        acc = acc_scr[q_blk]
        o = acc[:, 0:LANES] / acc[:, LANES:2 * LANES]
        o = _head_rmsnorm(o, lane) * g_ref[...] * gate_ref[rows, :].astype(F32)
        o_ref[rows, :] = o.astype(o_ref.dtype)


def _fox_attention(q, k, v, fcum, gate, g):
    bsz, s, width = q.shape
    t = ATT_T
    n_q = s // t
    n_tiles = width // LANES
    tasks = [(qb, kb, qb, int(kb == qb)) for qb in range(n_q) for kb in range(qb + 1)]
    table = _task_table(tasks, n_q)
    seq_spec = pl.BlockSpec((None, s, LANES), lambda b, hp: (b, 0, hp))
    f_tiles = fcum.reshape(bsz, n_tiles, HEADS_PER_TILE, s)
    return pl.pallas_call(
        _fox_kernel,
        grid=(bsz, n_tiles),
        in_specs=[pl.BlockSpec(memory_space=pltpu.SMEM), seq_spec, seq_spec, seq_spec,
                  pl.BlockSpec((None, None, HEADS_PER_TILE, s), lambda b, hp: (b, hp, 0, 0)),
                  seq_spec, pl.BlockSpec((1, LANES), lambda b, hp: (0, hp))],
        out_specs=seq_spec,
        out_shape=jax.ShapeDtypeStruct((bsz, s, width), BF16),
        scratch_shapes=[pltpu.VMEM((n_q, 2 * t, 2 * LANES), BF16),
                        pltpu.VMEM((2, t, t), F32),
                        pltpu.VMEM((n_q + 1, t, 2 * LANES), F32),
                        pltpu.VMEM((n_q + 1, HEADS_PER_TILE, t, LANES), F32),
                        pltpu.VMEM((HEADS_PER_TILE, t, t), F32),
                        pltpu.VMEM((t, HEADS_PER_TILE * t), BF16),
                        pltpu.VMEM((t, LANES), F32)],
        compiler_params=pltpu.CompilerParams(
            dimension_semantics=("arbitrary", "arbitrary"), vmem_limit_bytes=VMEM_LIMIT_BYTES),
        name="fox_attn",
    )(jnp.asarray(table), q, k, v, f_tiles, gate, g.reshape(1, width))


def _ffn_weights(w_up, w_down):
    d = w_up.shape[0]
    nc = FFN_HIDDEN // FFN_HC
    wg = w_up[:, :FFN_HIDDEN].astype(BF16).reshape(d, nc, FFN_HC).transpose(1, 0, 2)
    wu = w_up[:, FFN_HIDDEN:].astype(BF16).reshape(d, nc, FFN_HC).transpose(1, 0, 2)
    wd = w_down.astype(BF16).reshape(nc, FFN_HC, d)
    return wg, wu, wd


def kernel(x, c, w_ada, b_ada, g_ffn1, w_ffn1_up, w_ffn1_down, g_mix, w_in, b_forget, g_fox_q,
           g_fox_k, g_sb_out, g_fox_out, w_out, g_ffn2, w_ffn2_up, w_ffn2_down, g_final):
    bsz, s, d = x.shape
    depth = w_ada.shape[0]
    t = ATT_T

    head_of = jnp.arange(FOX_WIDTH) // HEAD_DIM
    grp = jnp.where(head_of[:, None] == head_of[None, :], 1.0 / HEAD_DIM, 0.0).astype(BF16)
    key = jnp.arange(t)
    tri1 = (key[:, None] >= key[None, :]).astype(BF16)
    tri = jnp.concatenate([tri1, tri1], axis=0)

    n_main = 3 * SB_WIDTH + 4 * FOX_WIDTH
    for l in range(depth):
        ada = _ada(c, w_ada[l], b_ada[l]).reshape(bsz, N_SUBLAYERS, 3, 1, d)
        shift, scale, gate = ada[:, :, 0], ada[:, :, 1], ada[:, :, 2]

        x = _ffn(x, shift[:, 0], scale[:, 0], gate[:, 0], g_ffn1[l],
                 *_ffn_weights(w_ffn1_up[l], w_ffn1_down[l]))

        w_main = w_in[l][:, :n_main].astype(BF16)
        w_f = jnp.pad(w_in[l][:, n_main:], ((0, 0), (0, LANES - FOX_HEADS))).astype(BF16)
        sbq, sbk, sbv, fxq, fxk, fxv, fxg, fcum = _in_proj(
            x, shift[:, 1], scale[:, 1], g_mix[l], w_main, w_f,
            b_forget[l].reshape(FOX_HEADS, 1),
            jnp.tile(g_fox_q[l], FOX_HEADS).reshape(1, FOX_WIDTH),
            jnp.tile(g_fox_k[l], FOX_HEADS).reshape(1, FOX_WIDTH), grp)

        o_sb = _sb_attention(sbq, sbk, sbv, g_sb_out[l], tri)
        o_fx = _fox_attention(fxq, fxk, fxv, fcum, fxg, g_fox_out[l])

        x = _ffn(x, shift[:, 2], scale[:, 2], gate[:, 2], g_ffn2[l],
                 *_ffn_weights(w_ffn2_up[l], w_ffn2_down[l]),
                 mix=(o_sb, o_fx, w_out[l].astype(BF16), gate[:, 1]),
                 g_final=g_final if l == depth - 1 else None)
    return x
```

```python
import functools

import numpy as np

import jax
import jax.numpy as jnp
from jax import lax
from jax.experimental import pallas as pl
from jax.experimental.pallas import tpu as pltpu

D_MODEL = 1024
HEAD_DIM = 64
SB_HEADS = 8
FOX_HEADS = 8
SB_WIDTH = SB_HEADS * HEAD_DIM
FOX_WIDTH = FOX_HEADS * HEAD_DIM
FFN_HIDDEN = 2816
N_SUBLAYERS = 3
EPS = 1e-6
QK_SCALE = HEAD_DIM ** -0.5

LANES = 128
HEADS_PER_TILE = LANES // HEAD_DIM
VMEM_LIMIT_BYTES = 56 * 1024 * 1024

ADA_TN = 1152
FFN_TM = 512
FFN_HC = 256
PROJ_TM = 512
SB_T = 256
FOX_T = 256
ATT_PAIRS = 2
PIPE_LAG = 2

LOG2E = 1.4426950408889634
F32_EXP2_ZERO_BELOW = -150.0
MASKED = -1e30

BF16 = jnp.bfloat16
F32 = jnp.float32


def _dot(a, b):
    return jnp.dot(a, b, preferred_element_type=F32)


def _dot_nt(a, b):
    return lax.dot_general(a, b, (((1,), (1,)), ((), ())), preferred_element_type=F32)


def _sigmoid(x):
    return 1.0 / (1.0 + jnp.exp(-x))


def _softplus(x):
    return jnp.maximum(x, 0.0) + jnp.log(1.0 + jnp.exp(-jnp.abs(x)))


def _modulate(x, g, shift, scale):
    ms = jnp.mean(x * x, axis=-1, keepdims=True)
    return (x * lax.rsqrt(ms + EPS)) * (g * (1.0 + scale)) + shift


def _const_spec(shape):
    return pl.BlockSpec(shape, lambda *_: (0,) * len(shape), pipeline_mode=pl.Buffered(1))


def _ada_kernel(c_ref, w_ref, b_ref, o_ref):
    c = c_ref[...]
    cond = (c * _sigmoid(c)).astype(BF16)
    o_ref[...] = _dot(cond, w_ref[...].astype(BF16)) + b_ref[...]


def _ada(c, w, b):
    bsz, d = c.shape
    n = w.shape[1]
    return pl.pallas_call(
        _ada_kernel,
        grid=(n // ADA_TN,),
        in_specs=[
            pl.BlockSpec((bsz, d), lambda j: (0, 0)),
            pl.BlockSpec((d, ADA_TN), lambda j: (0, j)),
            pl.BlockSpec((1, ADA_TN), lambda j: (0, j)),
        ],
        out_specs=pl.BlockSpec((bsz, ADA_TN), lambda j: (0, j)),
        out_shape=jax.ShapeDtypeStruct((bsz, n), F32),
        compiler_params=pltpu.CompilerParams(
            dimension_semantics=("arbitrary",), vmem_limit_bytes=VMEM_LIMIT_BYTES),
        name="ada",
    )(c, w, b.reshape(1, n))


def _ffn_kernel(*refs, pre_mix, final_norm):
    it = iter(refs)
    x_ref = next(it)
    if pre_mix:
        osb_ref, ofx_ref, wout_ref, gate_mix_ref = next(it), next(it), next(it), next(it)
    shift_ref, scale_ref, gate_ref, g_ref = next(it), next(it), next(it), next(it)
    wup_ref, wdown_ref = next(it), next(it)
    if final_norm:
        gf_ref = next(it)
    o_ref, h_scr, acc_scr = next(it), next(it), next(it)

    x = x_ref[...]
    if pre_mix:
        mix = (_dot(osb_ref[...], wout_ref[0:SB_WIDTH, :])
               + _dot(ofx_ref[...], wout_ref[SB_WIDTH:SB_WIDTH + FOX_WIDTH, :]))
        x = x + gate_mix_ref[...] * mix
    o_ref[...] = x
    h_scr[...] = _modulate(x, g_ref[...], shift_ref[...], scale_ref[...]).astype(BF16)
    acc_scr[...] = jnp.zeros_like(acc_scr)

    hidden = wdown_ref.shape[0]
    for lo in range(0, hidden, FFN_HC):
        h = h_scr[...]
        gt = _dot(h, wup_ref[:, lo:lo + FFN_HC])
        up = _dot(h, wup_ref[:, hidden + lo:hidden + lo + FFN_HC])
        a = (gt * _sigmoid(gt) * up).astype(BF16)
        acc_scr[...] += _dot(a, wdown_ref[lo:lo + FFN_HC, :])
    y = o_ref[...] + (0.5 * gate_ref[...]) * acc_scr[...]
    if final_norm:
        ms = jnp.mean(y * y, axis=-1, keepdims=True)
        y = (y * lax.rsqrt(ms + EPS)) * gf_ref[...]
    o_ref[...] = y


def _ffn(x, shift, scale, gate, g, w_up, w_down, mix=None, g_final=None):
    bsz, s, d = x.shape
    assert w_down.shape[0] % FFN_HC == 0
    tm = FFN_TM
    tok = pl.BlockSpec((None, tm, d), lambda b, i: (b, i, 0))
    per_batch = pl.BlockSpec((None, 1, d), lambda b, i: (b, 0, 0))
    args, specs = [x], [tok]
    if mix is not None:
        o_sb, o_fx, w_out, gate_mix = mix
        args += [o_sb, o_fx, w_out, gate_mix]
        specs += [pl.BlockSpec((None, tm, SB_WIDTH), lambda b, i: (b, i, 0)),
                  pl.BlockSpec((None, tm, FOX_WIDTH), lambda b, i: (b, i, 0)),
                  _const_spec(w_out.shape), per_batch]
    args += [shift, scale, gate, g.reshape(1, d), w_up, w_down]
    specs += [per_batch, per_batch, per_batch, _const_spec((1, d)),
              _const_spec(w_up.shape), _const_spec(w_down.shape)]
    if g_final is not None:
        args.append(g_final.reshape(1, d))
        specs.append(_const_spec((1, d)))
    return pl.pallas_call(
        functools.partial(_ffn_kernel, pre_mix=mix is not None, final_norm=g_final is not None),
        grid=(bsz, s // tm),
        in_specs=specs,
        out_specs=tok,
        out_shape=jax.ShapeDtypeStruct((bsz, s, d), F32),
        scratch_shapes=[pltpu.VMEM((tm, d), BF16), pltpu.VMEM((tm, d), F32)],
        compiler_params=pltpu.CompilerParams(
            dimension_semantics=("arbitrary", "arbitrary"), vmem_limit_bytes=VMEM_LIMIT_BYTES),
        name="ffn_mix" if mix is not None else "ffn",
    )(*args)


def _lane_cumsum(y):
    n = y.shape[-1]
    lane = lax.broadcasted_iota(jnp.int32, y.shape, y.ndim - 1)
    shift = 1
    while shift < n:
        y = y + jnp.where(lane >= shift, pltpu.roll(y, shift, axis=y.ndim - 1), 0.0)
        shift *= 2
    return y


def _proj_kernel(x_ref, shift_ref, scale_ref, g_ref, w_ref, wf_ref, bf_ref, gq_ref, gk_ref,
                 grp_ref, sbq_ref, sbk_ref, sbv_ref, fxq_ref, fxk_ref, fxv_ref, fxg_ref,
                 fcum_ref, h_scr, carry_scr):
    @pl.when(pl.program_id(1) == 0)
    def _():
        carry_scr[...] = jnp.zeros_like(carry_scr)

    x = x_ref[...]
    h_scr[...] = _modulate(x, g_ref[...], shift_ref[...], scale_ref[...]).astype(BF16)
    h = h_scr[...]
    w = SB_WIDTH

    def proj(idx):
        return _dot(h, w_ref[:, idx * w:(idx + 1) * w])

    def head_rms(p):
        ms = _dot((p * p).astype(BF16), grp_ref[...])
        return p * lax.rsqrt(ms + EPS)

    f = _dot(h, wf_ref[...])
    ft = jnp.transpose(f)[0:FOX_HEADS, :] + bf_ref[...]
    log_f = -_softplus(-ft)
    fcum = _lane_cumsum(log_f) + carry_scr[:, 0:1]
    fcum_ref[...] = fcum
    carry_scr[...] = jnp.broadcast_to(fcum[:, fcum.shape[1] - 1:], carry_scr.shape)

    fxq_ref[...] = (head_rms(proj(3)) * (gq_ref[...] * (QK_SCALE * LOG2E))).astype(BF16)
    fxk_ref[...] = (head_rms(proj(4)) * gk_ref[...]).astype(BF16)
    fxg_ref[...] = _sigmoid(proj(6)).astype(BF16)
    sbq_ref[...] = (proj(0) * (QK_SCALE * LOG2E)).astype(BF16)
    sbk_ref[...] = proj(1).astype(BF16)
    sbv_ref[...] = proj(2).astype(BF16)
    fxv_ref[...] = proj(5).astype(BF16)


def _in_proj(x, shift, scale, g, w_main, w_f, b_f, gq, gk, grp):
    bsz, s, d = x.shape
    tm = PROJ_TM
    tok = pl.BlockSpec((None, tm, d), lambda b, i: (b, i, 0))
    per_batch = pl.BlockSpec((None, 1, d), lambda b, i: (b, 0, 0))
    head_out = pl.BlockSpec((None, tm, SB_WIDTH), lambda b, i: (b, i, 0))
    head_shape = jax.ShapeDtypeStruct((bsz, s, SB_WIDTH), BF16)
    return pl.pallas_call(
        _proj_kernel,
        grid=(bsz, s // tm),
        in_specs=[tok, per_batch, per_batch, _const_spec((1, d)), _const_spec(w_main.shape),
                  _const_spec(w_f.shape), _const_spec(b_f.shape), _const_spec(gq.shape),
                  _const_spec(gk.shape), _const_spec(grp.shape)],
        out_specs=[head_out] * 7 + [pl.BlockSpec((None, FOX_HEADS, tm), lambda b, i: (b, 0, i))],
        out_shape=[head_shape] * 7 + [jax.ShapeDtypeStruct((bsz, FOX_HEADS, s), F32)],
        scratch_shapes=[pltpu.VMEM((tm, d), BF16), pltpu.VMEM((FOX_HEADS, LANES), F32)],
        compiler_params=pltpu.CompilerParams(
            dimension_semantics=("arbitrary", "arbitrary"), vmem_limit_bytes=VMEM_LIMIT_BYTES),
        name="in_proj",
    )(x, shift, scale, g.reshape(1, d), w_main, w_f, b_f, gq, gk, grp)


def _split_heads(x):
    lane = lax.broadcasted_iota(jnp.int32, x.shape, 1)
    zero = jnp.zeros_like(x)
    return jnp.where(lane < HEAD_DIM, x, zero), jnp.where(lane >= HEAD_DIM, x, zero)


def _pair_lanes(pp):
    return slice(pp * LANES, (pp + 1) * LANES)


def _fill_vexp(v_ref, lanes, vexp_ref, n_blocks, t, with_ones):
    def body(j, carry):
        v0, v1 = _split_heads(v_ref[pl.ds(pl.multiple_of(j * t, t), t), lanes])
        vexp_ref[j, 0:t, 0:LANES] = v0
        vexp_ref[j, t:2 * t, 0:LANES] = v1
        if with_ones:
            lane = lax.broadcasted_iota(jnp.int32, (t, LANES), 1)
            vexp_ref[j, 0:t, LANES:2 * LANES] = (lane < HEAD_DIM).astype(F32).astype(vexp_ref.dtype)
            vexp_ref[j, t:2 * t, LANES:2 * LANES] = (lane >= HEAD_DIM).astype(F32).astype(
                vexp_ref.dtype)
        return carry

    lax.fori_loop(0, n_blocks, body, 0)


def _per_head(lane, a0, a1):
    return jnp.where(lane < HEAD_DIM, a0, a1)


def _lane_tile(a, width):
    return jnp.concatenate([a] * (width // LANES), axis=1)


def _head_rmsnorm(o, lane):
    sq = o * o
    ss0 = jnp.sum(jnp.where(lane < HEAD_DIM, sq, 0.0), axis=1, keepdims=True)
    ss1 = jnp.sum(jnp.where(lane >= HEAD_DIM, sq, 0.0), axis=1, keepdims=True)
    inv = _per_head(lane, lax.rsqrt(ss0 * (1.0 / HEAD_DIM) + EPS),
                    lax.rsqrt(ss1 * (1.0 / HEAD_DIM) + EPS))
    return o * inv


def _task_table(tasks, n_q):
    idle = [(0, 0, n_q, 0)] * PIPE_LAG
    return np.asarray(idle + list(tasks) + idle, dtype=np.int32).T.copy()


def _fill_qexp(q_ref, lanes, qexp_ref, n_blocks, t):
    def body(j, carry):
        rows = pl.ds(pl.multiple_of(j * t, t), t)
        q0, q1 = _split_heads(q_ref[rows, lanes])
        qexp_ref[0, rows, :] = q0
        qexp_ref[1, rows, :] = q1
        return carry

    lax.fori_loop(0, n_blocks, body, 0)


def _score_stage(q_blk, k_blk, qexp_ref, k_ref, lanes, s_ref, t):
    rows = pl.ds(pl.multiple_of(q_blk * t, t), t)
    k = k_ref[pl.ds(pl.multiple_of(k_blk * t, t), t), lanes]
    for hd in range(HEADS_PER_TILE):
        s_ref[hd] = _dot_nt(qexp_ref[hd, rows, :], k)


def _causal_bias(t, strict):
    row = lax.broadcasted_iota(jnp.int32, (t, t), 0)
    col = lax.broadcasted_iota(jnp.int32, (t, t), 1)
    keep = col < row if strict else col <= row
    return jnp.where(keep, 0.0, MASKED).astype(F32)


def _sb_kernel(tab_ref, q_ref, k_ref, v_ref, g_ref, tri_ref, o_ref, vexp_scr, qexp_scr, bias_scr,
               acc_scr, carry_scr, z_scr, zc_scr, suf_scr):
    t = SB_T
    n_q = q_ref.shape[0] // t
    n_iter = tab_ref.shape[1] - PIPE_LAG
    pairs = range(ATT_PAIRS)

    for pp in pairs:
        _fill_vexp(v_ref, _pair_lanes(pp), vexp_scr.at[pp], n_q, t, with_ones=False)
        _fill_qexp(q_ref, _pair_lanes(pp), qexp_scr.at[pp], n_q, t)
    bias_scr[0] = jnp.zeros((t, t), F32)
    bias_scr[1] = _causal_bias(t, strict=True)
    acc_scr[...] = jnp.zeros_like(acc_scr)
    carry_scr[...] = jnp.zeros_like(carry_scr)
    z_scr[...] = jnp.zeros_like(z_scr)
    zc_scr[...] = jnp.zeros_like(zc_scr)
    suf_scr[...] = jnp.zeros_like(suf_scr)

    def keep_stage(pp, slot, mask_idx):
        bias = bias_scr[mask_idx]
        for hd in range(HEADS_PER_TILE):
            z = z_scr[pp, hd] + bias
            sp = jnp.maximum(z, 0.0) + jnp.log2(1.0 + jnp.exp2(-jnp.abs(z)))
            hi = sp.astype(BF16)
            lo = (sp - hi.astype(F32)).astype(BF16)
            suf = _dot(jnp.concatenate([hi, lo], axis=1), tri_ref[...])
            suf_scr[pp, hd] = suf
            carry = carry_scr[pp, slot, hd]
            zc_scr[pp, hd] = z - _lane_tile(carry, t)
            carry_scr[pp, slot, hd] = carry + suf[:, 0:1]

    def value_stage(pp, k_blk, slot):
        ws = [jnp.exp2(zc_scr[pp, hd] - suf_scr[pp, hd]).astype(BF16)
              for hd in range(HEADS_PER_TILE)]
        acc_scr[pp, slot] += _dot(jnp.concatenate(ws, axis=1), vexp_scr[pp, k_blk])

    def score_stage(pp, q_blk, k_blk):
        _score_stage(q_blk, k_blk, qexp_scr.at[pp], k_ref, _pair_lanes(pp), z_scr.at[pp], t)

    def body(i, carry):
        for pp in pairs:
            value_stage(pp, tab_ref[1, i], tab_ref[2, i])
        for pp in pairs:
            keep_stage(pp, tab_ref[2, i + 1], tab_ref[3, i + 1])
        for pp in pairs:
            score_stage(pp, tab_ref[0, i + 2], tab_ref[1, i + 2])
        return carry

    lax.fori_loop(0, n_iter, body, 0)

    def tail(pp, q_blk):
        def live():
            return (jnp.min(carry_scr[pp, q_blk]) <= -F32_EXP2_ZERO_BELOW).astype(jnp.int32)

        def cond(state):
            k_blk, go = state
            return jnp.logical_and(k_blk >= 0, go > 0)

        def step(state):
            k_blk, _ = state
            score_stage(pp, q_blk, k_blk)
            keep_stage(pp, q_blk, 0)
            value_stage(pp, k_blk, q_blk)
            return k_blk - 1, live()

        lax.while_loop(cond, step, (q_blk - 2, live()))

    lane = lax.broadcasted_iota(jnp.int32, (t, LANES), 1)
    for pp in pairs:
        def tail_body(q_blk, carry, pp=pp):
            tail(pp, q_blk)
            return carry

        lax.fori_loop(2, n_q, tail_body, 0)
        lanes = _pair_lanes(pp)
        for q_blk in range(n_q):
            o = _head_rmsnorm(acc_scr[pp, q_blk], lane) * g_ref[:, lanes]
            o_ref[q_blk * t:(q_blk + 1) * t, lanes] = o.astype(o_ref.dtype)


def _sb_attention(q, k, v, g, tri):
    bsz, s, width = q.shape
    t = SB_T
    n_q = s // t
    n_tiles = width // LANES
    tasks = []
    for qb in range(n_q):
        tasks.append((qb, qb, qb, 1))
        if qb >= 1:
            tasks.append((qb, qb - 1, qb, 0))
    table = _task_table(tasks, n_q)
    pp, hh = ATT_PAIRS, HEADS_PER_TILE
    seq_spec = pl.BlockSpec((None, s, pp * LANES), lambda b, hp: (b, 0, hp))
    return pl.pallas_call(
        _sb_kernel,
        grid=(bsz, n_tiles // pp),
        in_specs=[pl.BlockSpec(memory_space=pltpu.SMEM), seq_spec, seq_spec, seq_spec,
                  pl.BlockSpec((1, pp * LANES), lambda b, hp: (0, hp)), _const_spec(tri.shape)],
        out_specs=seq_spec,
        out_shape=jax.ShapeDtypeStruct((bsz, s, width), BF16),
        scratch_shapes=[pltpu.VMEM((pp, n_q, 2 * t, LANES), BF16),
                        pltpu.VMEM((pp, hh, s, LANES), BF16),
                        pltpu.VMEM((2, t, t), F32),
                        pltpu.VMEM((pp, n_q + 1, t, LANES), F32),
                        pltpu.VMEM((pp, n_q + 1, hh, t, LANES), F32),
                        pltpu.VMEM((pp, hh, t, t), F32),
                        pltpu.VMEM((pp, hh, t, t), F32),
                        pltpu.VMEM((pp, hh, t, t), F32)],
        compiler_params=pltpu.CompilerParams(
            dimension_semantics=("arbitrary", "arbitrary"), vmem_limit_bytes=VMEM_LIMIT_BYTES),
        name="sb_attn",
    )(jnp.asarray(table), q, k, v, g.reshape(1, width), tri)


def _fox_kernel(tab_ref, q_ref, k_ref, v_ref, f_ref, gate_ref, g_ref, o_ref, vexp_scr, qexp_scr,
                bias_scr, acc_scr, m_scr, s_scr, p_scr, alpha_scr):
    t = FOX_T
    n_q = q_ref.shape[0] // t
    n_iter = tab_ref.shape[1] - PIPE_LAG
    pairs = range(ATT_PAIRS)
    lane = lax.broadcasted_iota(jnp.int32, (t, LANES), 1)

    for pp in pairs:
        _fill_vexp(v_ref, _pair_lanes(pp), vexp_scr.at[pp], n_q, t, with_ones=True)
        _fill_qexp(q_ref, _pair_lanes(pp), qexp_scr.at[pp], n_q, t)
    bias_scr[0] = jnp.zeros((t, t), F32)
    bias_scr[1] = _causal_bias(t, strict=False)
    acc_scr[...] = jnp.zeros_like(acc_scr)
    m_scr[...] = jnp.full_like(m_scr, -jnp.inf)
    s_scr[...] = jnp.zeros_like(s_scr)
    p_scr[...] = jnp.zeros_like(p_scr)
    alpha_scr[...] = jnp.zeros_like(alpha_scr)

    def softmax_stage(pp, k_blk, slot, mask_idx):
        f_keys = f_ref[pp, :, pl.ds(pl.multiple_of(k_blk * t, t), t)] * LOG2E
        bias = bias_scr[mask_idx]
        alphas = []
        for hd in range(HEADS_PER_TILE):
            sc = s_scr[pp, hd] - f_keys[hd:hd + 1, :] + bias
            m_old = m_scr[pp, slot, hd]
            m_new = jnp.maximum(m_old, jnp.max(sc, axis=1, keepdims=True))
            alphas.append(jnp.exp2(m_old - m_new))
            m_scr[pp, slot, hd] = m_new
            p_scr[pp, :, hd * t:(hd + 1) * t] = jnp.exp2(sc - _lane_tile(m_new, t)).astype(BF16)
        alpha_scr[pp] = _per_head(lane, alphas[0], alphas[1])

    def value_stage(pp, k_blk, slot):
        pv = _dot(p_scr[pp], vexp_scr[pp, k_blk])
        acc_scr[pp, slot] = acc_scr[pp, slot] * _lane_tile(alpha_scr[pp], 2 * LANES) + pv

    def body(i, carry):
        for pp in pairs:
            value_stage(pp, tab_ref[1, i], tab_ref[2, i])
        for pp in pairs:
            softmax_stage(pp, tab_ref[1, i + 1], tab_ref[2, i + 1], tab_ref[3, i + 1])
        for pp in pairs:
            _score_stage(tab_ref[0, i + 2], tab_ref[1, i + 2], qexp_scr.at[pp], k_ref,
                         _pair_lanes(pp), s_scr.at[pp], t)
        return carry

    lax.fori_loop(0, n_iter, body, 0)

    for pp in pairs:
        lanes = _pair_lanes(pp)
        for q_blk in range(n_q):
            rows = slice(q_blk * t, (q_blk + 1) * t)
            acc = acc_scr[pp, q_blk]
            o = acc[:, 0:LANES] / acc[:, LANES:2 * LANES]
            o = _head_rmsnorm(o, lane) * g_ref[:, lanes] * gate_ref[rows, lanes].astype(F32)
            o_ref[rows, lanes] = o.astype(o_ref.dtype)


def _fox_attention(q, k, v, fcum, gate, g):
    bsz, s, width = q.shape
    t = FOX_T
    n_q = s // t
    n_tiles = width // LANES
    tasks = [(qb, kb, qb, int(kb == qb)) for qb in range(n_q) for kb in range(qb + 1)]
    table = _task_table(tasks, n_q)
    pp, hh = ATT_PAIRS, HEADS_PER_TILE
    seq_spec = pl.BlockSpec((None, s, pp * LANES), lambda b, hp: (b, 0, hp))
    f_tiles = fcum.reshape(bsz, n_tiles, hh, s)
    return pl.pallas_call(
        _fox_kernel,
        grid=(bsz, n_tiles // pp),
        in_specs=[pl.BlockSpec(memory_space=pltpu.SMEM), seq_spec, seq_spec, seq_spec,
                  pl.BlockSpec((None, pp, hh, s), lambda b, hp: (b, hp, 0, 0)),
                  seq_spec, pl.BlockSpec((1, pp * LANES), lambda b, hp: (0, hp))],
        out_specs=seq_spec,
        out_shape=jax.ShapeDtypeStruct((bsz, s, width), BF16),
        scratch_shapes=[pltpu.VMEM((pp, n_q, 2 * t, 2 * LANES), BF16),
                        pltpu.VMEM((pp, hh, s, LANES), BF16),
                        pltpu.VMEM((2, t, t), F32),
                        pltpu.VMEM((pp, n_q + 1, t, 2 * LANES), F32),
                        pltpu.VMEM((pp, n_q + 1, hh, t, LANES), F32),
                        pltpu.VMEM((pp, hh, t, t), F32),
                        pltpu.VMEM((pp, t, hh * t), BF16),
                        pltpu.VMEM((pp, t, LANES), F32)],
        compiler_params=pltpu.CompilerParams(
            dimension_semantics=("arbitrary", "arbitrary"), vmem_limit_bytes=VMEM_LIMIT_BYTES),
        name="fox_attn",
    )(jnp.asarray(table), q, k, v, f_tiles, gate, g.reshape(1, width))


def kernel(x, c, w_ada, b_ada, g_ffn1, w_ffn1_up, w_ffn1_down, g_mix, w_in, b_forget, g_fox_q,
           g_fox_k, g_sb_out, g_fox_out, w_out, g_ffn2, w_ffn2_up, w_ffn2_down, g_final):
    bsz, s, d = x.shape
    depth = w_ada.shape[0]
    t = SB_T

    head_of = jnp.arange(FOX_WIDTH) // HEAD_DIM
    grp = jnp.where(head_of[:, None] == head_of[None, :], 1.0 / HEAD_DIM, 0.0).astype(BF16)
    key = jnp.arange(t)
    tri1 = (key[:, None] >= key[None, :]).astype(BF16)
    tri = jnp.concatenate([tri1, tri1], axis=0)

    n_main = 3 * SB_WIDTH + 4 * FOX_WIDTH
    for l in range(depth):
        ada = _ada(c, w_ada[l], b_ada[l]).reshape(bsz, N_SUBLAYERS, 3, 1, d)
        shift, scale, gate = ada[:, :, 0], ada[:, :, 1], ada[:, :, 2]

        x = _ffn(x, shift[:, 0], scale[:, 0], gate[:, 0], g_ffn1[l],
                 w_ffn1_up[l].astype(BF16), w_ffn1_down[l].astype(BF16))

        w_main = w_in[l][:, :n_main].astype(BF16)
        w_f = jnp.pad(w_in[l][:, n_main:], ((0, 0), (0, LANES - FOX_HEADS))).astype(BF16)
        sbq, sbk, sbv, fxq, fxk, fxv, fxg, fcum = _in_proj(
            x, shift[:, 1], scale[:, 1], g_mix[l], w_main, w_f,
            b_forget[l].reshape(FOX_HEADS, 1),
            jnp.tile(g_fox_q[l], FOX_HEADS).reshape(1, FOX_WIDTH),
            jnp.tile(g_fox_k[l], FOX_HEADS).reshape(1, FOX_WIDTH), grp)

        o_sb = _sb_attention(sbq, sbk, sbv, g_sb_out[l], tri)
        o_fx = _fox_attention(fxq, fxk, fxv, fcum, fxg, g_fox_out[l])

        x = _ffn(x, shift[:, 2], scale[:, 2], gate[:, 2], g_ffn2[l],
                 w_ffn2_up[l].astype(BF16), w_ffn2_down[l].astype(BF16),
                 mix=(o_sb, o_fx, w_out[l].astype(BF16), gate[:, 1]),
                 g_final=g_final if l == depth - 1 else None)
    return x
```

```python
import functools

import numpy as np

import jax
import jax.numpy as jnp
from jax import lax
from jax.experimental import pallas as pl
from jax.experimental.pallas import tpu as pltpu

D_MODEL = 1024
HEAD_DIM = 64
SB_HEADS = 8
FOX_HEADS = 8
SB_WIDTH = SB_HEADS * HEAD_DIM
FOX_WIDTH = FOX_HEADS * HEAD_DIM
FFN_HIDDEN = 2816
N_SUBLAYERS = 3
EPS = 1e-6
QK_SCALE = HEAD_DIM ** -0.5

LANES = 128
HEADS_PER_TILE = LANES // HEAD_DIM
VMEM_LIMIT_BYTES = 56 * 1024 * 1024

ADA_TN = 1152
FFN_TM = 512
FFN_HC = 256
PROJ_TM = 512
SB_T = 256
FOX_T = 512
ATT_PAIRS = 2
PIPE_LAG = 2
ROW_CHUNK = 32

LOG2E = 1.4426950408889634
F32_EXP2_ZERO_BELOW = -150.0
MASKED = -1e30

BF16 = jnp.bfloat16
F32 = jnp.float32


def _dot(a, b):
    return jnp.dot(a, b, preferred_element_type=F32)


def _dot_nt(a, b):
    return lax.dot_general(a, b, (((1,), (1,)), ((), ())), preferred_element_type=F32)


def _sigmoid(x):
    return 1.0 / (1.0 + jnp.exp(-x))


def _softplus(x):
    return jnp.maximum(x, 0.0) + jnp.log(1.0 + jnp.exp(-jnp.abs(x)))


def _modulate(x, g, shift, scale):
    ms = jnp.mean(x * x, axis=-1, keepdims=True)
    return (x * lax.rsqrt(ms + EPS)) * (g * (1.0 + scale)) + shift


def _const_spec(shape):
    return pl.BlockSpec(shape, lambda *_: (0,) * len(shape), pipeline_mode=pl.Buffered(1))


def _ada_kernel(c_ref, w_ref, b_ref, o_ref):
    c = c_ref[...]
    cond = (c * _sigmoid(c)).astype(BF16)
    o_ref[...] = _dot(cond, w_ref[...].astype(BF16)) + b_ref[...]


def _ada(c, w, b):
    bsz, d = c.shape
    n = w.shape[1]
    return pl.pallas_call(
        _ada_kernel,
        grid=(n // ADA_TN,),
        in_specs=[
            pl.BlockSpec((bsz, d), lambda j: (0, 0)),
            pl.BlockSpec((d, ADA_TN), lambda j: (0, j)),
            pl.BlockSpec((1, ADA_TN), lambda j: (0, j)),
        ],
        out_specs=pl.BlockSpec((bsz, ADA_TN), lambda j: (0, j)),
        out_shape=jax.ShapeDtypeStruct((bsz, n), F32),
        compiler_params=pltpu.CompilerParams(
            dimension_semantics=("arbitrary",), vmem_limit_bytes=VMEM_LIMIT_BYTES),
        name="ada",
    )(c, w, b.reshape(1, n))


def _ffn_kernel(*refs, pre_mix, final_norm):
    it = iter(refs)
    x_ref = next(it)
    if pre_mix:
        osb_ref, ofx_ref, wout_ref, gate_mix_ref = next(it), next(it), next(it), next(it)
    shift_ref, scale_ref, gate_ref, g_ref = next(it), next(it), next(it), next(it)
    wup_ref, wdown_ref = next(it), next(it)
    if final_norm:
        gf_ref = next(it)
    o_ref, h_scr, acc_scr = next(it), next(it), next(it)

    x = x_ref[...]
    if pre_mix:
        mix = (_dot(osb_ref[...], wout_ref[0:SB_WIDTH, :].astype(BF16))
               + _dot(ofx_ref[...], wout_ref[SB_WIDTH:SB_WIDTH + FOX_WIDTH, :].astype(BF16)))
        x = x + gate_mix_ref[...] * mix
    o_ref[...] = x
    h_scr[...] = _modulate(x, g_ref[...], shift_ref[...], scale_ref[...]).astype(BF16)
    acc_scr[...] = jnp.zeros_like(acc_scr)

    hidden = wdown_ref.shape[0]
    for lo in range(0, hidden, FFN_HC):
        h = h_scr[...]
        gt = _dot(h, wup_ref[:, lo:lo + FFN_HC].astype(BF16))
        up = _dot(h, wup_ref[:, hidden + lo:hidden + lo + FFN_HC].astype(BF16))
        a = (gt * _sigmoid(gt) * up).astype(BF16)
        acc_scr[...] += _dot(a, wdown_ref[lo:lo + FFN_HC, :].astype(BF16))
    y = o_ref[...] + (0.5 * gate_ref[...]) * acc_scr[...]
    if final_norm:
        ms = jnp.mean(y * y, axis=-1, keepdims=True)
        y = (y * lax.rsqrt(ms + EPS)) * gf_ref[...]
    o_ref[...] = y


def _ffn(x, shift, scale, gate, g, w_up, w_down, mix=None, g_final=None):
    bsz, s, d = x.shape
    assert w_down.shape[0] % FFN_HC == 0
    tm = FFN_TM
    tok = pl.BlockSpec((None, tm, d), lambda b, i: (b, i, 0))
    per_batch = pl.BlockSpec((None, 1, d), lambda b, i: (b, 0, 0))
    args, specs = [x], [tok]
    if mix is not None:
        o_sb, o_fx, w_out, gate_mix = mix
        args += [o_sb, o_fx, w_out, gate_mix]
        specs += [pl.BlockSpec((None, tm, SB_WIDTH), lambda b, i: (b, i, 0)),
                  pl.BlockSpec((None, tm, FOX_WIDTH), lambda b, i: (b, i, 0)),
                  _const_spec(w_out.shape), per_batch]
    args += [shift, scale, gate, g.reshape(1, d), w_up, w_down]
    specs += [per_batch, per_batch, per_batch, _const_spec((1, d)),
              _const_spec(w_up.shape), _const_spec(w_down.shape)]
    if g_final is not None:
        args.append(g_final.reshape(1, d))
        specs.append(_const_spec((1, d)))
    return pl.pallas_call(
        functools.partial(_ffn_kernel, pre_mix=mix is not None, final_norm=g_final is not None),
        grid=(bsz, s // tm),
        in_specs=specs,
        out_specs=tok,
        out_shape=jax.ShapeDtypeStruct((bsz, s, d), F32),
        scratch_shapes=[pltpu.VMEM((tm, d), BF16), pltpu.VMEM((tm, d), F32)],
        compiler_params=pltpu.CompilerParams(
            dimension_semantics=("arbitrary", "arbitrary"), vmem_limit_bytes=VMEM_LIMIT_BYTES),
        name="ffn_mix" if mix is not None else "ffn",
    )(*args)


def _lane_cumsum(y):
    n = y.shape[-1]
    lane = lax.broadcasted_iota(jnp.int32, y.shape, y.ndim - 1)
    shift = 1
    while shift < n:
        y = y + jnp.where(lane >= shift, pltpu.roll(y, shift, axis=y.ndim - 1), 0.0)
        shift *= 2
    return y


def _proj_kernel(x_ref, shift_ref, scale_ref, g_ref, w_ref, wf_ref, bf_ref, gq_ref, gk_ref,
                 grp_ref, sbq_ref, sbk_ref, sbv_ref, fxq_ref, fxk_ref, fxv_ref, fxg_ref,
                 fcum_ref, h_scr, carry_scr):
    @pl.when(pl.program_id(1) == 0)
    def _():
        carry_scr[...] = jnp.zeros_like(carry_scr)

    x = x_ref[...]
    h_scr[...] = _modulate(x, g_ref[...], shift_ref[...], scale_ref[...]).astype(BF16)
    h = h_scr[...]
    w = SB_WIDTH

    def proj(idx):
        return _dot(h, w_ref[:, idx * w:(idx + 1) * w].astype(BF16))

    def head_rms(p):
        ms = _dot((p * p).astype(BF16), grp_ref[...])
        return p * lax.rsqrt(ms + EPS)

    f = _dot(h, wf_ref[...])
    ft = jnp.transpose(f)[0:FOX_HEADS, :] + bf_ref[...]
    log_f = -_softplus(-ft)
    fcum = _lane_cumsum(log_f) + carry_scr[:, 0:1]
    fcum_ref[...] = fcum
    carry_scr[...] = jnp.broadcast_to(fcum[:, fcum.shape[1] - 1:], carry_scr.shape)

    fxq_ref[...] = (head_rms(proj(3)) * (gq_ref[...] * (QK_SCALE * LOG2E))).astype(BF16)
    fxk_ref[...] = (head_rms(proj(4)) * gk_ref[...]).astype(BF16)
    fxg_ref[...] = _sigmoid(proj(6)).astype(BF16)
    sbq_ref[...] = (proj(0) * (QK_SCALE * LOG2E)).astype(BF16)
    sbk_ref[...] = proj(1).astype(BF16)
    sbv_ref[...] = proj(2).astype(BF16)
    fxv_ref[...] = proj(5).astype(BF16)


def _in_proj(x, shift, scale, g, w_main, w_f, b_f, gq, gk, grp):
    bsz, s, d = x.shape
    tm = PROJ_TM
    tok = pl.BlockSpec((None, tm, d), lambda b, i: (b, i, 0))
    per_batch = pl.BlockSpec((None, 1, d), lambda b, i: (b, 0, 0))
    head_out = pl.BlockSpec((None, tm, SB_WIDTH), lambda b, i: (b, i, 0))
    head_shape = jax.ShapeDtypeStruct((bsz, s, SB_WIDTH), BF16)
    return pl.pallas_call(
        _proj_kernel,
        grid=(bsz, s // tm),
        in_specs=[tok, per_batch, per_batch, _const_spec((1, d)), _const_spec(w_main.shape),
                  _const_spec(w_f.shape), _const_spec(b_f.shape), _const_spec(gq.shape),
                  _const_spec(gk.shape), _const_spec(grp.shape)],
        out_specs=[head_out] * 7 + [pl.BlockSpec((None, FOX_HEADS, tm), lambda b, i: (b, 0, i))],
        out_shape=[head_shape] * 7 + [jax.ShapeDtypeStruct((bsz, FOX_HEADS, s), F32)],
        scratch_shapes=[pltpu.VMEM((tm, d), BF16), pltpu.VMEM((FOX_HEADS, LANES), F32)],
        compiler_params=pltpu.CompilerParams(
            dimension_semantics=("arbitrary", "arbitrary"), vmem_limit_bytes=VMEM_LIMIT_BYTES),
        name="in_proj",
    )(x, shift, scale, g.reshape(1, d), w_main, w_f, b_f, gq, gk, grp)


def _split_heads(x):
    lane = lax.broadcasted_iota(jnp.int32, x.shape, 1)
    zero = jnp.zeros_like(x)
    return jnp.where(lane < HEAD_DIM, x, zero), jnp.where(lane >= HEAD_DIM, x, zero)


def _pair_lanes(pp):
    return slice(pp * LANES, (pp + 1) * LANES)


def _fill_vexp(v_ref, lanes, vexp_ref, n_blocks, t, with_ones):
    def body(j, carry):
        v0, v1 = _split_heads(v_ref[pl.ds(pl.multiple_of(j * t, t), t), lanes])
        vexp_ref[j, 0:t, 0:LANES] = v0
        vexp_ref[j, t:2 * t, 0:LANES] = v1
        if with_ones:
            lane = lax.broadcasted_iota(jnp.int32, (t, LANES), 1)
            vexp_ref[j, 0:t, LANES:2 * LANES] = (lane < HEAD_DIM).astype(F32).astype(vexp_ref.dtype)
            vexp_ref[j, t:2 * t, LANES:2 * LANES] = (lane >= HEAD_DIM).astype(F32).astype(
                vexp_ref.dtype)
        return carry

    lax.fori_loop(0, n_blocks, body, 0)


def _per_head(lane, a0, a1):
    return jnp.where(lane < HEAD_DIM, a0, a1)


def _lane_tile(a, width):
    return jnp.concatenate([a] * (width // LANES), axis=1)


def _head_rmsnorm(o, lane):
    sq = o * o
    ss0 = jnp.sum(jnp.where(lane < HEAD_DIM, sq, 0.0), axis=1, keepdims=True)
    ss1 = jnp.sum(jnp.where(lane >= HEAD_DIM, sq, 0.0), axis=1, keepdims=True)
    inv = _per_head(lane, lax.rsqrt(ss0 * (1.0 / HEAD_DIM) + EPS),
                    lax.rsqrt(ss1 * (1.0 / HEAD_DIM) + EPS))
    return o * inv


def _task_table(tasks, n_q):
    idle = [(0, 0, n_q, 0)] * PIPE_LAG
    return np.asarray(idle + list(tasks) + idle, dtype=np.int32).T.copy()


def _fill_qexp(q_ref, lanes, qexp_ref, n_blocks, t):
    def body(j, carry):
        rows = pl.ds(pl.multiple_of(j * t, t), t)
        q0, q1 = _split_heads(q_ref[rows, lanes])
        qexp_ref[0, rows, :] = q0
        qexp_ref[1, rows, :] = q1
        return carry

    lax.fori_loop(0, n_blocks, body, 0)


def _score_stage(q_blk, k_blk, qexp_ref, k_ref, lanes, s_ref, t):
    rows = pl.ds(pl.multiple_of(q_blk * t, t), t)
    k = k_ref[pl.ds(pl.multiple_of(k_blk * t, t), t), lanes]
    for hd in range(HEADS_PER_TILE):
        s_ref[hd] = _dot_nt(qexp_ref[hd, rows, :], k)


def _causal_bias(t, strict):
    row = lax.broadcasted_iota(jnp.int32, (t, t), 0)
    col = lax.broadcasted_iota(jnp.int32, (t, t), 1)
    keep = col < row if strict else col <= row
    return jnp.where(keep, 0.0, MASKED).astype(F32)


def _sb_kernel(tab_ref, q_ref, k_ref, v_ref, g_ref, tri_ref, o_ref, vexp_scr, qexp_scr, bias_scr,
               acc_scr, carry_scr, z_scr, zc_scr, suf_scr, hl_scr, w_scr):
    t = SB_T
    n_q = q_ref.shape[0] // t
    n_iter = tab_ref.shape[1] - PIPE_LAG
    pairs = range(ATT_PAIRS)

    for pp in pairs:
        _fill_vexp(v_ref, _pair_lanes(pp), vexp_scr.at[pp], n_q, t, with_ones=False)
        _fill_qexp(q_ref, _pair_lanes(pp), qexp_scr.at[pp], n_q, t)
    bias_scr[0] = jnp.zeros((t, t), F32)
    bias_scr[1] = _causal_bias(t, strict=True)
    acc_scr[...] = jnp.zeros_like(acc_scr)
    carry_scr[...] = jnp.zeros_like(carry_scr)
    z_scr[...] = jnp.zeros_like(z_scr)
    zc_scr[...] = jnp.zeros_like(zc_scr)
    suf_scr[...] = jnp.zeros_like(suf_scr)

    def keep_stage(pp, slot, mask_idx):
        for hd in range(HEADS_PER_TILE):
            for r in range(0, t, ROW_CHUNK):
                rows = slice(r, r + ROW_CHUNK)
                z = z_scr[pp, hd, rows, :] + bias_scr[mask_idx, rows, :]
                sp = jnp.maximum(z, 0.0) + jnp.log2(1.0 + jnp.exp2(-jnp.abs(z)))
                hi = sp.astype(BF16)
                hl_scr[pp, hd, rows, 0:t] = hi
                hl_scr[pp, hd, rows, t:2 * t] = (sp - hi.astype(F32)).astype(BF16)
                zc_scr[pp, hd, rows, :] = z - _lane_tile(carry_scr[pp, slot, hd, rows, :], t)
            suf = _dot(hl_scr[pp, hd], tri_ref[...])
            suf_scr[pp, hd] = suf
            carry_scr[pp, slot, hd] = carry_scr[pp, slot, hd] + suf[:, 0:1]

    def value_stage(pp, k_blk, slot):
        for hd in range(HEADS_PER_TILE):
            for r in range(0, t, ROW_CHUNK):
                rows = slice(r, r + ROW_CHUNK)
                w_scr[pp, rows, hd * t:(hd + 1) * t] = jnp.exp2(
                    zc_scr[pp, hd, rows, :] - suf_scr[pp, hd, rows, :]).astype(BF16)
        acc_scr[pp, slot] += _dot(w_scr[pp], vexp_scr[pp, k_blk])

    def score_stage(pp, q_blk, k_blk):
        _score_stage(q_blk, k_blk, qexp_scr.at[pp], k_ref, _pair_lanes(pp), z_scr.at[pp], t)

    def body(i, carry):
        for pp in pairs:
            value_stage(pp, tab_ref[1, i], tab_ref[2, i])
        for pp in pairs:
            keep_stage(pp, tab_ref[2, i + 1], tab_ref[3, i + 1])
        for pp in pairs:
            score_stage(pp, tab_ref[0, i + 2], tab_ref[1, i + 2])
        return carry

    lax.fori_loop(0, n_iter, body, 0)

    def tail(pp, q_blk):
        def live():
            return (jnp.min(carry_scr[pp, q_blk]) <= -F32_EXP2_ZERO_BELOW).astype(jnp.int32)

        def cond(state):
            k_blk, go = state
            return jnp.logical_and(k_blk >= 0, go > 0)

        def step(state):
            k_blk, _ = state
            score_stage(pp, q_blk, k_blk)
            keep_stage(pp, q_blk, 0)
            value_stage(pp, k_blk, q_blk)
            return k_blk - 1, live()

        lax.while_loop(cond, step, (q_blk - 2, live()))

    lane = lax.broadcasted_iota(jnp.int32, (t, LANES), 1)
    for pp in pairs:
        def tail_body(q_blk, carry, pp=pp):
            tail(pp, q_blk)
            return carry

        @pl.when(jnp.min(carry_scr[pp, 2:n_q]) <= -F32_EXP2_ZERO_BELOW)
        def _():
            lax.fori_loop(2, n_q, tail_body, 0)

        lanes = _pair_lanes(pp)
        for q_blk in range(n_q):
            o = _head_rmsnorm(acc_scr[pp, q_blk], lane) * g_ref[:, lanes]
            o_ref[q_blk * t:(q_blk + 1) * t, lanes] = o.astype(o_ref.dtype)


def _sb_attention(q, k, v, g, tri):
    bsz, s, width = q.shape
    t = SB_T
    n_q = s // t
    n_tiles = width // LANES
    tasks = []
    for qb in range(n_q):
        tasks.append((qb, qb, qb, 1))
        if qb >= 1:
            tasks.append((qb, qb - 1, qb, 0))
    table = _task_table(tasks, n_q)
    pp, hh = ATT_PAIRS, HEADS_PER_TILE
    seq_spec = pl.BlockSpec((None, s, pp * LANES), lambda b, hp: (b, 0, hp))
    return pl.pallas_call(
        _sb_kernel,
        grid=(bsz, n_tiles // pp),
        in_specs=[pl.BlockSpec(memory_space=pltpu.SMEM), seq_spec, seq_spec, seq_spec,
                  pl.BlockSpec((1, pp * LANES), lambda b, hp: (0, hp)), _const_spec(tri.shape)],
        out_specs=seq_spec,
        out_shape=jax.ShapeDtypeStruct((bsz, s, width), BF16),
        scratch_shapes=[pltpu.VMEM((pp, n_q, 2 * t, LANES), BF16),
                        pltpu.VMEM((pp, hh, s, LANES), BF16),
                        pltpu.VMEM((2, t, t), F32),
                        pltpu.VMEM((pp, n_q + 1, t, LANES), F32),
                        pltpu.VMEM((pp, n_q + 1, hh, t, LANES), F32),
                        pltpu.VMEM((pp, hh, t, t), F32),
                        pltpu.VMEM((pp, hh, t, t), F32),
                        pltpu.VMEM((pp, hh, t, t), F32),
                        pltpu.VMEM((pp, hh, t, 2 * t), BF16),
                        pltpu.VMEM((pp, t, hh * t), BF16)],
        compiler_params=pltpu.CompilerParams(
            dimension_semantics=("arbitrary", "arbitrary"), vmem_limit_bytes=VMEM_LIMIT_BYTES),
        name="sb_attn",
    )(jnp.asarray(table), q, k, v, g.reshape(1, width), tri)


def _fox_kernel(tab_ref, q_ref, k_ref, v_ref, f_ref, gate_ref, g_ref, o_ref, vexp_scr, qexp_scr,
                bias_scr, acc_scr, m_scr, s_scr, p_scr, alpha_scr, bmax_scr):
    t = FOX_T
    n_q = q_ref.shape[0] // t
    n_iter = tab_ref.shape[1] - PIPE_LAG
    pairs = range(ATT_PAIRS)
    lane = lax.broadcasted_iota(jnp.int32, (t, LANES), 1)

    for pp in pairs:
        _fill_vexp(v_ref, _pair_lanes(pp), vexp_scr.at[pp], n_q, t, with_ones=True)
        _fill_qexp(q_ref, _pair_lanes(pp), qexp_scr.at[pp], n_q, t)
    bias_scr[0] = jnp.zeros((t, t), F32)
    bias_scr[1] = _causal_bias(t, strict=False)
    acc_scr[...] = jnp.zeros_like(acc_scr)
    m_scr[...] = jnp.full_like(m_scr, -jnp.inf)
    s_scr[...] = jnp.zeros_like(s_scr)
    bmax_scr[...] = jnp.zeros_like(bmax_scr)
    p_scr[...] = jnp.zeros_like(p_scr)
    alpha_scr[...] = jnp.zeros_like(alpha_scr)

    def score_stage(pp, q_blk, k_blk, mask_idx):
        rows = pl.ds(pl.multiple_of(q_blk * t, t), t)
        cols = pl.ds(pl.multiple_of(k_blk * t, t), t)
        k = k_ref[cols, _pair_lanes(pp)]
        f_keys = f_ref[pp, :, cols] * LOG2E
        bias = bias_scr[mask_idx]
        for hd in range(HEADS_PER_TILE):
            sc = _dot_nt(qexp_scr[pp, hd, rows, :], k) - f_keys[hd:hd + 1, :] + bias
            s_scr[pp, hd] = sc
            bmax_scr[pp, hd] = jnp.broadcast_to(jnp.max(sc, axis=1, keepdims=True), (t, LANES))

    def softmax_stage(pp, slot):
        for hd in range(HEADS_PER_TILE):
            m_old = m_scr[pp, slot, hd]
            m_new = jnp.maximum(m_old, bmax_scr[pp, hd])
            alpha_scr[pp, hd] = jnp.exp2(m_old - m_new)
            m_scr[pp, slot, hd] = m_new
            p_scr[pp, :, hd * t:(hd + 1) * t] = jnp.exp2(
                s_scr[pp, hd] - _lane_tile(m_new, t)).astype(BF16)

    def value_stage(pp, k_blk, slot):
        pv = _dot(p_scr[pp], vexp_scr[pp, k_blk])
        alpha = _per_head(lane, alpha_scr[pp, 0], alpha_scr[pp, 1])
        acc_scr[pp, slot] = acc_scr[pp, slot] * _lane_tile(alpha, 2 * LANES) + pv

    def body(i, carry):
        for pp in pairs:
            value_stage(pp, tab_ref[1, i], tab_ref[2, i])
        for pp in pairs:
            softmax_stage(pp, tab_ref[2, i + 1])
        for pp in pairs:
            score_stage(pp, tab_ref[0, i + 2], tab_ref[1, i + 2], tab_ref[3, i + 2])
        return carry

    lax.fori_loop(0, n_iter, body, 0)

    for pp in pairs:
        lanes = _pair_lanes(pp)
        for q_blk in range(n_q):
            rows = slice(q_blk * t, (q_blk + 1) * t)
            acc = acc_scr[pp, q_blk]
            o = acc[:, 0:LANES] / acc[:, LANES:2 * LANES]
            o = _head_rmsnorm(o, lane) * g_ref[:, lanes] * gate_ref[rows, lanes].astype(F32)
            o_ref[rows, lanes] = o.astype(o_ref.dtype)


def _fox_attention(q, k, v, fcum, gate, g):
    bsz, s, width = q.shape
    t = FOX_T
    n_q = s // t
    n_tiles = width // LANES
    tasks = [(qb, kb, qb, int(kb == qb)) for qb in range(n_q) for kb in range(qb + 1)]
    table = _task_table(tasks, n_q)
    pp, hh = ATT_PAIRS, HEADS_PER_TILE
    seq_spec = pl.BlockSpec((None, s, pp * LANES), lambda b, hp: (b, 0, hp))
    f_tiles = fcum.reshape(bsz, n_tiles, hh, s)
    return pl.pallas_call(
        _fox_kernel,
        grid=(bsz, n_tiles // pp),
        in_specs=[pl.BlockSpec(memory_space=pltpu.SMEM), seq_spec, seq_spec, seq_spec,
                  pl.BlockSpec((None, pp, hh, s), lambda b, hp: (b, hp, 0, 0)),
                  seq_spec, pl.BlockSpec((1, pp * LANES), lambda b, hp: (0, hp))],
        out_specs=seq_spec,
        out_shape=jax.ShapeDtypeStruct((bsz, s, width), BF16),
        scratch_shapes=[pltpu.VMEM((pp, n_q, 2 * t, 2 * LANES), BF16),
                        pltpu.VMEM((pp, hh, s, LANES), BF16),
                        pltpu.VMEM((2, t, t), F32),
                        pltpu.VMEM((pp, n_q + 1, t, 2 * LANES), F32),
                        pltpu.VMEM((pp, n_q + 1, hh, t, LANES), F32),
                        pltpu.VMEM((pp, hh, t, t), F32),
                        pltpu.VMEM((pp, t, hh * t), BF16),
                        pltpu.VMEM((pp, hh, t, LANES), F32),
                        pltpu.VMEM((pp, hh, t, LANES), F32)],
        compiler_params=pltpu.CompilerParams(
            dimension_semantics=("arbitrary", "arbitrary"), vmem_limit_bytes=VMEM_LIMIT_BYTES),
        name="fox_attn",
    )(jnp.asarray(table), q, k, v, f_tiles, gate, g.reshape(1, width))


def kernel(x, c, w_ada, b_ada, g_ffn1, w_ffn1_up, w_ffn1_down, g_mix, w_in, b_forget, g_fox_q,
           g_fox_k, g_sb_out, g_fox_out, w_out, g_ffn2, w_ffn2_up, w_ffn2_down, g_final):
    bsz, s, d = x.shape
    depth = w_ada.shape[0]
    t = SB_T

    head_of = jnp.arange(FOX_WIDTH) // HEAD_DIM
    grp = jnp.where(head_of[:, None] == head_of[None, :], 1.0 / HEAD_DIM, 0.0).astype(BF16)
    key = jnp.arange(t)
    tri1 = (key[:, None] >= key[None, :]).astype(BF16)
    tri = jnp.concatenate([tri1, tri1], axis=0)

    n_main = 3 * SB_WIDTH + 4 * FOX_WIDTH
    for l in range(depth):
        ada = _ada(c, w_ada[l], b_ada[l]).reshape(bsz, N_SUBLAYERS, 3, 1, d)
        shift, scale, gate = ada[:, :, 0], ada[:, :, 1], ada[:, :, 2]

        x = _ffn(x, shift[:, 0], scale[:, 0], gate[:, 0], g_ffn1[l],
                 w_ffn1_up[l], w_ffn1_down[l])

        w_f = jnp.pad(w_in[l][:, n_main:], ((0, 0), (0, LANES - FOX_HEADS))).astype(BF16)
        sbq, sbk, sbv, fxq, fxk, fxv, fxg, fcum = _in_proj(
            x, shift[:, 1], scale[:, 1], g_mix[l], w_in[l], w_f,
            b_forget[l].reshape(FOX_HEADS, 1),
            jnp.tile(g_fox_q[l], FOX_HEADS).reshape(1, FOX_WIDTH),
            jnp.tile(g_fox_k[l], FOX_HEADS).reshape(1, FOX_WIDTH), grp)

        o_sb = _sb_attention(sbq, sbk, sbv, g_sb_out[l], tri)
        o_fx = _fox_attention(fxq, fxk, fxv, fcum, fxg, g_fox_out[l])

        x = _ffn(x, shift[:, 2], scale[:, 2], gate[:, 2], g_ffn2[l],
                 w_ffn2_up[l], w_ffn2_down[l],
                 mix=(o_sb, o_fx, w_out[l], gate[:, 1]),
                 g_final=g_final if l == depth - 1 else None)
    return x
```

```python
import functools

import numpy as np

import jax
import jax.numpy as jnp
from jax import lax
from jax.experimental import pallas as pl
from jax.experimental.pallas import tpu as pltpu

D_MODEL = 1024
HEAD_DIM = 64
SB_HEADS = 8
FOX_HEADS = 8
SB_WIDTH = SB_HEADS * HEAD_DIM
FOX_WIDTH = FOX_HEADS * HEAD_DIM
FFN_HIDDEN = 2816
N_SUBLAYERS = 3
EPS = 1e-6
QK_SCALE = HEAD_DIM ** -0.5

LANES = 128
HEADS_PER_TILE = LANES // HEAD_DIM
VMEM_LIMIT_BYTES = 56 * 1024 * 1024

ADA_TN = 1152
FFN_TM = 512
FFN_HC = 256
PROJ_TM = 512
SB_T = 256
FOX_T = 512
ATT_PAIRS = 2
PIPE_LAG = 2
ROW_CHUNK = 32

LOG2E = 1.4426950408889634
F32_EXP2_ZERO_BELOW = -150.0
MASKED = -1e30

BF16 = jnp.bfloat16
F32 = jnp.float32


def _dot(a, b):
    return jnp.dot(a, b, preferred_element_type=F32)


def _dot_nt(a, b):
    return lax.dot_general(a, b, (((1,), (1,)), ((), ())), preferred_element_type=F32)


def _sigmoid(x):
    return 1.0 / (1.0 + jnp.exp(-x))


def _softplus(x):
    return jnp.maximum(x, 0.0) + jnp.log(1.0 + jnp.exp(-jnp.abs(x)))


def _modulate(x, g, shift, scale):
    ms = jnp.mean(x * x, axis=-1, keepdims=True)
    return (x * lax.rsqrt(ms + EPS)) * (g * (1.0 + scale)) + shift


def _const_spec(shape):
    return pl.BlockSpec(shape, lambda *_: (0,) * len(shape), pipeline_mode=pl.Buffered(1))


def _ada_kernel(c_ref, w_ref, b_ref, o_ref):
    c = c_ref[...]
    cond = (c * _sigmoid(c)).astype(BF16)
    o_ref[...] = _dot(cond, w_ref[...].astype(BF16)) + b_ref[...]


def _ada(c, w, b):
    bsz, d = c.shape
    n = w.shape[1]
    return pl.pallas_call(
        _ada_kernel,
        grid=(n // ADA_TN,),
        in_specs=[
            pl.BlockSpec((bsz, d), lambda j: (0, 0)),
            pl.BlockSpec((d, ADA_TN), lambda j: (0, j)),
            pl.BlockSpec((1, ADA_TN), lambda j: (0, j)),
        ],
        out_specs=pl.BlockSpec((bsz, ADA_TN), lambda j: (0, j)),
        out_shape=jax.ShapeDtypeStruct((bsz, n), F32),
        compiler_params=pltpu.CompilerParams(
            dimension_semantics=("arbitrary",), vmem_limit_bytes=VMEM_LIMIT_BYTES),
        name="ada",
    )(c, w, b.reshape(1, n))


def _ffn_kernel(*refs, pre_mix, final_norm):
    it = iter(refs)
    x_ref = next(it)
    if pre_mix:
        osb_ref, ofx_ref, wout_ref, gate_mix_ref = next(it), next(it), next(it), next(it)
    shift_ref, scale_ref, gate_ref, g_ref = next(it), next(it), next(it), next(it)
    wup_ref, wdown_ref = next(it), next(it)
    if final_norm:
        gf_ref = next(it)
    o_ref, h_scr, acc_scr = next(it), next(it), next(it)

    x = x_ref[...]
    if pre_mix:
        mix = (_dot(osb_ref[...], wout_ref[0:SB_WIDTH, :].astype(BF16))
               + _dot(ofx_ref[...], wout_ref[SB_WIDTH:SB_WIDTH + FOX_WIDTH, :].astype(BF16)))
        x = x + gate_mix_ref[...] * mix
    o_ref[...] = x
    h_scr[...] = _modulate(x, g_ref[...], shift_ref[...], scale_ref[...]).astype(BF16)
    acc_scr[...] = jnp.zeros_like(acc_scr)

    hidden = wdown_ref.shape[0]
    for lo in range(0, hidden, FFN_HC):
        h = h_scr[...]
        gt = _dot(h, wup_ref[:, lo:lo + FFN_HC].astype(BF16))
        up = _dot(h, wup_ref[:, hidden + lo:hidden + lo + FFN_HC].astype(BF16))
        a = (gt * _sigmoid(gt) * up).astype(BF16)
        acc_scr[...] += _dot(a, wdown_ref[lo:lo + FFN_HC, :].astype(BF16))
    y = o_ref[...] + (0.5 * gate_ref[...]) * acc_scr[...]
    if final_norm:
        ms = jnp.mean(y * y, axis=-1, keepdims=True)
        y = (y * lax.rsqrt(ms + EPS)) * gf_ref[...]
    o_ref[...] = y


def _ffn(x, shift, scale, gate, g, w_up, w_down, mix=None, g_final=None):
    bsz, s, d = x.shape
    assert w_down.shape[0] % FFN_HC == 0
    tm = FFN_TM
    tok = pl.BlockSpec((None, tm, d), lambda b, i: (b, i, 0))
    per_batch = pl.BlockSpec((None, 1, d), lambda b, i: (b, 0, 0))
    args, specs = [x], [tok]
    if mix is not None:
        o_sb, o_fx, w_out, gate_mix = mix
        args += [o_sb, o_fx, w_out, gate_mix]
        specs += [pl.BlockSpec((None, tm, SB_WIDTH), lambda b, i: (b, i, 0)),
                  pl.BlockSpec((None, tm, FOX_WIDTH), lambda b, i: (b, i, 0)),
                  _const_spec(w_out.shape), per_batch]
    args += [shift, scale, gate, g.reshape(1, d), w_up, w_down]
    specs += [per_batch, per_batch, per_batch, _const_spec((1, d)),
              _const_spec(w_up.shape), _const_spec(w_down.shape)]
    if g_final is not None:
        args.append(g_final.reshape(1, d))
        specs.append(_const_spec((1, d)))
    return pl.pallas_call(
        functools.partial(_ffn_kernel, pre_mix=mix is not None, final_norm=g_final is not None),
        grid=(bsz, s // tm),
        in_specs=specs,
        out_specs=tok,
        out_shape=jax.ShapeDtypeStruct((bsz, s, d), F32),
        scratch_shapes=[pltpu.VMEM((tm, d), BF16), pltpu.VMEM((tm, d), F32)],
        compiler_params=pltpu.CompilerParams(
            dimension_semantics=("arbitrary", "arbitrary"), vmem_limit_bytes=VMEM_LIMIT_BYTES),
        name="ffn_mix" if mix is not None else "ffn",
    )(*args)


def _lane_cumsum(y):
    n = y.shape[-1]
    lane = lax.broadcasted_iota(jnp.int32, y.shape, y.ndim - 1)
    shift = 1
    while shift < n:
        y = y + jnp.where(lane >= shift, pltpu.roll(y, shift, axis=y.ndim - 1), 0.0)
        shift *= 2
    return y


def _proj_kernel(x_ref, shift_ref, scale_ref, g_ref, w_ref, wf_ref, bf_ref, gq_ref, gk_ref,
                 grp_ref, sbq_ref, sbk_ref, sbv_ref, fxq_ref, fxk_ref, fxv_ref, fxg_ref,
                 fcum_ref, h_scr, carry_scr):
    @pl.when(pl.program_id(1) == 0)
    def _():
        carry_scr[...] = jnp.zeros_like(carry_scr)

    x = x_ref[...]
    h_scr[...] = _modulate(x, g_ref[...], shift_ref[...], scale_ref[...]).astype(BF16)
    h = h_scr[...]
    w = SB_WIDTH

    def proj(idx):
        return _dot(h, w_ref[:, idx * w:(idx + 1) * w].astype(BF16))

    def head_rms(p):
        ms = _dot((p * p).astype(BF16), grp_ref[...])
        return p * lax.rsqrt(ms + EPS)

    f = _dot(h, wf_ref[...])
    ft = jnp.transpose(f)[0:FOX_HEADS, :] + bf_ref[...]
    log_f = -_softplus(-ft)
    fcum = _lane_cumsum(log_f) + carry_scr[:, 0:1]
    fcum_ref[...] = fcum
    carry_scr[...] = jnp.broadcast_to(fcum[:, fcum.shape[1] - 1:], carry_scr.shape)

    fxq_ref[...] = (head_rms(proj(3)) * (gq_ref[...] * (QK_SCALE * LOG2E))).astype(BF16)
    fxk_ref[...] = (head_rms(proj(4)) * gk_ref[...]).astype(BF16)
    fxg_ref[...] = _sigmoid(proj(6)).astype(BF16)
    sbq_ref[...] = (proj(0) * (QK_SCALE * LOG2E)).astype(BF16)
    sbk_ref[...] = proj(1).astype(BF16)
    sbv_ref[...] = proj(2).astype(BF16)
    fxv_ref[...] = proj(5).astype(BF16)


def _in_proj(x, shift, scale, g, w_main, w_f, b_f, gq, gk, grp):
    bsz, s, d = x.shape
    tm = PROJ_TM
    tok = pl.BlockSpec((None, tm, d), lambda b, i: (b, i, 0))
    per_batch = pl.BlockSpec((None, 1, d), lambda b, i: (b, 0, 0))
    head_out = pl.BlockSpec((None, tm, SB_WIDTH), lambda b, i: (b, i, 0))
    head_shape = jax.ShapeDtypeStruct((bsz, s, SB_WIDTH), BF16)
    return pl.pallas_call(
        _proj_kernel,
        grid=(bsz, s // tm),
        in_specs=[tok, per_batch, per_batch, _const_spec((1, d)), _const_spec(w_main.shape),
                  _const_spec(w_f.shape), _const_spec(b_f.shape), _const_spec(gq.shape),
                  _const_spec(gk.shape), _const_spec(grp.shape)],
        out_specs=[head_out] * 7 + [pl.BlockSpec((None, FOX_HEADS, tm), lambda b, i: (b, 0, i))],
        out_shape=[head_shape] * 7 + [jax.ShapeDtypeStruct((bsz, FOX_HEADS, s), F32)],
        scratch_shapes=[pltpu.VMEM((tm, d), BF16), pltpu.VMEM((FOX_HEADS, LANES), F32)],
        compiler_params=pltpu.CompilerParams(
            dimension_semantics=("arbitrary", "arbitrary"), vmem_limit_bytes=VMEM_LIMIT_BYTES),
        name="in_proj",
    )(x, shift, scale, g.reshape(1, d), w_main, w_f, b_f, gq, gk, grp)


def _split_heads(x):
    lane = lax.broadcasted_iota(jnp.int32, x.shape, 1)
    zero = jnp.zeros_like(x)
    return jnp.where(lane < HEAD_DIM, x, zero), jnp.where(lane >= HEAD_DIM, x, zero)


def _pair_lanes(pp):
    return slice(pp * LANES, (pp + 1) * LANES)


def _fill_vexp(v_ref, lanes, vexp_ref, n_blocks, t, with_ones):
    def body(j, carry):
        v0, v1 = _split_heads(v_ref[pl.ds(pl.multiple_of(j * t, t), t), lanes])
        vexp_ref[j, 0:t, 0:LANES] = v0
        vexp_ref[j, t:2 * t, 0:LANES] = v1
        if with_ones:
            lane = lax.broadcasted_iota(jnp.int32, (t, LANES), 1)
            vexp_ref[j, 0:t, LANES:2 * LANES] = (lane < HEAD_DIM).astype(F32).astype(vexp_ref.dtype)
            vexp_ref[j, t:2 * t, LANES:2 * LANES] = (lane >= HEAD_DIM).astype(F32).astype(
                vexp_ref.dtype)
        return carry

    lax.fori_loop(0, n_blocks, body, 0)


def _per_head(lane, a0, a1):
    return jnp.where(lane < HEAD_DIM, a0, a1)


def _lane_tile(a, width):
    return jnp.concatenate([a] * (width // LANES), axis=1)


def _head_rmsnorm(o, lane):
    sq = o * o
    ss0 = jnp.sum(jnp.where(lane < HEAD_DIM, sq, 0.0), axis=1, keepdims=True)
    ss1 = jnp.sum(jnp.where(lane >= HEAD_DIM, sq, 0.0), axis=1, keepdims=True)
    inv = _per_head(lane, lax.rsqrt(ss0 * (1.0 / HEAD_DIM) + EPS),
                    lax.rsqrt(ss1 * (1.0 / HEAD_DIM) + EPS))
    return o * inv


def _task_table(tasks, n_q):
    idle = [(0, 0, n_q, 0)] * PIPE_LAG
    return np.asarray(idle + list(tasks) + idle, dtype=np.int32).T.copy()


def _fill_qexp(q_ref, lanes, qexp_ref, n_blocks, t):
    def body(j, carry):
        rows = pl.ds(pl.multiple_of(j * t, t), t)
        q0, q1 = _split_heads(q_ref[rows, lanes])
        qexp_ref[0, rows, :] = q0
        qexp_ref[1, rows, :] = q1
        return carry

    lax.fori_loop(0, n_blocks, body, 0)


def _score_stage(q_blk, k_blk, qexp_ref, k_ref, lanes, s_ref, t):
    rows = pl.ds(pl.multiple_of(q_blk * t, t), t)
    k = k_ref[pl.ds(pl.multiple_of(k_blk * t, t), t), lanes]
    for hd in range(HEADS_PER_TILE):
        s_ref[hd] = _dot_nt(qexp_ref[hd, rows, :], k)


def _causal_bias(t, strict):
    row = lax.broadcasted_iota(jnp.int32, (t, t), 0)
    col = lax.broadcasted_iota(jnp.int32, (t, t), 1)
    keep = col < row if strict else col <= row
    return jnp.where(keep, 0.0, MASKED).astype(F32)


def _sb_kernel(tab_ref, q_ref, k_ref, v_ref, g_ref, tri_ref, o_ref, vexp_scr, qexp_scr, bias_scr,
               acc_scr, carry_scr, z_scr, zc_scr, suf_scr, hl_scr, w_scr, l_scr):
    t = SB_T
    n_q = q_ref.shape[0] // t
    n_iter = tab_ref.shape[1] - PIPE_LAG
    pairs = range(ATT_PAIRS)

    for pp in pairs:
        _fill_vexp(v_ref, _pair_lanes(pp), vexp_scr.at[pp], n_q, t, with_ones=False)
        _fill_qexp(q_ref, _pair_lanes(pp), qexp_scr.at[pp], n_q, t)
    bias_scr[0] = jnp.zeros((t, t), F32)
    bias_scr[1] = _causal_bias(t, strict=True)
    acc_scr[...] = jnp.zeros_like(acc_scr)
    carry_scr[...] = jnp.zeros_like(carry_scr)
    z_scr[...] = jnp.zeros_like(z_scr)
    zc_scr[...] = jnp.zeros_like(zc_scr)
    suf_scr[...] = jnp.zeros_like(suf_scr)
    l_scr[...] = jnp.zeros_like(l_scr)

    def score_stage(pp, q_blk, k_blk, mask_idx):
        rows = pl.ds(pl.multiple_of(q_blk * t, t), t)
        k = k_ref[pl.ds(pl.multiple_of(k_blk * t, t), t), _pair_lanes(pp)]
        bias = bias_scr[mask_idx]
        for hd in range(HEADS_PER_TILE):
            z = _dot_nt(qexp_scr[pp, hd, rows, :], k) + bias
            z_scr[pp, hd] = z
            l_scr[pp, hd] = jnp.log2(1.0 + jnp.exp2(-jnp.abs(z)))

    def keep_stage(pp, slot):
        for hd in range(HEADS_PER_TILE):
            for r in range(0, t, ROW_CHUNK):
                rows = slice(r, r + ROW_CHUNK)
                z = z_scr[pp, hd, rows, :]
                sp = jnp.maximum(z, 0.0) + l_scr[pp, hd, rows, :]
                hi = sp.astype(BF16)
                hl_scr[pp, hd, rows, 0:t] = hi
                hl_scr[pp, hd, rows, t:2 * t] = (sp - hi.astype(F32)).astype(BF16)
                zc_scr[pp, hd, rows, :] = z - _lane_tile(carry_scr[pp, slot, hd, rows, :], t)
            suf = _dot(hl_scr[pp, hd], tri_ref[...])
            suf_scr[pp, hd] = suf
            carry_scr[pp, slot, hd] = carry_scr[pp, slot, hd] + suf[:, 0:1]

    def value_stage(pp, k_blk, slot):
        for hd in range(HEADS_PER_TILE):
            for r in range(0, t, ROW_CHUNK):
                rows = slice(r, r + ROW_CHUNK)
                w_scr[pp, rows, hd * t:(hd + 1) * t] = jnp.exp2(
                    zc_scr[pp, hd, rows, :] - suf_scr[pp, hd, rows, :]).astype(BF16)
        acc_scr[pp, slot] += _dot(w_scr[pp], vexp_scr[pp, k_blk])

    def body(i, carry):
        for pp in pairs:
            value_stage(pp, tab_ref[1, i], tab_ref[2, i])
        for pp in pairs:
            keep_stage(pp, tab_ref[2, i + 1])
        for pp in pairs:
            score_stage(pp, tab_ref[0, i + 2], tab_ref[1, i + 2], tab_ref[3, i + 2])
        return carry

    lax.fori_loop(0, n_iter, body, 0)

    def tail(pp, q_blk):
        def live():
            return (jnp.min(carry_scr[pp, q_blk]) <= -F32_EXP2_ZERO_BELOW).astype(jnp.int32)

        def cond(state):
            k_blk, go = state
            return jnp.logical_and(k_blk >= 0, go > 0)

        def step(state):
            k_blk, _ = state
            score_stage(pp, q_blk, k_blk, 0)
            keep_stage(pp, q_blk)
            value_stage(pp, k_blk, q_blk)
            return k_blk - 1, live()

        lax.while_loop(cond, step, (q_blk - 2, live()))

    lane = lax.broadcasted_iota(jnp.int32, (t, LANES), 1)
    for pp in pairs:
        def tail_body(q_blk, carry, pp=pp):
            tail(pp, q_blk)
            return carry

        @pl.when(jnp.min(carry_scr[pp, 2:n_q]) <= -F32_EXP2_ZERO_BELOW)
        def _():
            lax.fori_loop(2, n_q, tail_body, 0)

        lanes = _pair_lanes(pp)
        for q_blk in range(n_q):
            o = _head_rmsnorm(acc_scr[pp, q_blk], lane) * g_ref[:, lanes]
            o_ref[q_blk * t:(q_blk + 1) * t, lanes] = o.astype(o_ref.dtype)


def _sb_attention(q, k, v, g, tri):
    bsz, s, width = q.shape
    t = SB_T
    n_q = s // t
    n_tiles = width // LANES
    tasks = []
    for qb in range(n_q):
        tasks.append((qb, qb, qb, 1))
        if qb >= 1:
            tasks.append((qb, qb - 1, qb, 0))
    table = _task_table(tasks, n_q)
    pp, hh = ATT_PAIRS, HEADS_PER_TILE
    seq_spec = pl.BlockSpec((None, s, pp * LANES), lambda b, hp: (b, 0, hp))
    return pl.pallas_call(
        _sb_kernel,
        grid=(bsz, n_tiles // pp),
        in_specs=[pl.BlockSpec(memory_space=pltpu.SMEM), seq_spec, seq_spec, seq_spec,
                  pl.BlockSpec((1, pp * LANES), lambda b, hp: (0, hp)), _const_spec(tri.shape)],
        out_specs=seq_spec,
        out_shape=jax.ShapeDtypeStruct((bsz, s, width), BF16),
        scratch_shapes=[pltpu.VMEM((pp, n_q, 2 * t, LANES), BF16),
                        pltpu.VMEM((pp, hh, s, LANES), BF16),
                        pltpu.VMEM((2, t, t), F32),
                        pltpu.VMEM((pp, n_q + 1, t, LANES), F32),
                        pltpu.VMEM((pp, n_q + 1, hh, t, LANES), F32),
                        pltpu.VMEM((pp, hh, t, t), F32),
                        pltpu.VMEM((pp, hh, t, t), F32),
                        pltpu.VMEM((pp, hh, t, t), F32),
                        pltpu.VMEM((pp, hh, t, 2 * t), BF16),
                        pltpu.VMEM((pp, t, hh * t), BF16),
                        pltpu.VMEM((pp, hh, t, t), F32)],
        compiler_params=pltpu.CompilerParams(
            dimension_semantics=("arbitrary", "arbitrary"), vmem_limit_bytes=VMEM_LIMIT_BYTES),
        name="sb_attn",
    )(jnp.asarray(table), q, k, v, g.reshape(1, width), tri)


def _fox_kernel(tab_ref, q_ref, k_ref, v_ref, f_ref, gate_ref, g_ref, o_ref, vexp_scr, qexp_scr,
                bias_scr, acc_scr, m_scr, s_scr, p_scr, alpha_scr, bmax_scr):
    t = FOX_T
    n_q = q_ref.shape[0] // t
    n_iter = tab_ref.shape[1] - PIPE_LAG
    pairs = range(ATT_PAIRS)
    lane = lax.broadcasted_iota(jnp.int32, (t, LANES), 1)

    for pp in pairs:
        _fill_vexp(v_ref, _pair_lanes(pp), vexp_scr.at[pp], n_q, t, with_ones=True)
        _fill_qexp(q_ref, _pair_lanes(pp), qexp_scr.at[pp], n_q, t)
    bias_scr[0] = jnp.zeros((t, t), F32)
    bias_scr[1] = _causal_bias(t, strict=False)
    acc_scr[...] = jnp.zeros_like(acc_scr)
    m_scr[...] = jnp.full_like(m_scr, -jnp.inf)
    s_scr[...] = jnp.zeros_like(s_scr)
    bmax_scr[...] = jnp.zeros_like(bmax_scr)
    p_scr[...] = jnp.zeros_like(p_scr)
    alpha_scr[...] = jnp.zeros_like(alpha_scr)

    def score_stage(pp, q_blk, k_blk, mask_idx):
        rows = pl.ds(pl.multiple_of(q_blk * t, t), t)
        cols = pl.ds(pl.multiple_of(k_blk * t, t), t)
        k = k_ref[cols, _pair_lanes(pp)]
        f_keys = f_ref[pp, :, cols] * LOG2E
        bias = bias_scr[mask_idx]
        for hd in range(HEADS_PER_TILE):
            sc = _dot_nt(qexp_scr[pp, hd, rows, :], k) - f_keys[hd:hd + 1, :] + bias
            s_scr[pp, hd] = sc
            bmax_scr[pp, hd] = jnp.broadcast_to(jnp.max(sc, axis=1, keepdims=True), (t, LANES))

    def softmax_stage(pp, slot):
        for hd in range(HEADS_PER_TILE):
            m_old = m_scr[pp, slot, hd]
            m_new = jnp.maximum(m_old, bmax_scr[pp, hd])
            alpha_scr[pp, hd] = jnp.exp2(m_old - m_new)
            m_scr[pp, slot, hd] = m_new
            p_scr[pp, :, hd * t:(hd + 1) * t] = jnp.exp2(
                s_scr[pp, hd] - _lane_tile(m_new, t)).astype(BF16)

    def value_stage(pp, k_blk, slot):
        pv = _dot(p_scr[pp], vexp_scr[pp, k_blk])
        alpha = _per_head(lane, alpha_scr[pp, 0], alpha_scr[pp, 1])
        acc_scr[pp, slot] = acc_scr[pp, slot] * _lane_tile(alpha, 2 * LANES) + pv

    def body(i, carry):
        for pp in pairs:
            value_stage(pp, tab_ref[1, i], tab_ref[2, i])
        for pp in pairs:
            softmax_stage(pp, tab_ref[2, i + 1])
        for pp in pairs:
            score_stage(pp, tab_ref[0, i + 2], tab_ref[1, i + 2], tab_ref[3, i + 2])
        return carry

    lax.fori_loop(0, n_iter, body, 0)

    for pp in pairs:
        lanes = _pair_lanes(pp)
        for q_blk in range(n_q):
            rows = slice(q_blk * t, (q_blk + 1) * t)
            acc = acc_scr[pp, q_blk]
            o = acc[:, 0:LANES] / acc[:, LANES:2 * LANES]
            o = _head_rmsnorm(o, lane) * g_ref[:, lanes] * gate_ref[rows, lanes].astype(F32)
            o_ref[rows, lanes] = o.astype(o_ref.dtype)


def _fox_attention(q, k, v, fcum, gate, g):
    bsz, s, width = q.shape
    t = FOX_T
    n_q = s // t
    n_tiles = width // LANES
    tasks = [(qb, kb, qb, int(kb == qb)) for qb in range(n_q) for kb in range(qb + 1)]
    table = _task_table(tasks, n_q)
    pp, hh = ATT_PAIRS, HEADS_PER_TILE
    seq_spec = pl.BlockSpec((None, s, pp * LANES), lambda b, hp: (b, 0, hp))
    f_tiles = fcum.reshape(bsz, n_tiles, hh, s)
    return pl.pallas_call(
        _fox_kernel,
        grid=(bsz, n_tiles // pp),
        in_specs=[pl.BlockSpec(memory_space=pltpu.SMEM), seq_spec, seq_spec, seq_spec,
                  pl.BlockSpec((None, pp, hh, s), lambda b, hp: (b, hp, 0, 0)),
                  seq_spec, pl.BlockSpec((1, pp * LANES), lambda b, hp: (0, hp))],
        out_specs=seq_spec,
        out_shape=jax.ShapeDtypeStruct((bsz, s, width), BF16),
        scratch_shapes=[pltpu.VMEM((pp, n_q, 2 * t, 2 * LANES), BF16),
                        pltpu.VMEM((pp, hh, s, LANES), BF16),
                        pltpu.VMEM((2, t, t), F32),
                        pltpu.VMEM((pp, n_q + 1, t, 2 * LANES), F32),
                        pltpu.VMEM((pp, n_q + 1, hh, t, LANES), F32),
                        pltpu.VMEM((pp, hh, t, t), F32),
                        pltpu.VMEM((pp, t, hh * t), BF16),
                        pltpu.VMEM((pp, hh, t, LANES), F32),
                        pltpu.VMEM((pp, hh, t, LANES), F32)],
        compiler_params=pltpu.CompilerParams(
            dimension_semantics=("arbitrary", "arbitrary"), vmem_limit_bytes=VMEM_LIMIT_BYTES),
        name="fox_attn",
    )(jnp.asarray(table), q, k, v, f_tiles, gate, g.reshape(1, width))


def kernel(x, c, w_ada, b_ada, g_ffn1, w_ffn1_up, w_ffn1_down, g_mix, w_in, b_forget, g_fox_q,
           g_fox_k, g_sb_out, g_fox_out, w_out, g_ffn2, w_ffn2_up, w_ffn2_down, g_final):
    bsz, s, d = x.shape
    depth = w_ada.shape[0]
    t = SB_T

    head_of = jnp.arange(FOX_WIDTH) // HEAD_DIM
    grp = jnp.where(head_of[:, None] == head_of[None, :], 1.0 / HEAD_DIM, 0.0).astype(BF16)
    key = jnp.arange(t)
    tri1 = (key[:, None] >= key[None, :]).astype(BF16)
    tri = jnp.concatenate([tri1, tri1], axis=0)

    n_main = 3 * SB_WIDTH + 4 * FOX_WIDTH
    for l in range(depth):
        ada = _ada(c, w_ada[l], b_ada[l]).reshape(bsz, N_SUBLAYERS, 3, 1, d)
        shift, scale, gate = ada[:, :, 0], ada[:, :, 1], ada[:, :, 2]

        x = _ffn(x, shift[:, 0], scale[:, 0], gate[:, 0], g_ffn1[l],
                 w_ffn1_up[l], w_ffn1_down[l])

        w_f = jnp.pad(w_in[l][:, n_main:], ((0, 0), (0, LANES - FOX_HEADS))).astype(BF16)
        sbq, sbk, sbv, fxq, fxk, fxv, fxg, fcum = _in_proj(
            x, shift[:, 1], scale[:, 1], g_mix[l], w_in[l], w_f,
            b_forget[l].reshape(FOX_HEADS, 1),
            jnp.tile(g_fox_q[l], FOX_HEADS).reshape(1, FOX_WIDTH),
            jnp.tile(g_fox_k[l], FOX_HEADS).reshape(1, FOX_WIDTH), grp)

        o_sb = _sb_attention(sbq, sbk, sbv, g_sb_out[l], tri)
        o_fx = _fox_attention(fxq, fxk, fxv, fcum, fxg, g_fox_out[l])

        x = _ffn(x, shift[:, 2], scale[:, 2], gate[:, 2], g_ffn2[l],
                 w_ffn2_up[l], w_ffn2_down[l],
                 mix=(o_sb, o_fx, w_out[l], gate[:, 1]),
                 g_final=g_final if l == depth - 1 else None)
    return x
```

```python
import functools

import numpy as np

import jax
import jax.numpy as jnp
from jax import lax
from jax.experimental import pallas as pl
from jax.experimental.pallas import tpu as pltpu

D_MODEL = 1024
HEAD_DIM = 64
SB_HEADS = 8
FOX_HEADS = 8
SB_WIDTH = SB_HEADS * HEAD_DIM
FOX_WIDTH = FOX_HEADS * HEAD_DIM
FFN_HIDDEN = 2816
N_SUBLAYERS = 3
EPS = 1e-6
QK_SCALE = HEAD_DIM ** -0.5

LANES = 128
HEADS_PER_TILE = LANES // HEAD_DIM
VMEM_LIMIT_BYTES = 56 * 1024 * 1024

ADA_TN = 1152
FFN_TM = 512
FFN_HC = 256
PROJ_TM = 512
SB_T = 256
FOX_T = 512
ATT_PAIRS = 2
PIPE_LAG = 2
ROW_CHUNK = 32

LOG2E = 1.4426950408889634
F32_EXP2_ZERO_BELOW = -150.0
MASKED = -1e30

BF16 = jnp.bfloat16
F32 = jnp.float32


def _dot(a, b):
    return jnp.dot(a, b, preferred_element_type=F32)


def _dot_nt(a, b):
    return lax.dot_general(a, b, (((1,), (1,)), ((), ())), preferred_element_type=F32)


def _sigmoid(x):
    return 1.0 / (1.0 + jnp.exp(-x))


def _softplus(x):
    return jnp.maximum(x, 0.0) + jnp.log(1.0 + jnp.exp(-jnp.abs(x)))


def _modulate(x, g, shift, scale):
    ms = jnp.mean(x * x, axis=-1, keepdims=True)
    return (x * lax.rsqrt(ms + EPS)) * (g * (1.0 + scale)) + shift


def _const_spec(shape):
    return pl.BlockSpec(shape, lambda *_: (0,) * len(shape), pipeline_mode=pl.Buffered(1))


def _ada_kernel(c_ref, w_ref, b_ref, o_ref):
    c = c_ref[...]
    cond = (c * _sigmoid(c)).astype(BF16)
    o_ref[...] = _dot(cond, w_ref[...].astype(BF16)) + b_ref[...]


def _ada(c, w, b):
    bsz, d = c.shape
    n = w.shape[1]
    return pl.pallas_call(
        _ada_kernel,
        grid=(n // ADA_TN,),
        in_specs=[
            pl.BlockSpec((bsz, d), lambda j: (0, 0)),
            pl.BlockSpec((d, ADA_TN), lambda j: (0, j)),
            pl.BlockSpec((1, ADA_TN), lambda j: (0, j)),
        ],
        out_specs=pl.BlockSpec((bsz, ADA_TN), lambda j: (0, j)),
        out_shape=jax.ShapeDtypeStruct((bsz, n), F32),
        compiler_params=pltpu.CompilerParams(
            dimension_semantics=("arbitrary",), vmem_limit_bytes=VMEM_LIMIT_BYTES),
        name="ada",
    )(c, w, b.reshape(1, n))


def _ffn_kernel(*refs, pre_mix, final_norm):
    it = iter(refs)
    x_ref = next(it)
    if pre_mix:
        osb_ref, ofx_ref, wout_ref, gate_mix_ref = next(it), next(it), next(it), next(it)
    shift_ref, scale_ref, gate_ref, g_ref = next(it), next(it), next(it), next(it)
    wup_ref, wdown_ref = next(it), next(it)
    if final_norm:
        gf_ref = next(it)
    o_ref, h_scr, acc_scr = next(it), next(it), next(it)

    x = x_ref[...]
    if pre_mix:
        mix = (_dot(osb_ref[...], wout_ref[0:SB_WIDTH, :].astype(BF16))
               + _dot(ofx_ref[...], wout_ref[SB_WIDTH:SB_WIDTH + FOX_WIDTH, :].astype(BF16)))
        x = x + gate_mix_ref[...] * mix
    o_ref[...] = x
    h_scr[...] = _modulate(x, g_ref[...], shift_ref[...], scale_ref[...]).astype(BF16)
    acc_scr[...] = jnp.zeros_like(acc_scr)

    hidden = wdown_ref.shape[0]
    for lo in range(0, hidden, FFN_HC):
        h = h_scr[...]
        gt = _dot(h, wup_ref[:, lo:lo + FFN_HC].astype(BF16))
        up = _dot(h, wup_ref[:, hidden + lo:hidden + lo + FFN_HC].astype(BF16))
        a = (gt * _sigmoid(gt) * up).astype(BF16)
        acc_scr[...] += _dot(a, wdown_ref[lo:lo + FFN_HC, :].astype(BF16))
    y = o_ref[...] + (0.5 * gate_ref[...]) * acc_scr[...]
    if final_norm:
        ms = jnp.mean(y * y, axis=-1, keepdims=True)
        y = (y * lax.rsqrt(ms + EPS)) * gf_ref[...]
    o_ref[...] = y


def _ffn(x, shift, scale, gate, g, w_up, w_down, mix=None, g_final=None):
    bsz, s, d = x.shape
    assert w_down.shape[0] % FFN_HC == 0
    tm = FFN_TM
    tok = pl.BlockSpec((None, tm, d), lambda b, i: (b, i, 0))
    per_batch = pl.BlockSpec((None, 1, d), lambda b, i: (b, 0, 0))
    args, specs = [x], [tok]
    if mix is not None:
        o_sb, o_fx, w_out, gate_mix = mix
        args += [o_sb, o_fx, w_out, gate_mix]
        specs += [pl.BlockSpec((None, tm, SB_WIDTH), lambda b, i: (b, i, 0)),
                  pl.BlockSpec((None, tm, FOX_WIDTH), lambda b, i: (b, i, 0)),
                  _const_spec(w_out.shape), per_batch]
    args += [shift, scale, gate, g.reshape(1, d), w_up, w_down]
    specs += [per_batch, per_batch, per_batch, _const_spec((1, d)),
              _const_spec(w_up.shape), _const_spec(w_down.shape)]
    if g_final is not None:
        args.append(g_final.reshape(1, d))
        specs.append(_const_spec((1, d)))
    return pl.pallas_call(
        functools.partial(_ffn_kernel, pre_mix=mix is not None, final_norm=g_final is not None),
        grid=(bsz, s // tm),
        in_specs=specs,
        out_specs=tok,
        out_shape=jax.ShapeDtypeStruct((bsz, s, d), F32),
        scratch_shapes=[pltpu.VMEM((tm, d), BF16), pltpu.VMEM((tm, d), F32)],
        compiler_params=pltpu.CompilerParams(
            dimension_semantics=("arbitrary", "arbitrary"), vmem_limit_bytes=VMEM_LIMIT_BYTES),
        name="ffn_mix" if mix is not None else "ffn",
    )(*args)


def _lane_cumsum(y):
    n = y.shape[-1]
    lane = lax.broadcasted_iota(jnp.int32, y.shape, y.ndim - 1)
    shift = 1
    while shift < n:
        y = y + jnp.where(lane >= shift, pltpu.roll(y, shift, axis=y.ndim - 1), 0.0)
        shift *= 2
    return y


def _proj_kernel(x_ref, shift_ref, scale_ref, g_ref, w_ref, wf_ref, bf_ref, gq_ref, gk_ref,
                 grp_ref, sbq_ref, sbk_ref, sbv_ref, fxq_ref, fxk_ref, fxv_ref, fxg_ref,
                 fcum_ref, h_scr, carry_scr):
    @pl.when(pl.program_id(1) == 0)
    def _():
        carry_scr[...] = jnp.zeros_like(carry_scr)

    x = x_ref[...]
    h_scr[...] = _modulate(x, g_ref[...], shift_ref[...], scale_ref[...]).astype(BF16)
    h = h_scr[...]
    w = SB_WIDTH

    def proj(idx):
        return _dot(h, w_ref[:, idx * w:(idx + 1) * w].astype(BF16))

    def head_rms(p):
        ms = _dot((p * p).astype(BF16), grp_ref[...])
        return p * lax.rsqrt(ms + EPS)

    f = _dot(h, wf_ref[...])
    ft = jnp.transpose(f)[0:FOX_HEADS, :] + bf_ref[...]
    log_f = -_softplus(-ft)
    fcum = _lane_cumsum(log_f) + carry_scr[:, 0:1]
    fcum_ref[...] = fcum
    carry_scr[...] = jnp.broadcast_to(fcum[:, fcum.shape[1] - 1:], carry_scr.shape)

    fxq_ref[...] = (head_rms(proj(3)) * (gq_ref[...] * (QK_SCALE * LOG2E))).astype(BF16)
    fxk_ref[...] = (head_rms(proj(4)) * gk_ref[...]).astype(BF16)
    fxg_ref[...] = _sigmoid(proj(6)).astype(BF16)
    sbq_ref[...] = (proj(0) * (QK_SCALE * LOG2E)).astype(BF16)
    sbk_ref[...] = proj(1).astype(BF16)
    sbv_ref[...] = proj(2).astype(BF16)
    fxv_ref[...] = proj(5).astype(BF16)


def _in_proj(x, shift, scale, g, w_main, w_f, b_f, gq, gk, grp):
    bsz, s, d = x.shape
    tm = PROJ_TM
    tok = pl.BlockSpec((None, tm, d), lambda b, i: (b, i, 0))
    per_batch = pl.BlockSpec((None, 1, d), lambda b, i: (b, 0, 0))
    head_out = pl.BlockSpec((None, tm, SB_WIDTH), lambda b, i: (b, i, 0))
    head_shape = jax.ShapeDtypeStruct((bsz, s, SB_WIDTH), BF16)
    return pl.pallas_call(
        _proj_kernel,
        grid=(bsz, s // tm),
        in_specs=[tok, per_batch, per_batch, _const_spec((1, d)), _const_spec(w_main.shape),
                  _const_spec(w_f.shape), _const_spec(b_f.shape), _const_spec(gq.shape),
                  _const_spec(gk.shape), _const_spec(grp.shape)],
        out_specs=[head_out] * 7 + [pl.BlockSpec((None, FOX_HEADS, tm), lambda b, i: (b, 0, i))],
        out_shape=[head_shape] * 7 + [jax.ShapeDtypeStruct((bsz, FOX_HEADS, s), F32)],
        scratch_shapes=[pltpu.VMEM((tm, d), BF16), pltpu.VMEM((FOX_HEADS, LANES), F32)],
        compiler_params=pltpu.CompilerParams(
            dimension_semantics=("arbitrary", "arbitrary"), vmem_limit_bytes=VMEM_LIMIT_BYTES),
        name="in_proj",
    )(x, shift, scale, g.reshape(1, d), w_main, w_f, b_f, gq, gk, grp)


def _split_heads(x):
    lane = lax.broadcasted_iota(jnp.int32, x.shape, 1)
    zero = jnp.zeros_like(x)
    return jnp.where(lane < HEAD_DIM, x, zero), jnp.where(lane >= HEAD_DIM, x, zero)


def _pair_lanes(pp):
    return slice(pp * LANES, (pp + 1) * LANES)


def _fill_vexp(v_ref, lanes, vexp_ref, n_blocks, t, with_ones):
    def body(j, carry):
        v0, v1 = _split_heads(v_ref[pl.ds(pl.multiple_of(j * t, t), t), lanes])
        vexp_ref[j, 0:t, 0:LANES] = v0
        vexp_ref[j, t:2 * t, 0:LANES] = v1
        if with_ones:
            lane = lax.broadcasted_iota(jnp.int32, (t, LANES), 1)
            vexp_ref[j, 0:t, LANES:2 * LANES] = (lane < HEAD_DIM).astype(F32).astype(vexp_ref.dtype)
            vexp_ref[j, t:2 * t, LANES:2 * LANES] = (lane >= HEAD_DIM).astype(F32).astype(
                vexp_ref.dtype)
        return carry

    lax.fori_loop(0, n_blocks, body, 0)


def _per_head(lane, a0, a1):
    return jnp.where(lane < HEAD_DIM, a0, a1)


def _lane_tile(a, width):
    return jnp.concatenate([a] * (width // LANES), axis=1)


def _head_rmsnorm(o, lane):
    sq = o * o
    ss0 = jnp.sum(jnp.where(lane < HEAD_DIM, sq, 0.0), axis=1, keepdims=True)
    ss1 = jnp.sum(jnp.where(lane >= HEAD_DIM, sq, 0.0), axis=1, keepdims=True)
    inv = _per_head(lane, lax.rsqrt(ss0 * (1.0 / HEAD_DIM) + EPS),
                    lax.rsqrt(ss1 * (1.0 / HEAD_DIM) + EPS))
    return o * inv


def _task_table(tasks, n_q):
    idle = [(0, 0, n_q, 0)] * PIPE_LAG
    return np.asarray(idle + list(tasks) + idle, dtype=np.int32).T.copy()


def _fill_qexp(q_ref, lanes, qexp_ref, n_blocks, t):
    def body(j, carry):
        rows = pl.ds(pl.multiple_of(j * t, t), t)
        q0, q1 = _split_heads(q_ref[rows, lanes])
        qexp_ref[0, rows, :] = q0
        qexp_ref[1, rows, :] = q1
        return carry

    lax.fori_loop(0, n_blocks, body, 0)


def _score_stage(q_blk, k_blk, qexp_ref, k_ref, lanes, s_ref, t):
    rows = pl.ds(pl.multiple_of(q_blk * t, t), t)
    k = k_ref[pl.ds(pl.multiple_of(k_blk * t, t), t), lanes]
    for hd in range(HEADS_PER_TILE):
        s_ref[hd] = _dot_nt(qexp_ref[hd, rows, :], k)


def _causal_bias(t, strict):
    row = lax.broadcasted_iota(jnp.int32, (t, t), 0)
    col = lax.broadcasted_iota(jnp.int32, (t, t), 1)
    keep = col < row if strict else col <= row
    return jnp.where(keep, 0.0, MASKED).astype(F32)


def _sb_kernel(tab_ref, q_ref, k_ref, v_ref, g_ref, tri_ref, o_ref, vexp_scr, qexp_scr, bias_scr,
               acc_scr, carry_scr, z_scr, zc_scr, suf_scr, sp_scr, w_scr, l_scr):
    t = SB_T
    n_q = q_ref.shape[0] // t
    n_iter = tab_ref.shape[1] - PIPE_LAG
    pairs = range(ATT_PAIRS)

    for pp in pairs:
        _fill_vexp(v_ref, _pair_lanes(pp), vexp_scr.at[pp], n_q, t, with_ones=False)
        _fill_qexp(q_ref, _pair_lanes(pp), qexp_scr.at[pp], n_q, t)
    bias_scr[0] = jnp.zeros((t, t), F32)
    bias_scr[1] = _causal_bias(t, strict=True)
    acc_scr[...] = jnp.zeros_like(acc_scr)
    carry_scr[...] = jnp.zeros_like(carry_scr)
    z_scr[...] = jnp.zeros_like(z_scr)
    zc_scr[...] = jnp.zeros_like(zc_scr)
    suf_scr[...] = jnp.zeros_like(suf_scr)
    l_scr[...] = jnp.zeros_like(l_scr)

    def score_stage(pp, q_blk, k_blk, mask_idx):
        rows = pl.ds(pl.multiple_of(q_blk * t, t), t)
        k = k_ref[pl.ds(pl.multiple_of(k_blk * t, t), t), _pair_lanes(pp)]
        bias = bias_scr[mask_idx]
        for hd in range(HEADS_PER_TILE):
            z = _dot_nt(qexp_scr[pp, hd, rows, :], k) + bias
            z_scr[pp, hd] = z
            l_scr[pp, hd] = jnp.log2(1.0 + jnp.exp2(-jnp.abs(z)))

    def keep_stage(pp, slot):
        for hd in range(HEADS_PER_TILE):
            for r in range(0, t, ROW_CHUNK):
                rows = slice(r, r + ROW_CHUNK)
                z = z_scr[pp, hd, rows, :]
                sp = jnp.maximum(z, 0.0) + l_scr[pp, hd, rows, :]
                sp_scr[pp, hd, rows, :] = sp.astype(BF16)
                zc_scr[pp, hd, rows, :] = z - _lane_tile(carry_scr[pp, slot, hd, rows, :], t)
            suf = _dot(sp_scr[pp, hd], tri_ref[...])
            suf_scr[pp, hd] = suf
            carry_scr[pp, slot, hd] = carry_scr[pp, slot, hd] + suf[:, 0:1]

    def value_stage(pp, k_blk, slot):
        for hd in range(HEADS_PER_TILE):
            for r in range(0, t, ROW_CHUNK):
                rows = slice(r, r + ROW_CHUNK)
                w_scr[pp, rows, hd * t:(hd + 1) * t] = jnp.exp2(
                    zc_scr[pp, hd, rows, :] - suf_scr[pp, hd, rows, :]).astype(BF16)
        acc_scr[pp, slot] += _dot(w_scr[pp], vexp_scr[pp, k_blk])

    def body(i, carry):
        for pp in pairs:
            value_stage(pp, tab_ref[1, i], tab_ref[2, i])
        for pp in pairs:
            keep_stage(pp, tab_ref[2, i + 1])
        for pp in pairs:
            score_stage(pp, tab_ref[0, i + 2], tab_ref[1, i + 2], tab_ref[3, i + 2])
        return carry

    lax.fori_loop(0, n_iter, body, 0)

    def tail(pp, q_blk):
        def live():
            return (jnp.min(carry_scr[pp, q_blk]) <= -F32_EXP2_ZERO_BELOW).astype(jnp.int32)

        def cond(state):
            k_blk, go = state
            return jnp.logical_and(k_blk >= 0, go > 0)

        def step(state):
            k_blk, _ = state
            score_stage(pp, q_blk, k_blk, 0)
            keep_stage(pp, q_blk)
            value_stage(pp, k_blk, q_blk)
            return k_blk - 1, live()

        lax.while_loop(cond, step, (q_blk - 2, live()))

    lane = lax.broadcasted_iota(jnp.int32, (t, LANES), 1)
    for pp in pairs:
        def tail_body(q_blk, carry, pp=pp):
            tail(pp, q_blk)
            return carry

        @pl.when(jnp.min(carry_scr[pp, 2:n_q]) <= -F32_EXP2_ZERO_BELOW)
        def _():
            lax.fori_loop(2, n_q, tail_body, 0)

        lanes = _pair_lanes(pp)
        for q_blk in range(n_q):
            o = _head_rmsnorm(acc_scr[pp, q_blk], lane) * g_ref[:, lanes]
            o_ref[q_blk * t:(q_blk + 1) * t, lanes] = o.astype(o_ref.dtype)


def _sb_attention(q, k, v, g, tri):
    bsz, s, width = q.shape
    t = SB_T
    n_q = s // t
    n_tiles = width // LANES
    tasks = []
    for qb in range(n_q):
        tasks.append((qb, qb, qb, 1))
        if qb >= 1:
            tasks.append((qb, qb - 1, qb, 0))
    table = _task_table(tasks, n_q)
    pp, hh = ATT_PAIRS, HEADS_PER_TILE
    seq_spec = pl.BlockSpec((None, s, pp * LANES), lambda b, hp: (b, 0, hp))
    return pl.pallas_call(
        _sb_kernel,
        grid=(bsz, n_tiles // pp),
        in_specs=[pl.BlockSpec(memory_space=pltpu.SMEM), seq_spec, seq_spec, seq_spec,
                  pl.BlockSpec((1, pp * LANES), lambda b, hp: (0, hp)), _const_spec(tri.shape)],
        out_specs=seq_spec,
        out_shape=jax.ShapeDtypeStruct((bsz, s, width), BF16),
        scratch_shapes=[pltpu.VMEM((pp, n_q, 2 * t, LANES), BF16),
                        pltpu.VMEM((pp, hh, s, LANES), BF16),
                        pltpu.VMEM((2, t, t), F32),
                        pltpu.VMEM((pp, n_q + 1, t, LANES), F32),
                        pltpu.VMEM((pp, n_q + 1, hh, t, LANES), F32),
                        pltpu.VMEM((pp, hh, t, t), F32),
                        pltpu.VMEM((pp, hh, t, t), F32),
                        pltpu.VMEM((pp, hh, t, t), F32),
                        pltpu.VMEM((pp, hh, t, t), BF16),
                        pltpu.VMEM((pp, t, hh * t), BF16),
                        pltpu.VMEM((pp, hh, t, t), F32)],
        compiler_params=pltpu.CompilerParams(
            dimension_semantics=("arbitrary", "arbitrary"), vmem_limit_bytes=VMEM_LIMIT_BYTES),
        name="sb_attn",
    )(jnp.asarray(table), q, k, v, g.reshape(1, width), tri)


def _fox_kernel(tab_ref, q_ref, k_ref, v_ref, f_ref, gate_ref, g_ref, o_ref, vexp_scr, qexp_scr,
                bias_scr, acc_scr, m_scr, s_scr, p_scr, alpha_scr, bmax_scr):
    t = FOX_T
    n_q = q_ref.shape[0] // t
    n_iter = tab_ref.shape[1] - PIPE_LAG
    pairs = range(ATT_PAIRS)
    lane = lax.broadcasted_iota(jnp.int32, (t, LANES), 1)

    for pp in pairs:
        _fill_vexp(v_ref, _pair_lanes(pp), vexp_scr.at[pp], n_q, t, with_ones=True)
        _fill_qexp(q_ref, _pair_lanes(pp), qexp_scr.at[pp], n_q, t)
    bias_scr[0] = jnp.zeros((t, t), F32)
    bias_scr[1] = _causal_bias(t, strict=False)
    acc_scr[...] = jnp.zeros_like(acc_scr)
    m_scr[...] = jnp.full_like(m_scr, -jnp.inf)
    s_scr[...] = jnp.zeros_like(s_scr)
    bmax_scr[...] = jnp.zeros_like(bmax_scr)
    p_scr[...] = jnp.zeros_like(p_scr)
    alpha_scr[...] = jnp.zeros_like(alpha_scr)

    def score_stage(pp, q_blk, k_blk, mask_idx):
        rows = pl.ds(pl.multiple_of(q_blk * t, t), t)
        cols = pl.ds(pl.multiple_of(k_blk * t, t), t)
        k = k_ref[cols, _pair_lanes(pp)]
        f_keys = f_ref[pp, :, cols] * LOG2E
        bias = bias_scr[mask_idx]
        for hd in range(HEADS_PER_TILE):
            sc = _dot_nt(qexp_scr[pp, hd, rows, :], k) - f_keys[hd:hd + 1, :] + bias
            s_scr[pp, hd] = sc
            bmax_scr[pp, hd] = jnp.broadcast_to(jnp.max(sc, axis=1, keepdims=True), (t, LANES))

    def softmax_stage(pp, slot):
        for hd in range(HEADS_PER_TILE):
            m_old = m_scr[pp, slot, hd]
            m_new = jnp.maximum(m_old, bmax_scr[pp, hd])
            alpha_scr[pp, hd] = jnp.exp2(m_old - m_new)
            m_scr[pp, slot, hd] = m_new
            p_scr[pp, :, hd * t:(hd + 1) * t] = jnp.exp2(
                s_scr[pp, hd] - _lane_tile(m_new, t)).astype(BF16)

    def value_stage(pp, k_blk, slot):
        pv = _dot(p_scr[pp], vexp_scr[pp, k_blk])
        alpha = _per_head(lane, alpha_scr[pp, 0], alpha_scr[pp, 1])
        acc_scr[pp, slot] = acc_scr[pp, slot] * _lane_tile(alpha, 2 * LANES) + pv

    def body(i, carry):
        for pp in pairs:
            value_stage(pp, tab_ref[1, i], tab_ref[2, i])
        for pp in pairs:
            softmax_stage(pp, tab_ref[2, i + 1])
        for pp in pairs:
            score_stage(pp, tab_ref[0, i + 2], tab_ref[1, i + 2], tab_ref[3, i + 2])
        return carry

    lax.fori_loop(0, n_iter, body, 0)

    for pp in pairs:
        lanes = _pair_lanes(pp)
        for q_blk in range(n_q):
            rows = slice(q_blk * t, (q_blk + 1) * t)
            acc = acc_scr[pp, q_blk]
            o = acc[:, 0:LANES] / acc[:, LANES:2 * LANES]
            o = _head_rmsnorm(o, lane) * g_ref[:, lanes] * gate_ref[rows, lanes].astype(F32)
            o_ref[rows, lanes] = o.astype(o_ref.dtype)


def _fox_attention(q, k, v, fcum, gate, g):
    bsz, s, width = q.shape
    t = FOX_T
    n_q = s // t
    n_tiles = width // LANES
    tasks = [(qb, kb, qb, int(kb == qb)) for qb in range(n_q) for kb in range(qb + 1)]
    table = _task_table(tasks, n_q)
    pp, hh = ATT_PAIRS, HEADS_PER_TILE
    seq_spec = pl.BlockSpec((None, s, pp * LANES), lambda b, hp: (b, 0, hp))
    f_tiles = fcum.reshape(bsz, n_tiles, hh, s)
    return pl.pallas_call(
        _fox_kernel,
        grid=(bsz, n_tiles // pp),
        in_specs=[pl.BlockSpec(memory_space=pltpu.SMEM), seq_spec, seq_spec, seq_spec,
                  pl.BlockSpec((None, pp, hh, s), lambda b, hp: (b, hp, 0, 0)),
                  seq_spec, pl.BlockSpec((1, pp * LANES), lambda b, hp: (0, hp))],
        out_specs=seq_spec,
        out_shape=jax.ShapeDtypeStruct((bsz, s, width), BF16),
        scratch_shapes=[pltpu.VMEM((pp, n_q, 2 * t, 2 * LANES), BF16),
                        pltpu.VMEM((pp, hh, s, LANES), BF16),
                        pltpu.VMEM((2, t, t), F32),
                        pltpu.VMEM((pp, n_q + 1, t, 2 * LANES), F32),
                        pltpu.VMEM((pp, n_q + 1, hh, t, LANES), F32),
                        pltpu.VMEM((pp, hh, t, t), F32),
                        pltpu.VMEM((pp, t, hh * t), BF16),
                        pltpu.VMEM((pp, hh, t, LANES), F32),
                        pltpu.VMEM((pp, hh, t, LANES), F32)],
        compiler_params=pltpu.CompilerParams(
            dimension_semantics=("arbitrary", "arbitrary"), vmem_limit_bytes=VMEM_LIMIT_BYTES),
        name="fox_attn",
    )(jnp.asarray(table), q, k, v, f_tiles, gate, g.reshape(1, width))


def kernel(x, c, w_ada, b_ada, g_ffn1, w_ffn1_up, w_ffn1_down, g_mix, w_in, b_forget, g_fox_q,
           g_fox_k, g_sb_out, g_fox_out, w_out, g_ffn2, w_ffn2_up, w_ffn2_down, g_final):
    bsz, s, d = x.shape
    depth = w_ada.shape[0]
    t = SB_T

    head_of = jnp.arange(FOX_WIDTH) // HEAD_DIM
    grp = jnp.where(head_of[:, None] == head_of[None, :], 1.0 / HEAD_DIM, 0.0).astype(BF16)
    key = jnp.arange(t)
    tri = (key[:, None] >= key[None, :]).astype(BF16)

    n_main = 3 * SB_WIDTH + 4 * FOX_WIDTH
    for l in range(depth):
        ada = _ada(c, w_ada[l], b_ada[l]).reshape(bsz, N_SUBLAYERS, 3, 1, d)
        shift, scale, gate = ada[:, :, 0], ada[:, :, 1], ada[:, :, 2]

        x = _ffn(x, shift[:, 0], scale[:, 0], gate[:, 0], g_ffn1[l],
                 w_ffn1_up[l], w_ffn1_down[l])

        w_f = jnp.pad(w_in[l][:, n_main:], ((0, 0), (0, LANES - FOX_HEADS))).astype(BF16)
        sbq, sbk, sbv, fxq, fxk, fxv, fxg, fcum = _in_proj(
            x, shift[:, 1], scale[:, 1], g_mix[l], w_in[l], w_f,
            b_forget[l].reshape(FOX_HEADS, 1),
            jnp.tile(g_fox_q[l], FOX_HEADS).reshape(1, FOX_WIDTH),
            jnp.tile(g_fox_k[l], FOX_HEADS).reshape(1, FOX_WIDTH), grp)

        o_sb = _sb_attention(sbq, sbk, sbv, g_sb_out[l], tri)
        o_fx = _fox_attention(fxq, fxk, fxv, fcum, fxg, g_fox_out[l])

        x = _ffn(x, shift[:, 2], scale[:, 2], gate[:, 2], g_ffn2[l],
                 w_ffn2_up[l], w_ffn2_down[l],
                 mix=(o_sb, o_fx, w_out[l], gate[:, 1]),
                 g_final=g_final if l == depth - 1 else None)
    return x
```

```python
import functools

import numpy as np

import jax
import jax.numpy as jnp
from jax import lax
from jax.experimental import pallas as pl
from jax.experimental.pallas import tpu as pltpu

D_MODEL = 1024
HEAD_DIM = 64
SB_HEADS = 8
FOX_HEADS = 8
SB_WIDTH = SB_HEADS * HEAD_DIM
FOX_WIDTH = FOX_HEADS * HEAD_DIM
FFN_HIDDEN = 2816
N_SUBLAYERS = 3
EPS = 1e-6
QK_SCALE = HEAD_DIM ** -0.5

LANES = 128
HEADS_PER_TILE = LANES // HEAD_DIM
VMEM_LIMIT_BYTES = 56 * 1024 * 1024

ADA_TN = 1152
FFN_TM = 512
FFN_HC = 256
PROJ_TM = 512
SB_T = 256
FOX_T = 512
SB_PAIRS = 2
FOX_PAIRS = 2
PIPE_LAG = 2
ROW_CHUNK = 32

LOG2E = 1.4426950408889634
F32_EXP2_ZERO_BELOW = -150.0
MASKED = -1e30

BF16 = jnp.bfloat16
F32 = jnp.float32


def _dot(a, b):
    return jnp.dot(a, b, preferred_element_type=F32)


def _dot_nt(a, b):
    return lax.dot_general(a, b, (((1,), (1,)), ((), ())), preferred_element_type=F32)


def _sigmoid(x):
    return 1.0 / (1.0 + jnp.exp(-x))


def _softplus(x):
    return jnp.maximum(x, 0.0) + jnp.log(1.0 + jnp.exp(-jnp.abs(x)))


def _modulate(x, g, shift, scale):
    ms = jnp.mean(x * x, axis=-1, keepdims=True)
    return (x * lax.rsqrt(ms + EPS)) * (g * (1.0 + scale)) + shift


def _const_spec(shape):
    return pl.BlockSpec(shape, lambda *_: (0,) * len(shape), pipeline_mode=pl.Buffered(1))


def _ada_kernel(c_ref, w_ref, b_ref, o_ref):
    c = c_ref[...]
    cond = (c * _sigmoid(c)).astype(BF16)
    o_ref[...] = _dot(cond, w_ref[...].astype(BF16)) + b_ref[...]


def _ada(c, w, b):
    bsz, d = c.shape
    n = w.shape[1]
    return pl.pallas_call(
        _ada_kernel,
        grid=(n // ADA_TN,),
        in_specs=[
            pl.BlockSpec((bsz, d), lambda j: (0, 0)),
            pl.BlockSpec((d, ADA_TN), lambda j: (0, j)),
            pl.BlockSpec((1, ADA_TN), lambda j: (0, j)),
        ],
        out_specs=pl.BlockSpec((bsz, ADA_TN), lambda j: (0, j)),
        out_shape=jax.ShapeDtypeStruct((bsz, n), F32),
        compiler_params=pltpu.CompilerParams(
            dimension_semantics=("arbitrary",), vmem_limit_bytes=VMEM_LIMIT_BYTES),
        name="ada",
    )(c, w, b.reshape(1, n))


def _ffn_kernel(*refs, pre_mix, final_norm):
    it = iter(refs)
    x_ref = next(it)
    if pre_mix:
        osb_ref, ofx_ref, wout_ref, gate_mix_ref = next(it), next(it), next(it), next(it)
    shift_ref, scale_ref, gate_ref, g_ref = next(it), next(it), next(it), next(it)
    wup_ref, wdown_ref = next(it), next(it)
    if final_norm:
        gf_ref = next(it)
    o_ref, h_scr, acc_scr = next(it), next(it), next(it)

    x = x_ref[...]
    if pre_mix:
        mix = (_dot(osb_ref[...], wout_ref[0:SB_WIDTH, :].astype(BF16))
               + _dot(ofx_ref[...], wout_ref[SB_WIDTH:SB_WIDTH + FOX_WIDTH, :].astype(BF16)))
        x = x + gate_mix_ref[...] * mix
    o_ref[...] = x
    h_scr[...] = _modulate(x, g_ref[...], shift_ref[...], scale_ref[...]).astype(BF16)
    acc_scr[...] = jnp.zeros_like(acc_scr)

    hidden = wdown_ref.shape[0]
    for lo in range(0, hidden, FFN_HC):
        h = h_scr[...]
        gt = _dot(h, wup_ref[:, lo:lo + FFN_HC].astype(BF16))
        up = _dot(h, wup_ref[:, hidden + lo:hidden + lo + FFN_HC].astype(BF16))
        a = (gt * _sigmoid(gt) * up).astype(BF16)
        acc_scr[...] += _dot(a, wdown_ref[lo:lo + FFN_HC, :].astype(BF16))
    y = o_ref[...] + (0.5 * gate_ref[...]) * acc_scr[...]
    if final_norm:
        ms = jnp.mean(y * y, axis=-1, keepdims=True)
        y = (y * lax.rsqrt(ms + EPS)) * gf_ref[...]
    o_ref[...] = y


def _ffn(x, shift, scale, gate, g, w_up, w_down, mix=None, g_final=None):
    bsz, s, d = x.shape
    assert w_down.shape[0] % FFN_HC == 0
    tm = FFN_TM
    tok = pl.BlockSpec((None, tm, d), lambda b, i: (b, i, 0))
    per_batch = pl.BlockSpec((None, 1, d), lambda b, i: (b, 0, 0))
    args, specs = [x], [tok]
    if mix is not None:
        o_sb, o_fx, w_out, gate_mix = mix
        args += [o_sb, o_fx, w_out, gate_mix]
        specs += [pl.BlockSpec((None, tm, SB_WIDTH), lambda b, i: (b, i, 0)),
                  pl.BlockSpec((None, tm, FOX_WIDTH), lambda b, i: (b, i, 0)),
                  _const_spec(w_out.shape), per_batch]
    args += [shift, scale, gate, g.reshape(1, d), w_up, w_down]
    specs += [per_batch, per_batch, per_batch, _const_spec((1, d)),
              _const_spec(w_up.shape), _const_spec(w_down.shape)]
    if g_final is not None:
        args.append(g_final.reshape(1, d))
        specs.append(_const_spec((1, d)))
    return pl.pallas_call(
        functools.partial(_ffn_kernel, pre_mix=mix is not None, final_norm=g_final is not None),
        grid=(bsz, s // tm),
        in_specs=specs,
        out_specs=tok,
        out_shape=jax.ShapeDtypeStruct((bsz, s, d), F32),
        scratch_shapes=[pltpu.VMEM((tm, d), BF16), pltpu.VMEM((tm, d), F32)],
        compiler_params=pltpu.CompilerParams(
            dimension_semantics=("arbitrary", "arbitrary"), vmem_limit_bytes=VMEM_LIMIT_BYTES),
        name="ffn_mix" if mix is not None else "ffn",
    )(*args)


def _lane_cumsum(y):
    n = y.shape[-1]
    lane = lax.broadcasted_iota(jnp.int32, y.shape, y.ndim - 1)
    shift = 1
    while shift < n:
        y = y + jnp.where(lane >= shift, pltpu.roll(y, shift, axis=y.ndim - 1), 0.0)
        shift *= 2
    return y


def _proj_kernel(x_ref, shift_ref, scale_ref, g_ref, w_ref, wf_ref, bf_ref, gq_ref, gk_ref,
                 grp_ref, sbq_ref, sbk_ref, sbv_ref, fxq_ref, fxk_ref, fxv_ref, fxg_ref,
                 fcum_ref, h_scr, carry_scr):
    @pl.when(pl.program_id(1) == 0)
    def _():
        carry_scr[...] = jnp.zeros_like(carry_scr)

    x = x_ref[...]
    h_scr[...] = _modulate(x, g_ref[...], shift_ref[...], scale_ref[...]).astype(BF16)
    h = h_scr[...]
    w = SB_WIDTH

    def proj(idx):
        return _dot(h, w_ref[:, idx * w:(idx + 1) * w].astype(BF16))

    def head_rms(p):
        ms = _dot((p * p).astype(BF16), grp_ref[...])
        return p * lax.rsqrt(ms + EPS)

    f = _dot(h, wf_ref[...])
    ft = jnp.transpose(f)[0:FOX_HEADS, :] + bf_ref[...]
    log_f = -_softplus(-ft)
    fcum = _lane_cumsum(log_f) + carry_scr[:, 0:1]
    fcum_ref[...] = fcum
    carry_scr[...] = jnp.broadcast_to(fcum[:, fcum.shape[1] - 1:], carry_scr.shape)

    fxq_ref[...] = (head_rms(proj(3)) * (gq_ref[...] * (QK_SCALE * LOG2E))).astype(BF16)
    fxk_ref[...] = (head_rms(proj(4)) * gk_ref[...]).astype(BF16)
    fxg_ref[...] = _sigmoid(proj(6)).astype(BF16)
    sbq_ref[...] = (proj(0) * (QK_SCALE * LOG2E)).astype(BF16)
    sbk_ref[...] = proj(1).astype(BF16)
    sbv_ref[...] = proj(2).astype(BF16)
    fxv_ref[...] = proj(5).astype(BF16)


def _in_proj(x, shift, scale, g, w_main, w_f, b_f, gq, gk, grp):
    bsz, s, d = x.shape
    tm = PROJ_TM
    tok = pl.BlockSpec((None, tm, d), lambda b, i: (b, i, 0))
    per_batch = pl.BlockSpec((None, 1, d), lambda b, i: (b, 0, 0))
    head_out = pl.BlockSpec((None, tm, SB_WIDTH), lambda b, i: (b, i, 0))
    head_shape = jax.ShapeDtypeStruct((bsz, s, SB_WIDTH), BF16)
    return pl.pallas_call(
        _proj_kernel,
        grid=(bsz, s // tm),
        in_specs=[tok, per_batch, per_batch, _const_spec((1, d)), _const_spec(w_main.shape),
                  _const_spec(w_f.shape), _const_spec(b_f.shape), _const_spec(gq.shape),
                  _const_spec(gk.shape), _const_spec(grp.shape)],
        out_specs=[head_out] * 7 + [pl.BlockSpec((None, FOX_HEADS, tm), lambda b, i: (b, 0, i))],
        out_shape=[head_shape] * 7 + [jax.ShapeDtypeStruct((bsz, FOX_HEADS, s), F32)],
        scratch_shapes=[pltpu.VMEM((tm, d), BF16), pltpu.VMEM((FOX_HEADS, LANES), F32)],
        compiler_params=pltpu.CompilerParams(
            dimension_semantics=("arbitrary", "arbitrary"), vmem_limit_bytes=VMEM_LIMIT_BYTES),
        name="in_proj",
    )(x, shift, scale, g.reshape(1, d), w_main, w_f, b_f, gq, gk, grp)


def _split_heads(x):
    lane = lax.broadcasted_iota(jnp.int32, x.shape, 1)
    zero = jnp.zeros_like(x)
    return jnp.where(lane < HEAD_DIM, x, zero), jnp.where(lane >= HEAD_DIM, x, zero)


def _pair_lanes(pp):
    return slice(pp * LANES, (pp + 1) * LANES)


def _fill_vexp(v_ref, lanes, vexp_ref, n_blocks, t, with_ones):
    def body(j, carry):
        v0, v1 = _split_heads(v_ref[pl.ds(pl.multiple_of(j * t, t), t), lanes])
        vexp_ref[j, 0:t, 0:LANES] = v0
        vexp_ref[j, t:2 * t, 0:LANES] = v1
        if with_ones:
            lane = lax.broadcasted_iota(jnp.int32, (t, LANES), 1)
            vexp_ref[j, 0:t, LANES:2 * LANES] = (lane < HEAD_DIM).astype(F32).astype(vexp_ref.dtype)
            vexp_ref[j, t:2 * t, LANES:2 * LANES] = (lane >= HEAD_DIM).astype(F32).astype(
                vexp_ref.dtype)
        return carry

    lax.fori_loop(0, n_blocks, body, 0)


def _per_head(lane, a0, a1):
    return jnp.where(lane < HEAD_DIM, a0, a1)


def _lane_tile(a, width):
    return jnp.concatenate([a] * (width // LANES), axis=1)


def _head_rmsnorm(o, lane):
    sq = o * o
    ss0 = jnp.sum(jnp.where(lane < HEAD_DIM, sq, 0.0), axis=1, keepdims=True)
    ss1 = jnp.sum(jnp.where(lane >= HEAD_DIM, sq, 0.0), axis=1, keepdims=True)
    inv = _per_head(lane, lax.rsqrt(ss0 * (1.0 / HEAD_DIM) + EPS),
                    lax.rsqrt(ss1 * (1.0 / HEAD_DIM) + EPS))
    return o * inv


def _task_table(tasks, n_q):
    idle = [(0, 0, n_q, 0)] * PIPE_LAG
    return np.asarray(idle + list(tasks) + idle, dtype=np.int32).T.copy()


def _fill_qexp(q_ref, lanes, qexp_ref, n_blocks, t):
    def body(j, carry):
        rows = pl.ds(pl.multiple_of(j * t, t), t)
        q0, q1 = _split_heads(q_ref[rows, lanes])
        qexp_ref[0, rows, :] = q0
        qexp_ref[1, rows, :] = q1
        return carry

    lax.fori_loop(0, n_blocks, body, 0)


def _score_stage(q_blk, k_blk, qexp_ref, k_ref, lanes, s_ref, t):
    rows = pl.ds(pl.multiple_of(q_blk * t, t), t)
    k = k_ref[pl.ds(pl.multiple_of(k_blk * t, t), t), lanes]
    for hd in range(HEADS_PER_TILE):
        s_ref[hd] = _dot_nt(qexp_ref[hd, rows, :], k)


def _causal_bias(t, strict):
    row = lax.broadcasted_iota(jnp.int32, (t, t), 0)
    col = lax.broadcasted_iota(jnp.int32, (t, t), 1)
    keep = col < row if strict else col <= row
    return jnp.where(keep, 0.0, MASKED).astype(F32)


def _sb_kernel(tab_ref, q_ref, k_ref, v_ref, g_ref, tri_ref, o_ref, vexp_scr, qexp_scr, bias_scr,
               acc_scr, carry_scr, z_scr, zc_scr, suf_scr, sp_scr, w_scr, l_scr):
    t = SB_T
    n_q = q_ref.shape[0] // t
    n_iter = tab_ref.shape[1] - PIPE_LAG
    pairs = range(q_ref.shape[1] // LANES)

    for pp in pairs:
        _fill_vexp(v_ref, _pair_lanes(pp), vexp_scr.at[pp], n_q, t, with_ones=False)
        _fill_qexp(q_ref, _pair_lanes(pp), qexp_scr.at[pp], n_q, t)
    bias_scr[0] = jnp.zeros((t, t), F32)
    bias_scr[1] = _causal_bias(t, strict=True)
    acc_scr[...] = jnp.zeros_like(acc_scr)
    carry_scr[...] = jnp.zeros_like(carry_scr)
    z_scr[...] = jnp.zeros_like(z_scr)
    zc_scr[...] = jnp.zeros_like(zc_scr)
    suf_scr[...] = jnp.zeros_like(suf_scr)
    l_scr[...] = jnp.zeros_like(l_scr)

    def score_stage(pp, q_blk, k_blk, mask_idx):
        rows = pl.ds(pl.multiple_of(q_blk * t, t), t)
        k = k_ref[pl.ds(pl.multiple_of(k_blk * t, t), t), _pair_lanes(pp)]
        bias = bias_scr[mask_idx]
        for hd in range(HEADS_PER_TILE):
            z = _dot_nt(qexp_scr[pp, hd, rows, :], k) + bias
            z_scr[pp, hd] = z
            l_scr[pp, hd] = jnp.log2(1.0 + jnp.exp2(-jnp.abs(z)))

    def keep_stage(pp, slot):
        for hd in range(HEADS_PER_TILE):
            for r in range(0, t, ROW_CHUNK):
                rows = slice(r, r + ROW_CHUNK)
                z = z_scr[pp, hd, rows, :]
                sp = jnp.maximum(z, 0.0) + l_scr[pp, hd, rows, :]
                sp_scr[pp, hd, rows, :] = sp.astype(BF16)
                zc_scr[pp, hd, rows, :] = z - _lane_tile(carry_scr[pp, slot, hd, rows, :], t)
            suf = _dot(sp_scr[pp, hd], tri_ref[...])
            suf_scr[pp, hd] = suf
            carry_scr[pp, slot, hd] = carry_scr[pp, slot, hd] + suf[:, 0:1]

    def value_stage(pp, k_blk, slot):
        for hd in range(HEADS_PER_TILE):
            for r in range(0, t, ROW_CHUNK):
                rows = slice(r, r + ROW_CHUNK)
                w_scr[pp, rows, hd * t:(hd + 1) * t] = jnp.exp2(
                    zc_scr[pp, hd, rows, :] - suf_scr[pp, hd, rows, :]).astype(BF16)
        acc_scr[pp, slot] += _dot(w_scr[pp], vexp_scr[pp, k_blk])

    def body(i, carry):
        for pp in pairs:
            value_stage(pp, tab_ref[1, i], tab_ref[2, i])
        for pp in pairs:
            keep_stage(pp, tab_ref[2, i + 1])
        for pp in pairs:
            score_stage(pp, tab_ref[0, i + 2], tab_ref[1, i + 2], tab_ref[3, i + 2])
        return carry

    lax.fori_loop(0, n_iter, body, 0)

    def tail(pp, q_blk):
        def live():
            return (jnp.min(carry_scr[pp, q_blk]) <= -F32_EXP2_ZERO_BELOW).astype(jnp.int32)

        def cond(state):
            k_blk, go = state
            return jnp.logical_and(k_blk >= 0, go > 0)

        def step(state):
            k_blk, _ = state
            score_stage(pp, q_blk, k_blk, 0)
            keep_stage(pp, q_blk)
            value_stage(pp, k_blk, q_blk)
            return k_blk - 1, live()

        lax.while_loop(cond, step, (q_blk - 2, live()))

    lane = lax.broadcasted_iota(jnp.int32, (t, LANES), 1)
    for pp in pairs:
        def tail_body(q_blk, carry, pp=pp):
            tail(pp, q_blk)
            return carry

        @pl.when(jnp.min(carry_scr[pp, 2:n_q]) <= -F32_EXP2_ZERO_BELOW)
        def _():
            lax.fori_loop(2, n_q, tail_body, 0)

        lanes = _pair_lanes(pp)
        for q_blk in range(n_q):
            o = _head_rmsnorm(acc_scr[pp, q_blk], lane) * g_ref[:, lanes]
            o_ref[q_blk * t:(q_blk + 1) * t, lanes] = o.astype(o_ref.dtype)


def _sb_attention(q, k, v, g, tri):
    bsz, s, width = q.shape
    t = SB_T
    n_q = s // t
    n_tiles = width // LANES
    tasks = []
    for qb in range(n_q):
        tasks.append((qb, qb, qb, 1))
        if qb >= 1:
            tasks.append((qb, qb - 1, qb, 0))
    table = _task_table(tasks, n_q)
    pp, hh = SB_PAIRS, HEADS_PER_TILE
    seq_spec = pl.BlockSpec((None, s, pp * LANES), lambda b, hp: (b, 0, hp))
    return pl.pallas_call(
        _sb_kernel,
        grid=(bsz, n_tiles // pp),
        in_specs=[pl.BlockSpec(memory_space=pltpu.SMEM), seq_spec, seq_spec, seq_spec,
                  pl.BlockSpec((1, pp * LANES), lambda b, hp: (0, hp)), _const_spec(tri.shape)],
        out_specs=seq_spec,
        out_shape=jax.ShapeDtypeStruct((bsz, s, width), BF16),
        scratch_shapes=[pltpu.VMEM((pp, n_q, 2 * t, LANES), BF16),
                        pltpu.VMEM((pp, hh, s, LANES), BF16),
                        pltpu.VMEM((2, t, t), F32),
                        pltpu.VMEM((pp, n_q + 1, t, LANES), F32),
                        pltpu.VMEM((pp, n_q + 1, hh, t, LANES), F32),
                        pltpu.VMEM((pp, hh, t, t), F32),
                        pltpu.VMEM((pp, hh, t, t), F32),
                        pltpu.VMEM((pp, hh, t, t), F32),
                        pltpu.VMEM((pp, hh, t, t), BF16),
                        pltpu.VMEM((pp, t, hh * t), BF16),
                        pltpu.VMEM((pp, hh, t, t), F32)],
        compiler_params=pltpu.CompilerParams(
            dimension_semantics=("arbitrary", "arbitrary"), vmem_limit_bytes=VMEM_LIMIT_BYTES),
        name="sb_attn",
    )(jnp.asarray(table), q, k, v, g.reshape(1, width), tri)


def _fox_kernel(tab_ref, q_ref, k_ref, v_ref, f_ref, gate_ref, g_ref, o_ref, vexp_scr, qexp_scr,
                bias_scr, acc_scr, m_scr, s_scr, p_scr, alpha_scr, bmax_scr):
    t = FOX_T
    n_q = q_ref.shape[0] // t
    n_iter = tab_ref.shape[1] - PIPE_LAG
    pairs = range(q_ref.shape[1] // LANES)
    lane = lax.broadcasted_iota(jnp.int32, (t, LANES), 1)

    for pp in pairs:
        _fill_vexp(v_ref, _pair_lanes(pp), vexp_scr.at[pp], n_q, t, with_ones=True)
        _fill_qexp(q_ref, _pair_lanes(pp), qexp_scr.at[pp], n_q, t)
    bias_scr[0] = jnp.zeros((t, t), F32)
    bias_scr[1] = _causal_bias(t, strict=False)
    acc_scr[...] = jnp.zeros_like(acc_scr)
    m_scr[...] = jnp.full_like(m_scr, -jnp.inf)
    s_scr[...] = jnp.zeros_like(s_scr)
    bmax_scr[...] = jnp.zeros_like(bmax_scr)
    p_scr[...] = jnp.zeros_like(p_scr)
    alpha_scr[...] = jnp.zeros_like(alpha_scr)

    def score_stage(pp, q_blk, k_blk, mask_idx):
        rows = pl.ds(pl.multiple_of(q_blk * t, t), t)
        cols = pl.ds(pl.multiple_of(k_blk * t, t), t)
        k = k_ref[cols, _pair_lanes(pp)]
        f_keys = f_ref[pp, :, cols] * LOG2E
        bias = bias_scr[mask_idx]
        for hd in range(HEADS_PER_TILE):
            sc = _dot_nt(qexp_scr[pp, hd, rows, :], k) - f_keys[hd:hd + 1, :] + bias
            s_scr[pp, hd] = sc
            bmax_scr[pp, hd] = jnp.broadcast_to(jnp.max(sc, axis=1, keepdims=True), (t, LANES))

    def softmax_stage(pp, slot):
        for hd in range(HEADS_PER_TILE):
            m_old = m_scr[pp, slot, hd]
            m_new = jnp.maximum(m_old, bmax_scr[pp, hd])
            alpha_scr[pp, hd] = jnp.exp2(m_old - m_new)
            m_scr[pp, slot, hd] = m_new
            p_scr[pp, :, hd * t:(hd + 1) * t] = jnp.exp2(
                s_scr[pp, hd] - _lane_tile(m_new, t)).astype(BF16)

    def value_stage(pp, k_blk, slot):
        pv = _dot(p_scr[pp], vexp_scr[pp, k_blk])
        alpha = _per_head(lane, alpha_scr[pp, 0], alpha_scr[pp, 1])
        acc_scr[pp, slot] = acc_scr[pp, slot] * _lane_tile(alpha, 2 * LANES) + pv

    def body(i, carry):
        for pp in pairs:
            value_stage(pp, tab_ref[1, i], tab_ref[2, i])
        for pp in pairs:
            softmax_stage(pp, tab_ref[2, i + 1])
        for pp in pairs:
            score_stage(pp, tab_ref[0, i + 2], tab_ref[1, i + 2], tab_ref[3, i + 2])
        return carry

    lax.fori_loop(0, n_iter, body, 0, unroll=4)

    for pp in pairs:
        lanes = _pair_lanes(pp)
        for q_blk in range(n_q):
            rows = slice(q_blk * t, (q_blk + 1) * t)
            acc = acc_scr[pp, q_blk]
            o = acc[:, 0:LANES] / acc[:, LANES:2 * LANES]
            o = _head_rmsnorm(o, lane) * g_ref[:, lanes] * gate_ref[rows, lanes].astype(F32)
            o_ref[rows, lanes] = o.astype(o_ref.dtype)


def _fox_attention(q, k, v, fcum, gate, g):
    bsz, s, width = q.shape
    t = FOX_T
    n_q = s // t
    n_tiles = width // LANES
    tasks = [(qb, kb, qb, int(kb == qb)) for qb in range(n_q) for kb in range(qb + 1)]
    table = _task_table(tasks, n_q)
    pp, hh = FOX_PAIRS, HEADS_PER_TILE
    seq_spec = pl.BlockSpec((None, s, pp * LANES), lambda b, hp: (b, 0, hp))
    f_tiles = fcum.reshape(bsz, n_tiles, hh, s)
    return pl.pallas_call(
        _fox_kernel,
        grid=(bsz, n_tiles // pp),
        in_specs=[pl.BlockSpec(memory_space=pltpu.SMEM), seq_spec, seq_spec, seq_spec,
                  pl.BlockSpec((None, pp, hh, s), lambda b, hp: (b, hp, 0, 0)),
                  seq_spec, pl.BlockSpec((1, pp * LANES), lambda b, hp: (0, hp))],
        out_specs=seq_spec,
        out_shape=jax.ShapeDtypeStruct((bsz, s, width), BF16),
        scratch_shapes=[pltpu.VMEM((pp, n_q, 2 * t, 2 * LANES), BF16),
                        pltpu.VMEM((pp, hh, s, LANES), BF16),
                        pltpu.VMEM((2, t, t), F32),
                        pltpu.VMEM((pp, n_q + 1, t, 2 * LANES), F32),
                        pltpu.VMEM((pp, n_q + 1, hh, t, LANES), F32),
                        pltpu.VMEM((pp, hh, t, t), F32),
                        pltpu.VMEM((pp, t, hh * t), BF16),
                        pltpu.VMEM((pp, hh, t, LANES), F32),
                        pltpu.VMEM((pp, hh, t, LANES), F32)],
        compiler_params=pltpu.CompilerParams(
            dimension_semantics=("arbitrary", "arbitrary"), vmem_limit_bytes=VMEM_LIMIT_BYTES),
        name="fox_attn",
    )(jnp.asarray(table), q, k, v, f_tiles, gate, g.reshape(1, width))


def kernel(x, c, w_ada, b_ada, g_ffn1, w_ffn1_up, w_ffn1_down, g_mix, w_in, b_forget, g_fox_q,
           g_fox_k, g_sb_out, g_fox_out, w_out, g_ffn2, w_ffn2_up, w_ffn2_down, g_final):
    bsz, s, d = x.shape
    depth = w_ada.shape[0]
    t = SB_T

    head_of = jnp.arange(FOX_WIDTH) // HEAD_DIM
    grp = jnp.where(head_of[:, None] == head_of[None, :], 1.0 / HEAD_DIM, 0.0).astype(BF16)
    key = jnp.arange(t)
    tri = (key[:, None] >= key[None, :]).astype(BF16)

    n_main = 3 * SB_WIDTH + 4 * FOX_WIDTH
    for l in range(depth):
        ada = _ada(c, w_ada[l], b_ada[l]).reshape(bsz, N_SUBLAYERS, 3, 1, d)
        shift, scale, gate = ada[:, :, 0], ada[:, :, 1], ada[:, :, 2]

        x = _ffn(x, shift[:, 0], scale[:, 0], gate[:, 0], g_ffn1[l],
                 w_ffn1_up[l], w_ffn1_down[l])

        w_f = jnp.pad(w_in[l][:, n_main:], ((0, 0), (0, LANES - FOX_HEADS))).astype(BF16)
        sbq, sbk, sbv, fxq, fxk, fxv, fxg, fcum = _in_proj(
            x, shift[:, 1], scale[:, 1], g_mix[l], w_in[l], w_f,
            b_forget[l].reshape(FOX_HEADS, 1),
            jnp.tile(g_fox_q[l], FOX_HEADS).reshape(1, FOX_WIDTH),
            jnp.tile(g_fox_k[l], FOX_HEADS).reshape(1, FOX_WIDTH), grp)

        o_sb = _sb_attention(sbq, sbk, sbv, g_sb_out[l], tri)
        o_fx = _fox_attention(fxq, fxk, fxv, fcum, fxg, g_fox_out[l])

        x = _ffn(x, shift[:, 2], scale[:, 2], gate[:, 2], g_ffn2[l],
                 w_ffn2_up[l], w_ffn2_down[l],
                 mix=(o_sb, o_fx, w_out[l], gate[:, 1]),
                 g_final=g_final if l == depth - 1 else None)
    return x
```

```python
import functools

import numpy as np

import jax
import jax.numpy as jnp
from jax import lax
from jax.experimental import pallas as pl
from jax.experimental.pallas import tpu as pltpu

D_MODEL = 1024
HEAD_DIM = 64
SB_HEADS = 8
FOX_HEADS = 8
SB_WIDTH = SB_HEADS * HEAD_DIM
FOX_WIDTH = FOX_HEADS * HEAD_DIM
FFN_HIDDEN = 2816
N_SUBLAYERS = 3
EPS = 1e-6
QK_SCALE = HEAD_DIM ** -0.5

LANES = 128
HEADS_PER_TILE = LANES // HEAD_DIM
VMEM_LIMIT_BYTES = 56 * 1024 * 1024

ADA_TN = 1152
FFN_TM = 512
FFN_HC = 256
PROJ_TM = 512
SB_T = 256
FOX_T = 512
SB_PAIRS = 2
FOX_PAIRS = 2
PIPE_LAG = 2
ROW_CHUNK = 32

LOG2E = 1.4426950408889634
F32_EXP2_ZERO_BELOW = -150.0
MASKED = -1e30

BF16 = jnp.bfloat16
F32 = jnp.float32


def _dot(a, b):
    return jnp.dot(a, b, preferred_element_type=F32)


def _dot_nt(a, b):
    return lax.dot_general(a, b, (((1,), (1,)), ((), ())), preferred_element_type=F32)


def _sigmoid(x):
    return 1.0 / (1.0 + jnp.exp(-x))


def _softplus(x):
    return jnp.maximum(x, 0.0) + jnp.log(1.0 + jnp.exp(-jnp.abs(x)))


def _modulate(x, g, shift, scale):
    ms = jnp.mean(x * x, axis=-1, keepdims=True)
    return (x * lax.rsqrt(ms + EPS)) * (g * (1.0 + scale)) + shift


def _const_spec(shape):
    return pl.BlockSpec(shape, lambda *_: (0,) * len(shape), pipeline_mode=pl.Buffered(1))


def _ada_kernel(c_ref, w_ref, b_ref, o_ref):
    c = c_ref[...]
    cond = (c * _sigmoid(c)).astype(BF16)
    o_ref[...] = _dot(cond, w_ref[...].astype(BF16)) + b_ref[...]


def _ada(c, w, b):
    bsz, d = c.shape
    n = w.shape[1]
    return pl.pallas_call(
        _ada_kernel,
        grid=(n // ADA_TN,),
        in_specs=[
            pl.BlockSpec((bsz, d), lambda j: (0, 0)),
            pl.BlockSpec((d, ADA_TN), lambda j: (0, j)),
            pl.BlockSpec((1, ADA_TN), lambda j: (0, j)),
        ],
        out_specs=pl.BlockSpec((bsz, ADA_TN), lambda j: (0, j)),
        out_shape=jax.ShapeDtypeStruct((bsz, n), F32),
        compiler_params=pltpu.CompilerParams(
            dimension_semantics=("arbitrary",), vmem_limit_bytes=VMEM_LIMIT_BYTES),
        name="ada",
    )(c, w, b.reshape(1, n))


def _ffn_kernel(*refs, pre_mix, final_norm):
    it = iter(refs)
    x_ref = next(it)
    if pre_mix:
        osb_ref, ofx_ref, wout_ref, gate_mix_ref = next(it), next(it), next(it), next(it)
    shift_ref, scale_ref, gate_ref, g_ref = next(it), next(it), next(it), next(it)
    wup_ref, wdown_ref = next(it), next(it)
    if final_norm:
        gf_ref = next(it)
    o_ref, h_scr, acc_scr = next(it), next(it), next(it)

    x = x_ref[...]
    if pre_mix:
        mix = (_dot(osb_ref[...], wout_ref[0:SB_WIDTH, :].astype(BF16))
               + _dot(ofx_ref[...], wout_ref[SB_WIDTH:SB_WIDTH + FOX_WIDTH, :].astype(BF16)))
        x = x + gate_mix_ref[...] * mix
    o_ref[...] = x
    h_scr[...] = _modulate(x, g_ref[...], shift_ref[...], scale_ref[...]).astype(BF16)
    acc_scr[...] = jnp.zeros_like(acc_scr)

    hidden = wdown_ref.shape[0]
    for lo in range(0, hidden, FFN_HC):
        h = h_scr[...]
        gt = _dot(h, wup_ref[:, lo:lo + FFN_HC].astype(BF16))
        up = _dot(h, wup_ref[:, hidden + lo:hidden + lo + FFN_HC].astype(BF16))
        a = (gt * _sigmoid(gt) * up).astype(BF16)
        acc_scr[...] += _dot(a, wdown_ref[lo:lo + FFN_HC, :].astype(BF16))
    y = o_ref[...] + (0.5 * gate_ref[...]) * acc_scr[...]
    if final_norm:
        ms = jnp.mean(y * y, axis=-1, keepdims=True)
        y = (y * lax.rsqrt(ms + EPS)) * gf_ref[...]
    o_ref[...] = y


def _ffn(x, shift, scale, gate, g, w_up, w_down, mix=None, g_final=None):
    bsz, s, d = x.shape
    assert w_down.shape[0] % FFN_HC == 0
    tm = FFN_TM
    tok = pl.BlockSpec((None, tm, d), lambda b, i: (b, i, 0))
    per_batch = pl.BlockSpec((None, 1, d), lambda b, i: (b, 0, 0))
    args, specs = [x], [tok]
    if mix is not None:
        o_sb, o_fx, w_out, gate_mix = mix
        args += [o_sb, o_fx, w_out, gate_mix]
        specs += [pl.BlockSpec((None, tm, SB_WIDTH), lambda b, i: (b, i, 0)),
                  pl.BlockSpec((None, tm, FOX_WIDTH), lambda b, i: (b, i, 0)),
                  _const_spec(w_out.shape), per_batch]
    args += [shift, scale, gate, g.reshape(1, d), w_up, w_down]
    specs += [per_batch, per_batch, per_batch, _const_spec((1, d)),
              _const_spec(w_up.shape), _const_spec(w_down.shape)]
    if g_final is not None:
        args.append(g_final.reshape(1, d))
        specs.append(_const_spec((1, d)))
    return pl.pallas_call(
        functools.partial(_ffn_kernel, pre_mix=mix is not None, final_norm=g_final is not None),
        grid=(bsz, s // tm),
        in_specs=specs,
        out_specs=tok,
        out_shape=jax.ShapeDtypeStruct((bsz, s, d), F32),
        scratch_shapes=[pltpu.VMEM((tm, d), BF16), pltpu.VMEM((tm, d), F32)],
        compiler_params=pltpu.CompilerParams(
            dimension_semantics=("arbitrary", "arbitrary"), vmem_limit_bytes=VMEM_LIMIT_BYTES),
        name="ffn_mix" if mix is not None else "ffn",
    )(*args)


def _lane_cumsum(y):
    n = y.shape[-1]
    lane = lax.broadcasted_iota(jnp.int32, y.shape, y.ndim - 1)
    shift = 1
    while shift < n:
        y = y + jnp.where(lane >= shift, pltpu.roll(y, shift, axis=y.ndim - 1), 0.0)
        shift *= 2
    return y


def _proj_kernel(x_ref, shift_ref, scale_ref, g_ref, w_ref, wf_ref, bf_ref, gq_ref, gk_ref,
                 grp_ref, sbq_ref, sbk_ref, sbv_ref, fxq_ref, fxk_ref, fxv_ref, fxg_ref,
                 fcum_ref, h_scr, carry_scr):
    @pl.when(pl.program_id(1) == 0)
    def _():
        carry_scr[...] = jnp.zeros_like(carry_scr)

    x = x_ref[...]
    h_scr[...] = _modulate(x, g_ref[...], shift_ref[...], scale_ref[...]).astype(BF16)
    h = h_scr[...]
    w = SB_WIDTH

    def proj(idx):
        return _dot(h, w_ref[:, idx * w:(idx + 1) * w].astype(BF16))

    def head_rms(p):
        ms = _dot((p * p).astype(BF16), grp_ref[...])
        return p * lax.rsqrt(ms + EPS)

    f = _dot(h, wf_ref[...])
    ft = jnp.transpose(f)[0:FOX_HEADS, :] + bf_ref[...]
    log_f = -_softplus(-ft)
    fcum = _lane_cumsum(log_f) + carry_scr[:, 0:1]
    fcum_ref[...] = fcum
    carry_scr[...] = jnp.broadcast_to(fcum[:, fcum.shape[1] - 1:], carry_scr.shape)

    fxq_ref[...] = (head_rms(proj(3)) * (gq_ref[...] * (QK_SCALE * LOG2E))).astype(BF16)
    fxk_ref[...] = (head_rms(proj(4)) * gk_ref[...]).astype(BF16)
    fxg_ref[...] = _sigmoid(proj(6)).astype(BF16)
    sbq_ref[...] = (proj(0) * (QK_SCALE * LOG2E)).astype(BF16)
    sbk_ref[...] = proj(1).astype(BF16)
    sbv_ref[...] = proj(2).astype(BF16)
    fxv_ref[...] = proj(5).astype(BF16)


def _in_proj(x, shift, scale, g, w_main, w_f, b_f, gq, gk, grp):
    bsz, s, d = x.shape
    tm = PROJ_TM
    tok = pl.BlockSpec((None, tm, d), lambda b, i: (b, i, 0))
    per_batch = pl.BlockSpec((None, 1, d), lambda b, i: (b, 0, 0))
    head_out = pl.BlockSpec((None, tm, SB_WIDTH), lambda b, i: (b, i, 0))
    head_shape = jax.ShapeDtypeStruct((bsz, s, SB_WIDTH), BF16)
    return pl.pallas_call(
        _proj_kernel,
        grid=(bsz, s // tm),
        in_specs=[tok, per_batch, per_batch, _const_spec((1, d)), _const_spec(w_main.shape),
                  _const_spec(w_f.shape), _const_spec(b_f.shape), _const_spec(gq.shape),
                  _const_spec(gk.shape), _const_spec(grp.shape)],
        out_specs=[head_out] * 7 + [pl.BlockSpec((None, FOX_HEADS, tm), lambda b, i: (b, 0, i))],
        out_shape=[head_shape] * 7 + [jax.ShapeDtypeStruct((bsz, FOX_HEADS, s), F32)],
        scratch_shapes=[pltpu.VMEM((tm, d), BF16), pltpu.VMEM((FOX_HEADS, LANES), F32)],
        compiler_params=pltpu.CompilerParams(
            dimension_semantics=("arbitrary", "arbitrary"), vmem_limit_bytes=VMEM_LIMIT_BYTES),
        name="in_proj",
    )(x, shift, scale, g.reshape(1, d), w_main, w_f, b_f, gq, gk, grp)


def _split_heads(x):
    lane = lax.broadcasted_iota(jnp.int32, x.shape, 1)
    zero = jnp.zeros_like(x)
    return jnp.where(lane < HEAD_DIM, x, zero), jnp.where(lane >= HEAD_DIM, x, zero)


def _pair_lanes(pp):
    return slice(pp * LANES, (pp + 1) * LANES)


def _fill_vexp(v_ref, lanes, vexp_ref, n_blocks, t):
    def body(j, carry):
        v0, v1 = _split_heads(v_ref[pl.ds(pl.multiple_of(j * t, t), t), lanes])
        vexp_ref[j, 0:t, 0:LANES] = v0
        vexp_ref[j, t:2 * t, 0:LANES] = v1
        return carry

    lax.fori_loop(0, n_blocks, body, 0)


def _fill_vexp_ones(vexp_ref, n_blocks, t):
    def body(j, carry):
        lane = lax.broadcasted_iota(jnp.int32, (t, LANES), 1)
        vexp_ref[j, 0:t, LANES:2 * LANES] = (lane < HEAD_DIM).astype(F32).astype(vexp_ref.dtype)
        vexp_ref[j, t:2 * t, LANES:2 * LANES] = (lane >= HEAD_DIM).astype(F32).astype(
            vexp_ref.dtype)
        return carry

    lax.fori_loop(0, n_blocks, body, 0)


def _per_head(lane, a0, a1):
    return jnp.where(lane < HEAD_DIM, a0, a1)


def _lane_tile(a, width):
    return jnp.concatenate([a] * (width // LANES), axis=1)


def _head_rmsnorm(o, lane):
    sq = o * o
    ss0 = jnp.sum(jnp.where(lane < HEAD_DIM, sq, 0.0), axis=1, keepdims=True)
    ss1 = jnp.sum(jnp.where(lane >= HEAD_DIM, sq, 0.0), axis=1, keepdims=True)
    inv = _per_head(lane, lax.rsqrt(ss0 * (1.0 / HEAD_DIM) + EPS),
                    lax.rsqrt(ss1 * (1.0 / HEAD_DIM) + EPS))
    return o * inv


def _task_table(tasks, n_q):
    idle = [(0, 0, n_q, 0)] * PIPE_LAG
    return np.asarray(idle + list(tasks) + idle, dtype=np.int32).T.copy()


def _fill_qexp(q_ref, lanes, qexp_ref, n_blocks, t):
    def body(j, carry):
        rows = pl.ds(pl.multiple_of(j * t, t), t)
        q0, q1 = _split_heads(q_ref[rows, lanes])
        qexp_ref[0, rows, :] = q0
        qexp_ref[1, rows, :] = q1
        return carry

    lax.fori_loop(0, n_blocks, body, 0)


def _score_stage(q_blk, k_blk, qexp_ref, k_ref, lanes, s_ref, t):
    rows = pl.ds(pl.multiple_of(q_blk * t, t), t)
    k = k_ref[pl.ds(pl.multiple_of(k_blk * t, t), t), lanes]
    for hd in range(HEADS_PER_TILE):
        s_ref[hd] = _dot_nt(qexp_ref[hd, rows, :], k)


def _first_grid_step():
    return jnp.logical_and(pl.program_id(0) == 0, pl.program_id(1) == 0)


def _causal_bias(t, strict):
    row = lax.broadcasted_iota(jnp.int32, (t, t), 0)
    col = lax.broadcasted_iota(jnp.int32, (t, t), 1)
    keep = col < row if strict else col <= row
    return jnp.where(keep, 0.0, MASKED).astype(F32)


def _sb_kernel(tab_ref, q_ref, k_ref, v_ref, g_ref, tri_ref, o_ref, vexp_scr, qexp_scr, bias_scr,
               acc_scr, carry_scr, z_scr, zc_scr, suf_scr, sp_scr, w_scr, l_scr):
    t = SB_T
    n_q = q_ref.shape[0] // t
    n_iter = tab_ref.shape[1] - PIPE_LAG
    pairs = range(q_ref.shape[1] // LANES)

    for pp in pairs:
        _fill_vexp(v_ref, _pair_lanes(pp), vexp_scr.at[pp], n_q, t)
        _fill_qexp(q_ref, _pair_lanes(pp), qexp_scr.at[pp], n_q, t)
    acc_scr[...] = jnp.zeros_like(acc_scr)
    carry_scr[...] = jnp.zeros_like(carry_scr)

    @pl.when(_first_grid_step())
    def _():
        bias_scr[0] = jnp.zeros((t, t), F32)
        bias_scr[1] = _causal_bias(t, strict=True)
        z_scr[...] = jnp.zeros_like(z_scr)
        zc_scr[...] = jnp.zeros_like(zc_scr)
        suf_scr[...] = jnp.zeros_like(suf_scr)
        l_scr[...] = jnp.zeros_like(l_scr)

    def score_stage(pp, q_blk, k_blk, mask_idx):
        rows = pl.ds(pl.multiple_of(q_blk * t, t), t)
        k = k_ref[pl.ds(pl.multiple_of(k_blk * t, t), t), _pair_lanes(pp)]
        bias = bias_scr[mask_idx]
        for hd in range(HEADS_PER_TILE):
            z = _dot_nt(qexp_scr[pp, hd, rows, :], k) + bias
            z_scr[pp, hd] = z
            l_scr[pp, hd] = jnp.log2(1.0 + jnp.exp2(-jnp.abs(z)))

    def keep_stage(pp, slot):
        for hd in range(HEADS_PER_TILE):
            for r in range(0, t, ROW_CHUNK):
                rows = slice(r, r + ROW_CHUNK)
                z = z_scr[pp, hd, rows, :]
                sp = jnp.maximum(z, 0.0) + l_scr[pp, hd, rows, :]
                sp_scr[pp, hd, rows, :] = sp.astype(BF16)
                zc_scr[pp, hd, rows, :] = z - _lane_tile(carry_scr[pp, slot, hd, rows, :], t)
            suf = _dot(sp_scr[pp, hd], tri_ref[...])
            suf_scr[pp, hd] = suf
            carry_scr[pp, slot, hd] = carry_scr[pp, slot, hd] + suf[:, 0:1]

    def value_stage(pp, k_blk, slot):
        for hd in range(HEADS_PER_TILE):
            for r in range(0, t, ROW_CHUNK):
                rows = slice(r, r + ROW_CHUNK)
                w_scr[pp, rows, hd * t:(hd + 1) * t] = jnp.exp2(
                    zc_scr[pp, hd, rows, :] - suf_scr[pp, hd, rows, :]).astype(BF16)
        acc_scr[pp, slot] += _dot(w_scr[pp], vexp_scr[pp, k_blk])

    def body(i, carry):
        for pp in pairs:
            value_stage(pp, tab_ref[1, i], tab_ref[2, i])
        for pp in pairs:
            keep_stage(pp, tab_ref[2, i + 1])
        for pp in pairs:
            score_stage(pp, tab_ref[0, i + 2], tab_ref[1, i + 2], tab_ref[3, i + 2])
        return carry

    lax.fori_loop(0, n_iter, body, 0)

    def tail(pp, q_blk):
        def live():
            return (jnp.min(carry_scr[pp, q_blk]) <= -F32_EXP2_ZERO_BELOW).astype(jnp.int32)

        def cond(state):
            k_blk, go = state
            return jnp.logical_and(k_blk >= 0, go > 0)

        def step(state):
            k_blk, _ = state
            score_stage(pp, q_blk, k_blk, 0)
            keep_stage(pp, q_blk)
            value_stage(pp, k_blk, q_blk)
            return k_blk - 1, live()

        lax.while_loop(cond, step, (q_blk - 2, live()))

    lane = lax.broadcasted_iota(jnp.int32, (t, LANES), 1)
    for pp in pairs:
        def tail_body(q_blk, carry, pp=pp):
            tail(pp, q_blk)
            return carry

        @pl.when(jnp.min(carry_scr[pp, 2:n_q]) <= -F32_EXP2_ZERO_BELOW)
        def _():
            lax.fori_loop(2, n_q, tail_body, 0)

        lanes = _pair_lanes(pp)
        for q_blk in range(n_q):
            o = _head_rmsnorm(acc_scr[pp, q_blk], lane) * g_ref[:, lanes]
            o_ref[q_blk * t:(q_blk + 1) * t, lanes] = o.astype(o_ref.dtype)


def _sb_attention(q, k, v, g, tri):
    bsz, s, width = q.shape
    t = SB_T
    n_q = s // t
    n_tiles = width // LANES
    tasks = []
    for qb in range(n_q):
        tasks.append((qb, qb, qb, 1))
        if qb >= 1:
            tasks.append((qb, qb - 1, qb, 0))
    table = _task_table(tasks, n_q)
    pp, hh = SB_PAIRS, HEADS_PER_TILE
    seq_spec = pl.BlockSpec((None, s, pp * LANES), lambda b, hp: (b, 0, hp))
    return pl.pallas_call(
        _sb_kernel,
        grid=(bsz, n_tiles // pp),
        in_specs=[pl.BlockSpec(memory_space=pltpu.SMEM), seq_spec, seq_spec, seq_spec,
                  pl.BlockSpec((1, pp * LANES), lambda b, hp: (0, hp)), _const_spec(tri.shape)],
        out_specs=seq_spec,
        out_shape=jax.ShapeDtypeStruct((bsz, s, width), BF16),
        scratch_shapes=[pltpu.VMEM((pp, n_q, 2 * t, LANES), BF16),
                        pltpu.VMEM((pp, hh, s, LANES), BF16),
                        pltpu.VMEM((2, t, t), F32),
                        pltpu.VMEM((pp, n_q + 1, t, LANES), F32),
                        pltpu.VMEM((pp, n_q + 1, hh, t, LANES), F32),
                        pltpu.VMEM((pp, hh, t, t), F32),
                        pltpu.VMEM((pp, hh, t, t), F32),
                        pltpu.VMEM((pp, hh, t, t), F32),
                        pltpu.VMEM((pp, hh, t, t), BF16),
                        pltpu.VMEM((pp, t, hh * t), BF16),
                        pltpu.VMEM((pp, hh, t, t), F32)],
        compiler_params=pltpu.CompilerParams(
            dimension_semantics=("arbitrary", "arbitrary"), vmem_limit_bytes=VMEM_LIMIT_BYTES),
        name="sb_attn",
    )(jnp.asarray(table), q, k, v, g.reshape(1, width), tri)


def _fox_kernel(tab_ref, q_ref, k_ref, v_ref, f_ref, gate_ref, g_ref, o_ref, vexp_scr, qexp_scr,
                bias_scr, acc_scr, m_scr, s_scr, p_scr, alpha_scr, bmax_scr):
    t = FOX_T
    n_q = q_ref.shape[0] // t
    n_iter = tab_ref.shape[1] - PIPE_LAG
    pairs = range(q_ref.shape[1] // LANES)
    lane = lax.broadcasted_iota(jnp.int32, (t, LANES), 1)

    for pp in pairs:
        _fill_vexp(v_ref, _pair_lanes(pp), vexp_scr.at[pp], n_q, t)
        _fill_qexp(q_ref, _pair_lanes(pp), qexp_scr.at[pp], n_q, t)
    acc_scr[...] = jnp.zeros_like(acc_scr)
    m_scr[...] = jnp.full_like(m_scr, -jnp.inf)

    @pl.when(_first_grid_step())
    def _():
        for pp in pairs:
            _fill_vexp_ones(vexp_scr.at[pp], n_q, t)
        bias_scr[0] = jnp.zeros((t, t), F32)
        bias_scr[1] = _causal_bias(t, strict=False)
        s_scr[...] = jnp.zeros_like(s_scr)
        bmax_scr[...] = jnp.zeros_like(bmax_scr)
        p_scr[...] = jnp.zeros_like(p_scr)
        alpha_scr[...] = jnp.zeros_like(alpha_scr)

    def score_stage(pp, q_blk, k_blk, mask_idx):
        rows = pl.ds(pl.multiple_of(q_blk * t, t), t)
        cols = pl.ds(pl.multiple_of(k_blk * t, t), t)
        k = k_ref[cols, _pair_lanes(pp)]
        f_keys = f_ref[pp, :, cols] * LOG2E
        bias = bias_scr[mask_idx]
        for hd in range(HEADS_PER_TILE):
            sc = _dot_nt(qexp_scr[pp, hd, rows, :], k) - f_keys[hd:hd + 1, :] + bias
            s_scr[pp, hd] = sc
            bmax_scr[pp, hd] = jnp.broadcast_to(jnp.max(sc, axis=1, keepdims=True), (t, LANES))

    def softmax_stage(pp, slot):
        for hd in range(HEADS_PER_TILE):
            m_old = m_scr[pp, slot, hd]
            m_new = jnp.maximum(m_old, bmax_scr[pp, hd])
            alpha_scr[pp, hd] = jnp.exp2(m_old - m_new)
            m_scr[pp, slot, hd] = m_new
            p_scr[pp, :, hd * t:(hd + 1) * t] = jnp.exp2(
                s_scr[pp, hd] - _lane_tile(m_new, t)).astype(BF16)

    def value_stage(pp, k_blk, slot):
        pv = _dot(p_scr[pp], vexp_scr[pp, k_blk])
        alpha = _per_head(lane, alpha_scr[pp, 0], alpha_scr[pp, 1])
        acc_scr[pp, slot] = acc_scr[pp, slot] * _lane_tile(alpha, 2 * LANES) + pv

    def body(i, carry):
        for pp in pairs:
            value_stage(pp, tab_ref[1, i], tab_ref[2, i])
        for pp in pairs:
            softmax_stage(pp, tab_ref[2, i + 1])
        for pp in pairs:
            score_stage(pp, tab_ref[0, i + 2], tab_ref[1, i + 2], tab_ref[3, i + 2])
        return carry

    lax.fori_loop(0, n_iter, body, 0)

    for pp in pairs:
        lanes = _pair_lanes(pp)
        for q_blk in range(n_q):
            rows = slice(q_blk * t, (q_blk + 1) * t)
            acc = acc_scr[pp, q_blk]
            o = acc[:, 0:LANES] / acc[:, LANES:2 * LANES]
            o = _head_rmsnorm(o, lane) * g_ref[:, lanes] * gate_ref[rows, lanes].astype(F32)
            o_ref[rows, lanes] = o.astype(o_ref.dtype)


def _fox_attention(q, k, v, fcum, gate, g):
    bsz, s, width = q.shape
    t = FOX_T
    n_q = s // t
    n_tiles = width // LANES
    tasks = [(qb, kb, qb, int(kb == qb)) for qb in range(n_q) for kb in range(qb + 1)]
    table = _task_table(tasks, n_q)
    pp, hh = FOX_PAIRS, HEADS_PER_TILE
    seq_spec = pl.BlockSpec((None, s, pp * LANES), lambda b, hp: (b, 0, hp))
    f_tiles = fcum.reshape(bsz, n_tiles, hh, s)
    return pl.pallas_call(
        _fox_kernel,
        grid=(bsz, n_tiles // pp),
        in_specs=[pl.BlockSpec(memory_space=pltpu.SMEM), seq_spec, seq_spec, seq_spec,
                  pl.BlockSpec((None, pp, hh, s), lambda b, hp: (b, hp, 0, 0)),
                  seq_spec, pl.BlockSpec((1, pp * LANES), lambda b, hp: (0, hp))],
        out_specs=seq_spec,
        out_shape=jax.ShapeDtypeStruct((bsz, s, width), BF16),
        scratch_shapes=[pltpu.VMEM((pp, n_q, 2 * t, 2 * LANES), BF16),
                        pltpu.VMEM((pp, hh, s, LANES), BF16),
                        pltpu.VMEM((2, t, t), F32),
                        pltpu.VMEM((pp, n_q + 1, t, 2 * LANES), F32),
                        pltpu.VMEM((pp, n_q + 1, hh, t, LANES), F32),
                        pltpu.VMEM((pp, hh, t, t), F32),
                        pltpu.VMEM((pp, t, hh * t), BF16),
                        pltpu.VMEM((pp, hh, t, LANES), F32),
                        pltpu.VMEM((pp, hh, t, LANES), F32)],
        compiler_params=pltpu.CompilerParams(
            dimension_semantics=("arbitrary", "arbitrary"), vmem_limit_bytes=VMEM_LIMIT_BYTES),
        name="fox_attn",
    )(jnp.asarray(table), q, k, v, f_tiles, gate, g.reshape(1, width))


def kernel(x, c, w_ada, b_ada, g_ffn1, w_ffn1_up, w_ffn1_down, g_mix, w_in, b_forget, g_fox_q,
           g_fox_k, g_sb_out, g_fox_out, w_out, g_ffn2, w_ffn2_up, w_ffn2_down, g_final):
    bsz, s, d = x.shape
    depth = w_ada.shape[0]
    t = SB_T

    head_of = jnp.arange(FOX_WIDTH) // HEAD_DIM
    grp = jnp.where(head_of[:, None] == head_of[None, :], 1.0 / HEAD_DIM, 0.0).astype(BF16)
    key = jnp.arange(t)
    tri = (key[:, None] >= key[None, :]).astype(BF16)

    n_main = 3 * SB_WIDTH + 4 * FOX_WIDTH
    for l in range(depth):
        ada = _ada(c, w_ada[l], b_ada[l]).reshape(bsz, N_SUBLAYERS, 3, 1, d)
        shift, scale, gate = ada[:, :, 0], ada[:, :, 1], ada[:, :, 2]

        x = _ffn(x, shift[:, 0], scale[:, 0], gate[:, 0], g_ffn1[l],
                 w_ffn1_up[l], w_ffn1_down[l])

        w_f = jnp.pad(w_in[l][:, n_main:], ((0, 0), (0, LANES - FOX_HEADS))).astype(BF16)
        sbq, sbk, sbv, fxq, fxk, fxv, fxg, fcum = _in_proj(
            x, shift[:, 1], scale[:, 1], g_mix[l], w_in[l], w_f,
            b_forget[l].reshape(FOX_HEADS, 1),
            jnp.tile(g_fox_q[l], FOX_HEADS).reshape(1, FOX_WIDTH),
            jnp.tile(g_fox_k[l], FOX_HEADS).reshape(1, FOX_WIDTH), grp)

        o_sb = _sb_attention(sbq, sbk, sbv, g_sb_out[l], tri)
        o_fx = _fox_attention(fxq, fxk, fxv, fcum, fxg, g_fox_out[l])

        x = _ffn(x, shift[:, 2], scale[:, 2], gate[:, 2], g_ffn2[l],
                 w_ffn2_up[l], w_ffn2_down[l],
                 mix=(o_sb, o_fx, w_out[l], gate[:, 1]),
                 g_final=g_final if l == depth - 1 else None)
    return x
```

```python
import functools

import numpy as np

import jax
import jax.numpy as jnp
from jax import lax
from jax.experimental import pallas as pl
from jax.experimental.pallas import tpu as pltpu

D_MODEL = 1024
HEAD_DIM = 64
SB_HEADS = 8
FOX_HEADS = 8
SB_WIDTH = SB_HEADS * HEAD_DIM
FOX_WIDTH = FOX_HEADS * HEAD_DIM
FFN_HIDDEN = 2816
N_SUBLAYERS = 3
EPS = 1e-6
QK_SCALE = HEAD_DIM ** -0.5

LANES = 128
HEADS_PER_TILE = LANES // HEAD_DIM
VMEM_LIMIT_BYTES = 56 * 1024 * 1024

ADA_TN = 1152
FFN_TM = 512
FFN_HC = 256
PROJ_TM = 512
SB_T = 256
FOX_T = 512
SB_PAIRS = 2
FOX_PAIRS = 2
PIPE_LAG = 2
ROW_CHUNK = 32

LOG2E = 1.4426950408889634
F32_EXP2_ZERO_BELOW = -150.0
MASKED = -1e30

BF16 = jnp.bfloat16
F32 = jnp.float32


def _dot(a, b):
    return jnp.dot(a, b, preferred_element_type=F32)


def _dot_nt(a, b):
    return lax.dot_general(a, b, (((1,), (1,)), ((), ())), preferred_element_type=F32)


def _sigmoid(x):
    return 1.0 / (1.0 + jnp.exp(-x))


def _softplus(x):
    return jnp.maximum(x, 0.0) + jnp.log(1.0 + jnp.exp(-jnp.abs(x)))


def _modulate(x, g, shift, scale):
    ms = jnp.mean(x * x, axis=-1, keepdims=True)
    return (x * lax.rsqrt(ms + EPS)) * (g * (1.0 + scale)) + shift


def _const_spec(shape):
    return pl.BlockSpec(shape, lambda *_: (0,) * len(shape), pipeline_mode=pl.Buffered(1))


def _ada_kernel(c_ref, w_ref, b_ref, o_ref):
    c = c_ref[...]
    cond = (c * _sigmoid(c)).astype(BF16)
    o_ref[...] = _dot(cond, w_ref[...].astype(BF16)) + b_ref[...]


def _ada(c, w, b):
    bsz, d = c.shape
    n = w.shape[1]
    return pl.pallas_call(
        _ada_kernel,
        grid=(n // ADA_TN,),
        in_specs=[
            pl.BlockSpec((bsz, d), lambda j: (0, 0)),
            pl.BlockSpec((d, ADA_TN), lambda j: (0, j)),
            pl.BlockSpec((1, ADA_TN), lambda j: (0, j)),
        ],
        out_specs=pl.BlockSpec((bsz, ADA_TN), lambda j: (0, j)),
        out_shape=jax.ShapeDtypeStruct((bsz, n), F32),
        compiler_params=pltpu.CompilerParams(
            dimension_semantics=("arbitrary",), vmem_limit_bytes=VMEM_LIMIT_BYTES),
        name="ada",
    )(c, w, b.reshape(1, n))


def _ffn_kernel(*refs, pre_mix, final_norm):
    it = iter(refs)
    x_ref = next(it)
    if pre_mix:
        osb_ref, ofx_ref, wout_ref, gate_mix_ref = next(it), next(it), next(it), next(it)
    shift_ref, scale_ref, gate_ref, g_ref = next(it), next(it), next(it), next(it)
    wup_ref, wdown_ref = next(it), next(it)
    if final_norm:
        gf_ref = next(it)
    o_ref, h_scr, acc_scr = next(it), next(it), next(it)

    x = x_ref[...]
    if pre_mix:
        mix = (_dot(osb_ref[...], wout_ref[0:SB_WIDTH, :].astype(BF16))
               + _dot(ofx_ref[...], wout_ref[SB_WIDTH:SB_WIDTH + FOX_WIDTH, :].astype(BF16)))
        x = x + gate_mix_ref[...] * mix
    o_ref[...] = x
    h_scr[...] = _modulate(x, g_ref[...], shift_ref[...], scale_ref[...]).astype(BF16)
    acc_scr[...] = jnp.zeros_like(acc_scr)

    hidden = wdown_ref.shape[0]
    for lo in range(0, hidden, FFN_HC):
        h = h_scr[...]
        gt = _dot(h, wup_ref[:, lo:lo + FFN_HC].astype(BF16))
        up = _dot(h, wup_ref[:, hidden + lo:hidden + lo + FFN_HC].astype(BF16))
        a = (gt * _sigmoid(gt) * up).astype(BF16)
        acc_scr[...] += _dot(a, wdown_ref[lo:lo + FFN_HC, :].astype(BF16))
    y = o_ref[...] + (0.5 * gate_ref[...]) * acc_scr[...]
    if final_norm:
        ms = jnp.mean(y * y, axis=-1, keepdims=True)
        y = (y * lax.rsqrt(ms + EPS)) * gf_ref[...]
    o_ref[...] = y


def _ffn(x, shift, scale, gate, g, w_up, w_down, mix=None, g_final=None):
    bsz, s, d = x.shape
    assert w_down.shape[0] % FFN_HC == 0
    tm = FFN_TM
    tok = pl.BlockSpec((None, tm, d), lambda b, i: (b, i, 0))
    per_batch = pl.BlockSpec((None, 1, d), lambda b, i: (b, 0, 0))
    args, specs = [x], [tok]
    if mix is not None:
        o_sb, o_fx, w_out, gate_mix = mix
        args += [o_sb, o_fx, w_out, gate_mix]
        specs += [pl.BlockSpec((None, tm, SB_WIDTH), lambda b, i: (b, i, 0)),
                  pl.BlockSpec((None, tm, FOX_WIDTH), lambda b, i: (b, i, 0)),
                  _const_spec(w_out.shape), per_batch]
    args += [shift, scale, gate, g.reshape(1, d), w_up, w_down]
    specs += [per_batch, per_batch, per_batch, _const_spec((1, d)),
              _const_spec(w_up.shape), _const_spec(w_down.shape)]
    if g_final is not None:
        args.append(g_final.reshape(1, d))
        specs.append(_const_spec((1, d)))
    return pl.pallas_call(
        functools.partial(_ffn_kernel, pre_mix=mix is not None, final_norm=g_final is not None),
        grid=(bsz, s // tm),
        in_specs=specs,
        out_specs=tok,
        out_shape=jax.ShapeDtypeStruct((bsz, s, d), F32),
        scratch_shapes=[pltpu.VMEM((tm, d), BF16), pltpu.VMEM((tm, d), F32)],
        compiler_params=pltpu.CompilerParams(
            dimension_semantics=("arbitrary", "arbitrary"), vmem_limit_bytes=VMEM_LIMIT_BYTES),
        name="ffn_mix" if mix is not None else "ffn",
    )(*args)


def _lane_cumsum(y):
    n = y.shape[-1]
    lane = lax.broadcasted_iota(jnp.int32, y.shape, y.ndim - 1)
    shift = 1
    while shift < n:
        y = y + jnp.where(lane >= shift, pltpu.roll(y, shift, axis=y.ndim - 1), 0.0)
        shift *= 2
    return y


def _proj_kernel(x_ref, shift_ref, scale_ref, g_ref, w_ref, wf_ref, bf_ref, gq_ref, gk_ref,
                 grp_ref, sbq_ref, sbk_ref, sbv_ref, fxq_ref, fxk_ref, fxv_ref, fxg_ref,
                 fcum_ref, h_scr, carry_scr):
    @pl.when(pl.program_id(1) == 0)
    def _():
        carry_scr[...] = jnp.zeros_like(carry_scr)

    x = x_ref[...]
    h_scr[...] = _modulate(x, g_ref[...], shift_ref[...], scale_ref[...]).astype(BF16)
    h = h_scr[...]
    w = SB_WIDTH

    def proj(idx):
        return _dot(h, w_ref[:, idx * w:(idx + 1) * w].astype(BF16))

    def head_rms(p):
        ms = _dot((p * p).astype(BF16), grp_ref[...])
        return p * lax.rsqrt(ms + EPS)

    f = _dot(h, wf_ref[...])
    ft = jnp.transpose(f)[0:FOX_HEADS, :] + bf_ref[...]
    log_f = -_softplus(-ft)
    fcum = _lane_cumsum(log_f) + carry_scr[:, 0:1]
    fcum_ref[...] = fcum
    carry_scr[...] = jnp.broadcast_to(fcum[:, fcum.shape[1] - 1:], carry_scr.shape)

    fxq_ref[...] = (head_rms(proj(3)) * (gq_ref[...] * (QK_SCALE * LOG2E))).astype(BF16)
    fxk_ref[...] = (head_rms(proj(4)) * gk_ref[...]).astype(BF16)
    fxg_ref[...] = _sigmoid(proj(6)).astype(BF16)
    sbq_ref[...] = (proj(0) * (QK_SCALE * LOG2E)).astype(BF16)
    sbk_ref[...] = proj(1).astype(BF16)
    sbv_ref[...] = proj(2).astype(BF16)
    fxv_ref[...] = proj(5).astype(BF16)


def _in_proj(x, shift, scale, g, w_main, w_f, b_f, gq, gk, grp):
    bsz, s, d = x.shape
    tm = PROJ_TM
    tok = pl.BlockSpec((None, tm, d), lambda b, i: (b, i, 0))
    per_batch = pl.BlockSpec((None, 1, d), lambda b, i: (b, 0, 0))
    head_out = pl.BlockSpec((None, tm, SB_WIDTH), lambda b, i: (b, i, 0))
    head_shape = jax.ShapeDtypeStruct((bsz, s, SB_WIDTH), BF16)
    return pl.pallas_call(
        _proj_kernel,
        grid=(bsz, s // tm),
        in_specs=[tok, per_batch, per_batch, _const_spec((1, d)), _const_spec(w_main.shape),
                  _const_spec(w_f.shape), _const_spec(b_f.shape), _const_spec(gq.shape),
                  _const_spec(gk.shape), _const_spec(grp.shape)],
        out_specs=[head_out] * 7 + [pl.BlockSpec((None, FOX_HEADS, tm), lambda b, i: (b, 0, i))],
        out_shape=[head_shape] * 7 + [jax.ShapeDtypeStruct((bsz, FOX_HEADS, s), F32)],
        scratch_shapes=[pltpu.VMEM((tm, d), BF16), pltpu.VMEM((FOX_HEADS, LANES), F32)],
        compiler_params=pltpu.CompilerParams(
            dimension_semantics=("arbitrary", "arbitrary"), vmem_limit_bytes=VMEM_LIMIT_BYTES),
        name="in_proj",
    )(x, shift, scale, g.reshape(1, d), w_main, w_f, b_f, gq, gk, grp)


def _split_heads(x):
    lane = lax.broadcasted_iota(jnp.int32, x.shape, 1)
    zero = jnp.zeros_like(x)
    return jnp.where(lane < HEAD_DIM, x, zero), jnp.where(lane >= HEAD_DIM, x, zero)


def _pair_lanes(pp):
    return slice(pp * LANES, (pp + 1) * LANES)


def _fill_vexp(v_ref, lanes, vexp_ref, n_blocks, t):
    def body(j, carry):
        v0, v1 = _split_heads(v_ref[pl.ds(pl.multiple_of(j * t, t), t), lanes])
        vexp_ref[j, 0:t, 0:LANES] = v0
        vexp_ref[j, t:2 * t, 0:LANES] = v1
        return carry

    lax.fori_loop(0, n_blocks, body, 0)


def _fill_vexp_ones(vexp_ref, n_blocks, t):
    def body(j, carry):
        lane = lax.broadcasted_iota(jnp.int32, (t, LANES), 1)
        vexp_ref[j, 0:t, LANES:2 * LANES] = (lane < HEAD_DIM).astype(F32).astype(vexp_ref.dtype)
        vexp_ref[j, t:2 * t, LANES:2 * LANES] = (lane >= HEAD_DIM).astype(F32).astype(
            vexp_ref.dtype)
        return carry

    lax.fori_loop(0, n_blocks, body, 0)


def _per_head(lane, a0, a1):
    return jnp.where(lane < HEAD_DIM, a0, a1)


def _lane_tile(a, width):
    return jnp.concatenate([a] * (width // LANES), axis=1)


def _head_rmsnorm(o, lane):
    sq = o * o
    ss0 = jnp.sum(jnp.where(lane < HEAD_DIM, sq, 0.0), axis=1, keepdims=True)
    ss1 = jnp.sum(jnp.where(lane >= HEAD_DIM, sq, 0.0), axis=1, keepdims=True)
    inv = _per_head(lane, lax.rsqrt(ss0 * (1.0 / HEAD_DIM) + EPS),
                    lax.rsqrt(ss1 * (1.0 / HEAD_DIM) + EPS))
    return o * inv


def _task_table(tasks, n_q):
    seen = set()
    rows = []
    for q_blk, k_blk, slot, mask_idx in tasks:
        rows.append((q_blk, k_blk, slot, mask_idx, int(slot not in seen)))
        seen.add(slot)
    idle = [(0, 0, n_q, 0, 0)] * PIPE_LAG
    return np.asarray(idle + rows + idle, dtype=np.int32).T.copy()


def _fill_qexp(q_ref, lanes, qexp_ref, n_blocks, t):
    def body(j, carry):
        rows = pl.ds(pl.multiple_of(j * t, t), t)
        q0, q1 = _split_heads(q_ref[rows, lanes])
        qexp_ref[0, rows, :] = q0
        qexp_ref[1, rows, :] = q1
        return carry

    lax.fori_loop(0, n_blocks, body, 0)


def _score_stage(q_blk, k_blk, qexp_ref, k_ref, lanes, s_ref, t):
    rows = pl.ds(pl.multiple_of(q_blk * t, t), t)
    k = k_ref[pl.ds(pl.multiple_of(k_blk * t, t), t), lanes]
    for hd in range(HEADS_PER_TILE):
        s_ref[hd] = _dot_nt(qexp_ref[hd, rows, :], k)


def _first_grid_step():
    return jnp.logical_and(pl.program_id(0) == 0, pl.program_id(1) == 0)


def _causal_bias(t, strict):
    row = lax.broadcasted_iota(jnp.int32, (t, t), 0)
    col = lax.broadcasted_iota(jnp.int32, (t, t), 1)
    keep = col < row if strict else col <= row
    return jnp.where(keep, 0.0, MASKED).astype(F32)


def _sb_kernel(tab_ref, q_ref, k_ref, v_ref, g_ref, tri_ref, o_ref, vexp_scr, qexp_scr, bias_scr,
               acc_scr, carry_scr, z_scr, zc_scr, suf_scr, sp_scr, w_scr, l_scr):
    t = SB_T
    n_q = q_ref.shape[0] // t
    n_iter = tab_ref.shape[1] - PIPE_LAG
    pairs = range(q_ref.shape[1] // LANES)

    for pp in pairs:
        _fill_vexp(v_ref, _pair_lanes(pp), vexp_scr.at[pp], n_q, t)
        _fill_qexp(q_ref, _pair_lanes(pp), qexp_scr.at[pp], n_q, t)
    @pl.when(_first_grid_step())
    def _():
        acc_scr[...] = jnp.zeros_like(acc_scr)
        carry_scr[...] = jnp.zeros_like(carry_scr)
        bias_scr[0] = jnp.zeros((t, t), F32)
        bias_scr[1] = _causal_bias(t, strict=True)
        z_scr[...] = jnp.zeros_like(z_scr)
        zc_scr[...] = jnp.zeros_like(zc_scr)
        suf_scr[...] = jnp.zeros_like(suf_scr)
        l_scr[...] = jnp.zeros_like(l_scr)

    def score_stage(pp, q_blk, k_blk, mask_idx):
        rows = pl.ds(pl.multiple_of(q_blk * t, t), t)
        k = k_ref[pl.ds(pl.multiple_of(k_blk * t, t), t), _pair_lanes(pp)]
        bias = bias_scr[mask_idx]
        for hd in range(HEADS_PER_TILE):
            z = _dot_nt(qexp_scr[pp, hd, rows, :], k) + bias
            z_scr[pp, hd] = z
            l_scr[pp, hd] = jnp.log2(1.0 + jnp.exp2(-jnp.abs(z)))

    def keep_stage(pp, slot, first):
        for hd in range(HEADS_PER_TILE):
            carry = jnp.where(first > 0, 0.0, carry_scr[pp, slot, hd])
            for r in range(0, t, ROW_CHUNK):
                rows = slice(r, r + ROW_CHUNK)
                z = z_scr[pp, hd, rows, :]
                sp = jnp.maximum(z, 0.0) + l_scr[pp, hd, rows, :]
                sp_scr[pp, hd, rows, :] = sp.astype(BF16)
                zc_scr[pp, hd, rows, :] = z - _lane_tile(carry[rows, :], t)
            suf = _dot(sp_scr[pp, hd], tri_ref[...])
            suf_scr[pp, hd] = suf
            carry_scr[pp, slot, hd] = carry + suf[:, 0:1]

    def value_stage(pp, k_blk, slot, first):
        for hd in range(HEADS_PER_TILE):
            for r in range(0, t, ROW_CHUNK):
                rows = slice(r, r + ROW_CHUNK)
                w_scr[pp, rows, hd * t:(hd + 1) * t] = jnp.exp2(
                    zc_scr[pp, hd, rows, :] - suf_scr[pp, hd, rows, :]).astype(BF16)
        acc = jnp.where(first > 0, 0.0, acc_scr[pp, slot])
        acc_scr[pp, slot] = acc + _dot(w_scr[pp], vexp_scr[pp, k_blk])

    def body(i, carry):
        for pp in pairs:
            value_stage(pp, tab_ref[1, i], tab_ref[2, i], tab_ref[4, i])
        for pp in pairs:
            keep_stage(pp, tab_ref[2, i + 1], tab_ref[4, i + 1])
        for pp in pairs:
            score_stage(pp, tab_ref[0, i + 2], tab_ref[1, i + 2], tab_ref[3, i + 2])
        return carry

    lax.fori_loop(0, n_iter, body, 0)

    def tail(pp, q_blk):
        def live():
            return (jnp.min(carry_scr[pp, q_blk]) <= -F32_EXP2_ZERO_BELOW).astype(jnp.int32)

        def cond(state):
            k_blk, go = state
            return jnp.logical_and(k_blk >= 0, go > 0)

        def step(state):
            k_blk, _ = state
            score_stage(pp, q_blk, k_blk, 0)
            keep_stage(pp, q_blk, 0)
            value_stage(pp, k_blk, q_blk, 0)
            return k_blk - 1, live()

        lax.while_loop(cond, step, (q_blk - 2, live()))

    lane = lax.broadcasted_iota(jnp.int32, (t, LANES), 1)
    for pp in pairs:
        def tail_body(q_blk, carry, pp=pp):
            tail(pp, q_blk)
            return carry

        @pl.when(jnp.min(carry_scr[pp, 2:n_q]) <= -F32_EXP2_ZERO_BELOW)
        def _():
            lax.fori_loop(2, n_q, tail_body, 0)

        lanes = _pair_lanes(pp)
        for q_blk in range(n_q):
            o = _head_rmsnorm(acc_scr[pp, q_blk], lane) * g_ref[:, lanes]
            o_ref[q_blk * t:(q_blk + 1) * t, lanes] = o.astype(o_ref.dtype)


def _sb_attention(q, k, v, g, tri):
    bsz, s, width = q.shape
    t = SB_T
    n_q = s // t
    n_tiles = width // LANES
    tasks = []
    for qb in range(n_q):
        tasks.append((qb, qb, qb, 1))
        if qb >= 1:
            tasks.append((qb, qb - 1, qb, 0))
    table = _task_table(tasks, n_q)
    pp, hh = SB_PAIRS, HEADS_PER_TILE
    seq_spec = pl.BlockSpec((None, s, pp * LANES), lambda b, hp: (b, 0, hp))
    return pl.pallas_call(
        _sb_kernel,
        grid=(bsz, n_tiles // pp),
        in_specs=[pl.BlockSpec(memory_space=pltpu.SMEM), seq_spec, seq_spec, seq_spec,
                  pl.BlockSpec((1, pp * LANES), lambda b, hp: (0, hp)), _const_spec(tri.shape)],
        out_specs=seq_spec,
        out_shape=jax.ShapeDtypeStruct((bsz, s, width), BF16),
        scratch_shapes=[pltpu.VMEM((pp, n_q, 2 * t, LANES), BF16),
                        pltpu.VMEM((pp, hh, s, LANES), BF16),
                        pltpu.VMEM((2, t, t), F32),
                        pltpu.VMEM((pp, n_q + 1, t, LANES), F32),
                        pltpu.VMEM((pp, n_q + 1, hh, t, LANES), F32),
                        pltpu.VMEM((pp, hh, t, t), F32),
                        pltpu.VMEM((pp, hh, t, t), F32),
                        pltpu.VMEM((pp, hh, t, t), F32),
                        pltpu.VMEM((pp, hh, t, t), BF16),
                        pltpu.VMEM((pp, t, hh * t), BF16),
                        pltpu.VMEM((pp, hh, t, t), F32)],
        compiler_params=pltpu.CompilerParams(
            dimension_semantics=("arbitrary", "arbitrary"), vmem_limit_bytes=VMEM_LIMIT_BYTES),
        name="sb_attn",
    )(jnp.asarray(table), q, k, v, g.reshape(1, width), tri)


def _fox_kernel(tab_ref, q_ref, k_ref, v_ref, f_ref, gate_ref, g_ref, o_ref, vexp_scr, qexp_scr,
                bias_scr, acc_scr, m_scr, s_scr, p_scr, alpha_scr, bmax_scr):
    t = FOX_T
    n_q = q_ref.shape[0] // t
    n_iter = tab_ref.shape[1] - PIPE_LAG
    pairs = range(q_ref.shape[1] // LANES)
    lane = lax.broadcasted_iota(jnp.int32, (t, LANES), 1)

    for pp in pairs:
        _fill_vexp(v_ref, _pair_lanes(pp), vexp_scr.at[pp], n_q, t)
        _fill_qexp(q_ref, _pair_lanes(pp), qexp_scr.at[pp], n_q, t)
    @pl.when(_first_grid_step())
    def _():
        acc_scr[...] = jnp.zeros_like(acc_scr)
        m_scr[...] = jnp.full_like(m_scr, -jnp.inf)
        for pp in pairs:
            _fill_vexp_ones(vexp_scr.at[pp], n_q, t)
        bias_scr[0] = jnp.zeros((t, t), F32)
        bias_scr[1] = _causal_bias(t, strict=False)
        s_scr[...] = jnp.zeros_like(s_scr)
        bmax_scr[...] = jnp.zeros_like(bmax_scr)
        p_scr[...] = jnp.zeros_like(p_scr)
        alpha_scr[...] = jnp.zeros_like(alpha_scr)

    def score_stage(pp, q_blk, k_blk, mask_idx):
        rows = pl.ds(pl.multiple_of(q_blk * t, t), t)
        cols = pl.ds(pl.multiple_of(k_blk * t, t), t)
        k = k_ref[cols, _pair_lanes(pp)]
        f_keys = f_ref[pp, :, cols] * LOG2E
        bias = bias_scr[mask_idx]
        for hd in range(HEADS_PER_TILE):
            sc = _dot_nt(qexp_scr[pp, hd, rows, :], k) - f_keys[hd:hd + 1, :] + bias
            s_scr[pp, hd] = sc
            bmax_scr[pp, hd] = jnp.broadcast_to(jnp.max(sc, axis=1, keepdims=True), (t, LANES))

    def softmax_stage(pp, slot, first):
        for hd in range(HEADS_PER_TILE):
            m_old = jnp.where(first > 0, -jnp.inf, m_scr[pp, slot, hd])
            m_new = jnp.maximum(m_old, bmax_scr[pp, hd])
            alpha_scr[pp, hd] = jnp.exp2(m_old - m_new)
            m_scr[pp, slot, hd] = m_new
            p_scr[pp, :, hd * t:(hd + 1) * t] = jnp.exp2(
                s_scr[pp, hd] - _lane_tile(m_new, t)).astype(BF16)

    def value_stage(pp, k_blk, slot):
        pv = _dot(p_scr[pp], vexp_scr[pp, k_blk])
        alpha = _per_head(lane, alpha_scr[pp, 0], alpha_scr[pp, 1])
        acc_scr[pp, slot] = acc_scr[pp, slot] * _lane_tile(alpha, 2 * LANES) + pv

    def body(i, carry):
        for pp in pairs:
            value_stage(pp, tab_ref[1, i], tab_ref[2, i])
        for pp in pairs:
            softmax_stage(pp, tab_ref[2, i + 1], tab_ref[4, i + 1])
        for pp in pairs:
            score_stage(pp, tab_ref[0, i + 2], tab_ref[1, i + 2], tab_ref[3, i + 2])
        return carry

    lax.fori_loop(0, n_iter, body, 0)

    for pp in pairs:
        lanes = _pair_lanes(pp)
        for q_blk in range(n_q):
            rows = slice(q_blk * t, (q_blk + 1) * t)
            acc = acc_scr[pp, q_blk]
            o = acc[:, 0:LANES] / acc[:, LANES:2 * LANES]
            o = _head_rmsnorm(o, lane) * g_ref[:, lanes] * gate_ref[rows, lanes].astype(F32)
            o_ref[rows, lanes] = o.astype(o_ref.dtype)


def _fox_attention(q, k, v, fcum, gate, g):
    bsz, s, width = q.shape
    t = FOX_T
    n_q = s // t
    n_tiles = width // LANES
    tasks = [(qb, kb, qb, int(kb == qb)) for qb in range(n_q) for kb in range(qb + 1)]
    table = _task_table(tasks, n_q)
    pp, hh = FOX_PAIRS, HEADS_PER_TILE
    seq_spec = pl.BlockSpec((None, s, pp * LANES), lambda b, hp: (b, 0, hp))
    f_tiles = fcum.reshape(bsz, n_tiles, hh, s)
    return pl.pallas_call(
        _fox_kernel,
        grid=(bsz, n_tiles // pp),
        in_specs=[pl.BlockSpec(memory_space=pltpu.SMEM), seq_spec, seq_spec, seq_spec,
                  pl.BlockSpec((None, pp, hh, s), lambda b, hp: (b, hp, 0, 0)),
                  seq_spec, pl.BlockSpec((1, pp * LANES), lambda b, hp: (0, hp))],
        out_specs=seq_spec,
        out_shape=jax.ShapeDtypeStruct((bsz, s, width), BF16),
        scratch_shapes=[pltpu.VMEM((pp, n_q, 2 * t, 2 * LANES), BF16),
                        pltpu.VMEM((pp, hh, s, LANES), BF16),
                        pltpu.VMEM((2, t, t), F32),
                        pltpu.VMEM((pp, n_q + 1, t, 2 * LANES), F32),
                        pltpu.VMEM((pp, n_q + 1, hh, t, LANES), F32),
                        pltpu.VMEM((pp, hh, t, t), F32),
                        pltpu.VMEM((pp, t, hh * t), BF16),
                        pltpu.VMEM((pp, hh, t, LANES), F32),
                        pltpu.VMEM((pp, hh, t, LANES), F32)],
        compiler_params=pltpu.CompilerParams(
            dimension_semantics=("arbitrary", "arbitrary"), vmem_limit_bytes=VMEM_LIMIT_BYTES),
        name="fox_attn",
    )(jnp.asarray(table), q, k, v, f_tiles, gate, g.reshape(1, width))


def kernel(x, c, w_ada, b_ada, g_ffn1, w_ffn1_up, w_ffn1_down, g_mix, w_in, b_forget, g_fox_q,
           g_fox_k, g_sb_out, g_fox_out, w_out, g_ffn2, w_ffn2_up, w_ffn2_down, g_final):
    bsz, s, d = x.shape
    depth = w_ada.shape[0]
    t = SB_T

    head_of = jnp.arange(FOX_WIDTH) // HEAD_DIM
    grp = jnp.where(head_of[:, None] == head_of[None, :], 1.0 / HEAD_DIM, 0.0).astype(BF16)
    key = jnp.arange(t)
    tri = (key[:, None] >= key[None, :]).astype(BF16)

    n_main = 3 * SB_WIDTH + 4 * FOX_WIDTH
    for l in range(depth):
        ada = _ada(c, w_ada[l], b_ada[l]).reshape(bsz, N_SUBLAYERS, 3, 1, d)
        shift, scale, gate = ada[:, :, 0], ada[:, :, 1], ada[:, :, 2]

        x = _ffn(x, shift[:, 0], scale[:, 0], gate[:, 0], g_ffn1[l],
                 w_ffn1_up[l], w_ffn1_down[l])

        w_f = jnp.pad(w_in[l][:, n_main:], ((0, 0), (0, LANES - FOX_HEADS))).astype(BF16)
        sbq, sbk, sbv, fxq, fxk, fxv, fxg, fcum = _in_proj(
            x, shift[:, 1], scale[:, 1], g_mix[l], w_in[l], w_f,
            b_forget[l].reshape(FOX_HEADS, 1),
            jnp.tile(g_fox_q[l], FOX_HEADS).reshape(1, FOX_WIDTH),
            jnp.tile(g_fox_k[l], FOX_HEADS).reshape(1, FOX_WIDTH), grp)

        o_sb = _sb_attention(sbq, sbk, sbv, g_sb_out[l], tri)
        o_fx = _fox_attention(fxq, fxk, fxv, fcum, fxg, g_fox_out[l])

        x = _ffn(x, shift[:, 2], scale[:, 2], gate[:, 2], g_ffn2[l],
                 w_ffn2_up[l], w_ffn2_down[l],
                 mix=(o_sb, o_fx, w_out[l], gate[:, 1]),
                 g_final=g_final if l == depth - 1 else None)
    return x
```

```python
import functools

import numpy as np

import jax
import jax.numpy as jnp
from jax import lax
from jax.experimental import pallas as pl
from jax.experimental.pallas import tpu as pltpu

HEAD_DIM = 64
SB_HEADS = 8
FOX_HEADS = 8
SB_WIDTH = SB_HEADS * HEAD_DIM
FOX_WIDTH = FOX_HEADS * HEAD_DIM
N_SUBLAYERS = 3
EPS = 1e-6
QK_SCALE = HEAD_DIM ** -0.5

LANES = 128
HEADS_PER_TILE = LANES // HEAD_DIM
VMEM_LIMIT_BYTES = 56 * 1024 * 1024

ADA_TN = 1152
FFN_TM = 512
FFN_HC = 256
PROJ_TM = 512
SB_T = 256
FOX_T = 512
SB_PAIRS = 2
FOX_PAIRS = 2
PIPE_LAG = 2

LOG2E = 1.4426950408889634
F32_EXP2_ZERO_BELOW = -150.0
MASKED = -1e30

BF16 = jnp.bfloat16
F32 = jnp.float32


def _dot(a, b):
    return jnp.dot(a, b, preferred_element_type=F32)


def _dot_nt(a, b):
    return lax.dot_general(a, b, (((1,), (1,)), ((), ())), preferred_element_type=F32)


def _sigmoid(x):
    return 1.0 / (1.0 + jnp.exp(-x))


def _softplus(x):
    return jnp.maximum(x, 0.0) + jnp.log(1.0 + jnp.exp(-jnp.abs(x)))


def _modulate(x, g, shift, scale):
    ms = jnp.mean(x * x, axis=-1, keepdims=True)
    return (x * lax.rsqrt(ms + EPS)) * (g * (1.0 + scale)) + shift


def _const_spec(shape):
    return pl.BlockSpec(shape, lambda *_: (0,) * len(shape), pipeline_mode=pl.Buffered(1))


def _ada_kernel(c_ref, w_ref, b_ref, o_ref):
    c = c_ref[...]
    cond = (c * _sigmoid(c)).astype(BF16)
    o_ref[...] = _dot(cond, w_ref[...].astype(BF16)) + b_ref[...]


def _ada(c, w, b):
    bsz, d = c.shape
    n = w.shape[1]
    return pl.pallas_call(
        _ada_kernel,
        grid=(n // ADA_TN,),
        in_specs=[
            pl.BlockSpec((bsz, d), lambda j: (0, 0)),
            pl.BlockSpec((d, ADA_TN), lambda j: (0, j)),
            pl.BlockSpec((1, ADA_TN), lambda j: (0, j)),
        ],
        out_specs=pl.BlockSpec((bsz, ADA_TN), lambda j: (0, j)),
        out_shape=jax.ShapeDtypeStruct((bsz, n), F32),
        compiler_params=pltpu.CompilerParams(
            dimension_semantics=("arbitrary",), vmem_limit_bytes=VMEM_LIMIT_BYTES),
        name="ada",
    )(c, w, b.reshape(1, n))


def _ffn_kernel(*refs, pre_mix, final_norm):
    it = iter(refs)
    x_ref = next(it)
    if pre_mix:
        osb_ref, ofx_ref, wout_ref, gate_mix_ref = next(it), next(it), next(it), next(it)
    shift_ref, scale_ref, gate_ref, g_ref = next(it), next(it), next(it), next(it)
    wup_ref, wdown_ref = next(it), next(it)
    if final_norm:
        gf_ref = next(it)
    o_ref, h_scr, acc_scr = next(it), next(it), next(it)

    x = x_ref[...]
    if pre_mix:
        mix = (_dot(osb_ref[...], wout_ref[0:SB_WIDTH, :].astype(BF16))
               + _dot(ofx_ref[...], wout_ref[SB_WIDTH:SB_WIDTH + FOX_WIDTH, :].astype(BF16)))
        x = x + gate_mix_ref[...] * mix
    o_ref[...] = x
    h_scr[...] = _modulate(x, g_ref[...], shift_ref[...], scale_ref[...]).astype(BF16)
    acc_scr[...] = jnp.zeros_like(acc_scr)

    hidden = wdown_ref.shape[0]
    for lo in range(0, hidden, FFN_HC):
        h = h_scr[...]
        gt = _dot(h, wup_ref[:, lo:lo + FFN_HC].astype(BF16))
        up = _dot(h, wup_ref[:, hidden + lo:hidden + lo + FFN_HC].astype(BF16))
        a = (gt * _sigmoid(gt) * up).astype(BF16)
        acc_scr[...] += _dot(a, wdown_ref[lo:lo + FFN_HC, :].astype(BF16))
    y = o_ref[...] + (0.5 * gate_ref[...]) * acc_scr[...]
    if final_norm:
        ms = jnp.mean(y * y, axis=-1, keepdims=True)
        y = (y * lax.rsqrt(ms + EPS)) * gf_ref[...]
    o_ref[...] = y


def _ffn(x, shift, scale, gate, g, w_up, w_down, mix=None, g_final=None):
    bsz, s, d = x.shape
    assert w_down.shape[0] % FFN_HC == 0
    tm = FFN_TM
    tok = pl.BlockSpec((None, tm, d), lambda b, i: (b, i, 0))
    per_batch = pl.BlockSpec((None, 1, d), lambda b, i: (b, 0, 0))
    args, specs = [x], [tok]
    if mix is not None:
        o_sb, o_fx, w_out, gate_mix = mix
        args += [o_sb, o_fx, w_out, gate_mix]
        specs += [pl.BlockSpec((None, tm, SB_WIDTH), lambda b, i: (b, i, 0)),
                  pl.BlockSpec((None, tm, FOX_WIDTH), lambda b, i: (b, i, 0)),
                  _const_spec(w_out.shape), per_batch]
    args += [shift, scale, gate, g.reshape(1, d), w_up, w_down]
    specs += [per_batch, per_batch, per_batch, _const_spec((1, d)),
              _const_spec(w_up.shape), _const_spec(w_down.shape)]
    if g_final is not None:
        args.append(g_final.reshape(1, d))
        specs.append(_const_spec((1, d)))
    return pl.pallas_call(
        functools.partial(_ffn_kernel, pre_mix=mix is not None, final_norm=g_final is not None),
        grid=(bsz, s // tm),
        in_specs=specs,
        out_specs=tok,
        out_shape=jax.ShapeDtypeStruct((bsz, s, d), F32),
        scratch_shapes=[pltpu.VMEM((tm, d), BF16), pltpu.VMEM((tm, d), F32)],
        compiler_params=pltpu.CompilerParams(
            dimension_semantics=("arbitrary", "arbitrary"), vmem_limit_bytes=VMEM_LIMIT_BYTES),
        name="ffn_mix" if mix is not None else "ffn",
    )(*args)


def _lane_cumsum(y):
    n = y.shape[-1]
    lane = lax.broadcasted_iota(jnp.int32, y.shape, y.ndim - 1)
    shift = 1
    while shift < n:
        y = y + jnp.where(lane >= shift, pltpu.roll(y, shift, axis=y.ndim - 1), 0.0)
        shift *= 2
    return y


def _proj_kernel(x_ref, shift_ref, scale_ref, g_ref, w_ref, wf_ref, bf_ref, gq_ref, gk_ref,
                 grp_ref, sbq_ref, sbk_ref, sbv_ref, fxq_ref, fxk_ref, fxv_ref, fxg_ref,
                 fcum_ref, h_scr, carry_scr):
    @pl.when(pl.program_id(1) == 0)
    def _():
        carry_scr[...] = jnp.zeros_like(carry_scr)

    x = x_ref[...]
    h_scr[...] = _modulate(x, g_ref[...], shift_ref[...], scale_ref[...]).astype(BF16)
    h = h_scr[...]
    w = SB_WIDTH

    def proj(idx):
        return _dot(h, w_ref[:, idx * w:(idx + 1) * w].astype(BF16))

    def head_rms(p):
        ms = _dot((p * p).astype(BF16), grp_ref[...])
        return p * lax.rsqrt(ms + EPS)

    f = _dot(h, wf_ref[...])
    ft = jnp.transpose(f)[0:FOX_HEADS, :] + bf_ref[...]
    log_f = -_softplus(-ft)
    fcum = _lane_cumsum(log_f) + carry_scr[:, 0:1]
    fcum_ref[...] = fcum
    carry_scr[...] = jnp.broadcast_to(fcum[:, fcum.shape[1] - 1:], carry_scr.shape)

    fxq_ref[...] = (head_rms(proj(3)) * (gq_ref[...] * (QK_SCALE * LOG2E))).astype(BF16)
    fxk_ref[...] = (head_rms(proj(4)) * gk_ref[...]).astype(BF16)
    fxg_ref[...] = _sigmoid(proj(6)).astype(BF16)
    sbq_ref[...] = (proj(0) * (QK_SCALE * LOG2E)).astype(BF16)
    sbk_ref[...] = proj(1).astype(BF16)
    sbv_ref[...] = proj(2).astype(BF16)
    fxv_ref[...] = proj(5).astype(BF16)


def _in_proj(x, shift, scale, g, w_main, w_f, b_f, gq, gk, grp):
    bsz, s, d = x.shape
    tm = PROJ_TM
    tok = pl.BlockSpec((None, tm, d), lambda b, i: (b, i, 0))
    per_batch = pl.BlockSpec((None, 1, d), lambda b, i: (b, 0, 0))
    head_out = pl.BlockSpec((None, tm, SB_WIDTH), lambda b, i: (b, i, 0))
    head_shape = jax.ShapeDtypeStruct((bsz, s, SB_WIDTH), BF16)
    return pl.pallas_call(
        _proj_kernel,
        grid=(bsz, s // tm),
        in_specs=[tok, per_batch, per_batch, _const_spec((1, d)), _const_spec(w_main.shape),
                  _const_spec(w_f.shape), _const_spec(b_f.shape), _const_spec(gq.shape),
                  _const_spec(gk.shape), _const_spec(grp.shape)],
        out_specs=[head_out] * 7 + [pl.BlockSpec((None, FOX_HEADS, tm), lambda b, i: (b, 0, i))],
        out_shape=[head_shape] * 7 + [jax.ShapeDtypeStruct((bsz, FOX_HEADS, s), F32)],
        scratch_shapes=[pltpu.VMEM((tm, d), BF16), pltpu.VMEM((FOX_HEADS, LANES), F32)],
        compiler_params=pltpu.CompilerParams(
            dimension_semantics=("arbitrary", "arbitrary"), vmem_limit_bytes=VMEM_LIMIT_BYTES),
        name="in_proj",
    )(x, shift, scale, g.reshape(1, d), w_main, w_f, b_f, gq, gk, grp)


def _split_heads(x):
    lane = lax.broadcasted_iota(jnp.int32, x.shape, 1)
    zero = jnp.zeros_like(x)
    return jnp.where(lane < HEAD_DIM, x, zero), jnp.where(lane >= HEAD_DIM, x, zero)


def _pair_lanes(pp):
    return slice(pp * LANES, (pp + 1) * LANES)


def _fill_vexp(v_ref, lanes, vexp_ref, n_blocks, t):
    def body(j, carry):
        v0, v1 = _split_heads(v_ref[pl.ds(pl.multiple_of(j * t, t), t), lanes])
        vexp_ref[j, 0:t, 0:LANES] = v0
        vexp_ref[j, t:2 * t, 0:LANES] = v1
        return carry

    lax.fori_loop(0, n_blocks, body, 0)


def _fill_vexp_ones(vexp_ref, n_blocks, t):
    def body(j, carry):
        lane = lax.broadcasted_iota(jnp.int32, (t, LANES), 1)
        vexp_ref[j, 0:t, LANES:2 * LANES] = (lane < HEAD_DIM).astype(F32).astype(vexp_ref.dtype)
        vexp_ref[j, t:2 * t, LANES:2 * LANES] = (lane >= HEAD_DIM).astype(F32).astype(
            vexp_ref.dtype)
        return carry

    lax.fori_loop(0, n_blocks, body, 0)


def _per_head(lane, a0, a1):
    return jnp.where(lane < HEAD_DIM, a0, a1)


def _lane_tile(a, width):
    return jnp.concatenate([a] * (width // LANES), axis=1)


def _head_rmsnorm(o, lane):
    sq = o * o
    ss0 = jnp.sum(jnp.where(lane < HEAD_DIM, sq, 0.0), axis=1, keepdims=True)
    ss1 = jnp.sum(jnp.where(lane >= HEAD_DIM, sq, 0.0), axis=1, keepdims=True)
    inv = _per_head(lane, lax.rsqrt(ss0 * (1.0 / HEAD_DIM) + EPS),
                    lax.rsqrt(ss1 * (1.0 / HEAD_DIM) + EPS))
    return o * inv


def _task_table(tasks, n_q):
    seen = set()
    rows = []
    for q_blk, k_blk, slot, mask_idx in tasks:
        rows.append((q_blk, k_blk, slot, mask_idx, int(slot not in seen)))
        seen.add(slot)
    idle = [(0, 0, n_q, 0, 0)] * PIPE_LAG
    return np.asarray(idle + rows + idle, dtype=np.int32).T.copy()


def _fill_qexp(q_ref, lanes, qexp_ref, n_blocks, t):
    def body(j, carry):
        rows = pl.ds(pl.multiple_of(j * t, t), t)
        q0, q1 = _split_heads(q_ref[rows, lanes])
        qexp_ref[0, rows, :] = q0
        qexp_ref[1, rows, :] = q1
        return carry

    lax.fori_loop(0, n_blocks, body, 0)


def _first_grid_step():
    return jnp.logical_and(pl.program_id(0) == 0, pl.program_id(1) == 0)


def _causal_bias(t, strict):
    row = lax.broadcasted_iota(jnp.int32, (t, t), 0)
    col = lax.broadcasted_iota(jnp.int32, (t, t), 1)
    keep = col < row if strict else col <= row
    return jnp.where(keep, 0.0, MASKED).astype(F32)


def _sb_kernel(tab_ref, q_ref, k_ref, v_ref, g_ref, tri_ref, o_ref, vexp_scr, qexp_scr, bias_scr,
               acc_scr, carry_scr, z_scr, zc_scr, suf_scr, sp_scr, w_scr, l_scr):
    t = SB_T
    n_q = q_ref.shape[0] // t
    n_iter = tab_ref.shape[1] - PIPE_LAG
    pairs = range(q_ref.shape[1] // LANES)

    for pp in pairs:
        _fill_vexp(v_ref, _pair_lanes(pp), vexp_scr.at[pp], n_q, t)
        _fill_qexp(q_ref, _pair_lanes(pp), qexp_scr.at[pp], n_q, t)
    @pl.when(_first_grid_step())
    def _():
        acc_scr[...] = jnp.zeros_like(acc_scr)
        carry_scr[...] = jnp.zeros_like(carry_scr)
        bias_scr[0] = jnp.zeros((t, t), F32)
        bias_scr[1] = _causal_bias(t, strict=True)
        z_scr[...] = jnp.zeros_like(z_scr)
        zc_scr[...] = jnp.zeros_like(zc_scr)
        suf_scr[...] = jnp.zeros_like(suf_scr)
        l_scr[...] = jnp.zeros_like(l_scr)

    def score_stage(pp, q_blk, k_blk, mask_idx):
        rows = pl.ds(pl.multiple_of(q_blk * t, t), t)
        k = k_ref[pl.ds(pl.multiple_of(k_blk * t, t), t), _pair_lanes(pp)]
        bias = bias_scr[mask_idx]
        for hd in range(HEADS_PER_TILE):
            z = _dot_nt(qexp_scr[pp, hd, rows, :], k) + bias
            z_scr[pp, hd] = z
            l_scr[pp, hd] = jnp.log2(1.0 + jnp.exp2(-jnp.abs(z)))

    def keep_stage(pp, slot, first):
        for hd in range(HEADS_PER_TILE):
            carry = jnp.where(first > 0, 0.0, carry_scr[pp, slot, hd])
            z = z_scr[pp, hd]
            sp_scr[pp, hd] = (jnp.maximum(z, 0.0) + l_scr[pp, hd]).astype(BF16)
            zc_scr[pp, hd] = z - _lane_tile(carry, t)
            suf = _dot(sp_scr[pp, hd], tri_ref[...])
            suf_scr[pp, hd] = suf
            carry_scr[pp, slot, hd] = carry + suf[:, 0:1]

    def value_stage(pp, k_blk, slot, first):
        for hd in range(HEADS_PER_TILE):
            w_scr[pp, :, hd * t:(hd + 1) * t] = jnp.exp2(
                zc_scr[pp, hd] - suf_scr[pp, hd]).astype(BF16)
        acc = jnp.where(first > 0, 0.0, acc_scr[pp, slot])
        acc_scr[pp, slot] = acc + _dot(w_scr[pp], vexp_scr[pp, k_blk])

    def body(i, carry):
        for pp in pairs:
            value_stage(pp, tab_ref[1, i], tab_ref[2, i], tab_ref[4, i])
        for pp in pairs:
            keep_stage(pp, tab_ref[2, i + 1], tab_ref[4, i + 1])
        for pp in pairs:
            score_stage(pp, tab_ref[0, i + 2], tab_ref[1, i + 2], tab_ref[3, i + 2])
        return carry

    lax.fori_loop(0, n_iter, body, 0)

    def tail(pp, q_blk):
        def live():
            return (jnp.min(carry_scr[pp, q_blk]) <= -F32_EXP2_ZERO_BELOW).astype(jnp.int32)

        def cond(state):
            k_blk, go = state
            return jnp.logical_and(k_blk >= 0, go > 0)

        def step(state):
            k_blk, _ = state
            score_stage(pp, q_blk, k_blk, 0)
            keep_stage(pp, q_blk, 0)
            value_stage(pp, k_blk, q_blk, 0)
            return k_blk - 1, live()

        lax.while_loop(cond, step, (q_blk - 2, live()))

    lane = lax.broadcasted_iota(jnp.int32, (t, LANES), 1)
    for pp in pairs:
        def tail_body(q_blk, carry, pp=pp):
            tail(pp, q_blk)
            return carry

        @pl.when(jnp.min(carry_scr[pp, 2:n_q]) <= -F32_EXP2_ZERO_BELOW)
        def _():
            lax.fori_loop(2, n_q, tail_body, 0)

        lanes = _pair_lanes(pp)
        for q_blk in range(n_q):
            o = _head_rmsnorm(acc_scr[pp, q_blk], lane) * g_ref[:, lanes]
            o_ref[q_blk * t:(q_blk + 1) * t, lanes] = o.astype(o_ref.dtype)


def _sb_attention(q, k, v, g, tri):
    bsz, s, width = q.shape
    t = SB_T
    n_q = s // t
    n_tiles = width // LANES
    tasks = []
    for qb in range(n_q):
        tasks.append((qb, qb, qb, 1))
        if qb >= 1:
            tasks.append((qb, qb - 1, qb, 0))
    table = _task_table(tasks, n_q)
    pp, hh = SB_PAIRS, HEADS_PER_TILE
    seq_spec = pl.BlockSpec((None, s, pp * LANES), lambda b, hp: (b, 0, hp))
    return pl.pallas_call(
        _sb_kernel,
        grid=(bsz, n_tiles // pp),
        in_specs=[pl.BlockSpec(memory_space=pltpu.SMEM), seq_spec, seq_spec, seq_spec,
                  pl.BlockSpec((1, pp * LANES), lambda b, hp: (0, hp)), _const_spec(tri.shape)],
        out_specs=seq_spec,
        out_shape=jax.ShapeDtypeStruct((bsz, s, width), BF16),
        scratch_shapes=[pltpu.VMEM((pp, n_q, 2 * t, LANES), BF16),
                        pltpu.VMEM((pp, hh, s, LANES), BF16),
                        pltpu.VMEM((2, t, t), F32),
                        pltpu.VMEM((pp, n_q + 1, t, LANES), F32),
                        pltpu.VMEM((pp, n_q + 1, hh, t, LANES), F32),
                        pltpu.VMEM((pp, hh, t, t), F32),
                        pltpu.VMEM((pp, hh, t, t), F32),
                        pltpu.VMEM((pp, hh, t, t), F32),
                        pltpu.VMEM((pp, hh, t, t), BF16),
                        pltpu.VMEM((pp, t, hh * t), BF16),
                        pltpu.VMEM((pp, hh, t, t), F32)],
        compiler_params=pltpu.CompilerParams(
            dimension_semantics=("arbitrary", "arbitrary"), vmem_limit_bytes=VMEM_LIMIT_BYTES),
        name="sb_attn",
    )(jnp.asarray(table), q, k, v, g.reshape(1, width), tri)


def _fox_kernel(tab_ref, q_ref, k_ref, v_ref, f_ref, gate_ref, g_ref, o_ref, vexp_scr, qexp_scr,
                bias_scr, acc_scr, m_scr, s_scr, p_scr, alpha_scr, bmax_scr):
    t = FOX_T
    n_q = q_ref.shape[0] // t
    n_iter = tab_ref.shape[1] - PIPE_LAG
    pairs = range(q_ref.shape[1] // LANES)
    lane = lax.broadcasted_iota(jnp.int32, (t, LANES), 1)

    for pp in pairs:
        _fill_vexp(v_ref, _pair_lanes(pp), vexp_scr.at[pp], n_q, t)
        _fill_qexp(q_ref, _pair_lanes(pp), qexp_scr.at[pp], n_q, t)
    @pl.when(_first_grid_step())
    def _():
        acc_scr[...] = jnp.zeros_like(acc_scr)
        m_scr[...] = jnp.full_like(m_scr, -jnp.inf)
        for pp in pairs:
            _fill_vexp_ones(vexp_scr.at[pp], n_q, t)
        bias_scr[0] = jnp.zeros((t, t), F32)
        bias_scr[1] = _causal_bias(t, strict=False)
        s_scr[...] = jnp.zeros_like(s_scr)
        bmax_scr[...] = jnp.zeros_like(bmax_scr)
        p_scr[...] = jnp.zeros_like(p_scr)
        alpha_scr[...] = jnp.zeros_like(alpha_scr)

    def score_stage(pp, q_blk, k_blk, mask_idx):
        rows = pl.ds(pl.multiple_of(q_blk * t, t), t)
        cols = pl.ds(pl.multiple_of(k_blk * t, t), t)
        k = k_ref[cols, _pair_lanes(pp)]
        f_keys = f_ref[pp, :, cols] * LOG2E
        bias = bias_scr[mask_idx]
        for hd in range(HEADS_PER_TILE):
            sc = _dot_nt(qexp_scr[pp, hd, rows, :], k) - f_keys[hd:hd + 1, :] + bias
            s_scr[pp, hd] = sc
            bmax_scr[pp, hd] = jnp.broadcast_to(jnp.max(sc, axis=1, keepdims=True), (t, LANES))

    def softmax_stage(pp, slot, first):
        for hd in range(HEADS_PER_TILE):
            m_old = jnp.where(first > 0, -jnp.inf, m_scr[pp, slot, hd])
            m_new = jnp.maximum(m_old, bmax_scr[pp, hd])
            alpha_scr[pp, hd] = jnp.exp2(m_old - m_new)
            m_scr[pp, slot, hd] = m_new
            p_scr[pp, :, hd * t:(hd + 1) * t] = jnp.exp2(
                s_scr[pp, hd] - _lane_tile(m_new, t)).astype(BF16)

    def value_stage(pp, k_blk, slot):
        pv = _dot(p_scr[pp], vexp_scr[pp, k_blk])
        alpha = _per_head(lane, alpha_scr[pp, 0], alpha_scr[pp, 1])
        acc_scr[pp, slot] = acc_scr[pp, slot] * _lane_tile(alpha, 2 * LANES) + pv

    def body(i, carry):
        for pp in pairs:
            value_stage(pp, tab_ref[1, i], tab_ref[2, i])
        for pp in pairs:
            softmax_stage(pp, tab_ref[2, i + 1], tab_ref[4, i + 1])
        for pp in pairs:
            score_stage(pp, tab_ref[0, i + 2], tab_ref[1, i + 2], tab_ref[3, i + 2])
        return carry

    lax.fori_loop(0, n_iter, body, 0)

    for pp in pairs:
        lanes = _pair_lanes(pp)
        for q_blk in range(n_q):
            rows = slice(q_blk * t, (q_blk + 1) * t)
            acc = acc_scr[pp, q_blk]
            o = acc[:, 0:LANES] / acc[:, LANES:2 * LANES]
            o = _head_rmsnorm(o, lane) * g_ref[:, lanes] * gate_ref[rows, lanes].astype(F32)
            o_ref[rows, lanes] = o.astype(o_ref.dtype)


def _fox_attention(q, k, v, fcum, gate, g):
    bsz, s, width = q.shape
    t = FOX_T
    n_q = s // t
    n_tiles = width // LANES
    tasks = [(qb, kb, qb, int(kb == qb)) for qb in range(n_q) for kb in range(qb + 1)]
    table = _task_table(tasks, n_q)
    pp, hh = FOX_PAIRS, HEADS_PER_TILE
    seq_spec = pl.BlockSpec((None, s, pp * LANES), lambda b, hp: (b, 0, hp))
    f_tiles = fcum.reshape(bsz, n_tiles, hh, s)
    return pl.pallas_call(
        _fox_kernel,
        grid=(bsz, n_tiles // pp),
        in_specs=[pl.BlockSpec(memory_space=pltpu.SMEM), seq_spec, seq_spec, seq_spec,
                  pl.BlockSpec((None, pp, hh, s), lambda b, hp: (b, hp, 0, 0)),
                  seq_spec, pl.BlockSpec((1, pp * LANES), lambda b, hp: (0, hp))],
        out_specs=seq_spec,
        out_shape=jax.ShapeDtypeStruct((bsz, s, width), BF16),
        scratch_shapes=[pltpu.VMEM((pp, n_q, 2 * t, 2 * LANES), BF16),
                        pltpu.VMEM((pp, hh, s, LANES), BF16),
                        pltpu.VMEM((2, t, t), F32),
                        pltpu.VMEM((pp, n_q + 1, t, 2 * LANES), F32),
                        pltpu.VMEM((pp, n_q + 1, hh, t, LANES), F32),
                        pltpu.VMEM((pp, hh, t, t), F32),
                        pltpu.VMEM((pp, t, hh * t), BF16),
                        pltpu.VMEM((pp, hh, t, LANES), F32),
                        pltpu.VMEM((pp, hh, t, LANES), F32)],
        compiler_params=pltpu.CompilerParams(
            dimension_semantics=("arbitrary", "arbitrary"), vmem_limit_bytes=VMEM_LIMIT_BYTES),
        name="fox_attn",
    )(jnp.asarray(table), q, k, v, f_tiles, gate, g.reshape(1, width))


def kernel(x, c, w_ada, b_ada, g_ffn1, w_ffn1_up, w_ffn1_down, g_mix, w_in, b_forget, g_fox_q,
           g_fox_k, g_sb_out, g_fox_out, w_out, g_ffn2, w_ffn2_up, w_ffn2_down, g_final):
    bsz, s, d = x.shape
    depth = w_ada.shape[0]
    t = SB_T

    head_of = jnp.arange(FOX_WIDTH) // HEAD_DIM
    grp = jnp.where(head_of[:, None] == head_of[None, :], 1.0 / HEAD_DIM, 0.0).astype(BF16)
    key = jnp.arange(t)
    tri = (key[:, None] >= key[None, :]).astype(BF16)

    n_main = 3 * SB_WIDTH + 4 * FOX_WIDTH
    for l in range(depth):
        ada = _ada(c, w_ada[l], b_ada[l]).reshape(bsz, N_SUBLAYERS, 3, 1, d)
        shift, scale, gate = ada[:, :, 0], ada[:, :, 1], ada[:, :, 2]

        x = _ffn(x, shift[:, 0], scale[:, 0], gate[:, 0], g_ffn1[l],
                 w_ffn1_up[l], w_ffn1_down[l])

        w_f = jnp.pad(w_in[l][:, n_main:], ((0, 0), (0, LANES - FOX_HEADS))).astype(BF16)
        sbq, sbk, sbv, fxq, fxk, fxv, fxg, fcum = _in_proj(
            x, shift[:, 1], scale[:, 1], g_mix[l], w_in[l], w_f,
            b_forget[l].reshape(FOX_HEADS, 1),
            jnp.tile(g_fox_q[l], FOX_HEADS).reshape(1, FOX_WIDTH),
            jnp.tile(g_fox_k[l], FOX_HEADS).reshape(1, FOX_WIDTH), grp)

        o_sb = _sb_attention(sbq, sbk, sbv, g_sb_out[l], tri)
        o_fx = _fox_attention(fxq, fxk, fxv, fcum, fxg, g_fox_out[l])

        x = _ffn(x, shift[:, 2], scale[:, 2], gate[:, 2], g_ffn2[l],
                 w_ffn2_up[l], w_ffn2_down[l],
                 mix=(o_sb, o_fx, w_out[l], gate[:, 1]),
                 g_final=g_final if l == depth - 1 else None)
    return x
```

```python
import functools

import numpy as np

import jax
import jax.numpy as jnp
from jax import lax
from jax.experimental import pallas as pl
from jax.experimental.pallas import tpu as pltpu

HEAD_DIM = 64
SB_HEADS = 8
FOX_HEADS = 8
SB_WIDTH = SB_HEADS * HEAD_DIM
FOX_WIDTH = FOX_HEADS * HEAD_DIM
N_SUBLAYERS = 3
EPS = 1e-6
QK_SCALE = HEAD_DIM ** -0.5

LANES = 128
HEADS_PER_TILE = LANES // HEAD_DIM
VMEM_LIMIT_BYTES = 56 * 1024 * 1024

ADA_TN = 1152
FFN_TM = 512
FFN_HC = 256
PROJ_TM = 512
SB_T = 256
FOX_T = 512
SB_PAIRS = 2
FOX_PAIRS = 2

LOG2E = 1.4426950408889634
F32_EXP2_ZERO_BELOW = -150.0
MASKED = -1e30

BF16 = jnp.bfloat16
F32 = jnp.float32


def _dot(a, b):
    return jnp.dot(a, b, preferred_element_type=F32)


def _dot_nt(a, b):
    return lax.dot_general(a, b, (((1,), (1,)), ((), ())), preferred_element_type=F32)


def _sigmoid(x):
    return 1.0 / (1.0 + jnp.exp(-x))


def _softplus(x):
    return jnp.maximum(x, 0.0) + jnp.log(1.0 + jnp.exp(-jnp.abs(x)))


def _modulate(x, g, shift, scale):
    ms = jnp.mean(x * x, axis=-1, keepdims=True)
    return (x * lax.rsqrt(ms + EPS)) * (g * (1.0 + scale)) + shift


def _const_spec(shape):
    return pl.BlockSpec(shape, lambda *_: (0,) * len(shape), pipeline_mode=pl.Buffered(1))


def _ada_kernel(c_ref, w_ref, b_ref, o_ref):
    c = c_ref[...]
    cond = (c * _sigmoid(c)).astype(BF16)
    o_ref[...] = _dot(cond, w_ref[...].astype(BF16)) + b_ref[...]


def _ada(c, w, b):
    bsz, d = c.shape
    n = w.shape[1]
    return pl.pallas_call(
        _ada_kernel,
        grid=(n // ADA_TN,),
        in_specs=[
            pl.BlockSpec((bsz, d), lambda j: (0, 0)),
            pl.BlockSpec((d, ADA_TN), lambda j: (0, j)),
            pl.BlockSpec((1, ADA_TN), lambda j: (0, j)),
        ],
        out_specs=pl.BlockSpec((bsz, ADA_TN), lambda j: (0, j)),
        out_shape=jax.ShapeDtypeStruct((bsz, n), F32),
        compiler_params=pltpu.CompilerParams(
            dimension_semantics=("arbitrary",), vmem_limit_bytes=VMEM_LIMIT_BYTES),
        name="ada",
    )(c, w, b.reshape(1, n))


def _ffn_kernel(*refs, pre_mix, final_norm):
    it = iter(refs)
    x_ref = next(it)
    if pre_mix:
        osb_ref, ofx_ref, wout_ref, gate_mix_ref = next(it), next(it), next(it), next(it)
    shift_ref, scale_ref, gate_ref, g_ref = next(it), next(it), next(it), next(it)
    wup_ref, wdown_ref = next(it), next(it)
    if final_norm:
        gf_ref = next(it)
    o_ref, h_scr, acc_scr = next(it), next(it), next(it)

    x = x_ref[...]
    if pre_mix:
        mix = (_dot(osb_ref[...], wout_ref[0:SB_WIDTH, :].astype(BF16))
               + _dot(ofx_ref[...], wout_ref[SB_WIDTH:SB_WIDTH + FOX_WIDTH, :].astype(BF16)))
        x = x + gate_mix_ref[...] * mix
    o_ref[...] = x
    h_scr[...] = _modulate(x, g_ref[...], shift_ref[...], scale_ref[...]).astype(BF16)
    acc_scr[...] = jnp.zeros_like(acc_scr)

    hidden = wdown_ref.shape[0]
    for lo in range(0, hidden, FFN_HC):
        h = h_scr[...]
        gt = _dot(h, wup_ref[:, lo:lo + FFN_HC].astype(BF16))
        up = _dot(h, wup_ref[:, hidden + lo:hidden + lo + FFN_HC].astype(BF16))
        a = (gt * _sigmoid(gt) * up).astype(BF16)
        acc_scr[...] += _dot(a, wdown_ref[lo:lo + FFN_HC, :].astype(BF16))
    y = o_ref[...] + (0.5 * gate_ref[...]) * acc_scr[...]
    if final_norm:
        ms = jnp.mean(y * y, axis=-1, keepdims=True)
        y = (y * lax.rsqrt(ms + EPS)) * gf_ref[...]
    o_ref[...] = y


def _ffn(x, shift, scale, gate, g, w_up, w_down, mix=None, g_final=None):
    bsz, s, d = x.shape
    assert w_down.shape[0] % FFN_HC == 0
    tm = FFN_TM
    tok = pl.BlockSpec((None, tm, d), lambda b, i: (b, i, 0))
    per_batch = pl.BlockSpec((None, 1, d), lambda b, i: (b, 0, 0))
    args, specs = [x], [tok]
    if mix is not None:
        o_sb, o_fx, w_out, gate_mix = mix
        args += [o_sb, o_fx, w_out, gate_mix]
        specs += [pl.BlockSpec((None, tm, SB_WIDTH), lambda b, i: (b, i, 0)),
                  pl.BlockSpec((None, tm, FOX_WIDTH), lambda b, i: (b, i, 0)),
                  _const_spec(w_out.shape), per_batch]
    args += [shift, scale, gate, g.reshape(1, d), w_up, w_down]
    specs += [per_batch, per_batch, per_batch, _const_spec((1, d)),
              _const_spec(w_up.shape), _const_spec(w_down.shape)]
    if g_final is not None:
        args.append(g_final.reshape(1, d))
        specs.append(_const_spec((1, d)))
    return pl.pallas_call(
        functools.partial(_ffn_kernel, pre_mix=mix is not None, final_norm=g_final is not None),
        grid=(bsz, s // tm),
        in_specs=specs,
        out_specs=tok,
        out_shape=jax.ShapeDtypeStruct((bsz, s, d), F32),
        scratch_shapes=[pltpu.VMEM((tm, d), BF16), pltpu.VMEM((tm, d), F32)],
        compiler_params=pltpu.CompilerParams(
            dimension_semantics=("arbitrary", "arbitrary"), vmem_limit_bytes=VMEM_LIMIT_BYTES),
        name="ffn_mix" if mix is not None else "ffn",
    )(*args)


def _lane_cumsum(y):
    n = y.shape[-1]
    lane = lax.broadcasted_iota(jnp.int32, y.shape, y.ndim - 1)
    shift = 1
    while shift < n:
        y = y + jnp.where(lane >= shift, pltpu.roll(y, shift, axis=y.ndim - 1), 0.0)
        shift *= 2
    return y


def _proj_kernel(x_ref, shift_ref, scale_ref, g_ref, w_ref, wf_ref, bf_ref, gq_ref, gk_ref,
                 grp_ref, sbq_ref, sbk_ref, sbv_ref, fxq_ref, fxk_ref, fxv_ref, fxg_ref,
                 fcum_ref, h_scr, carry_scr):
    @pl.when(pl.program_id(1) == 0)
    def _():
        carry_scr[...] = jnp.zeros_like(carry_scr)

    x = x_ref[...]
    h_scr[...] = _modulate(x, g_ref[...], shift_ref[...], scale_ref[...]).astype(BF16)
    h = h_scr[...]
    w = SB_WIDTH

    def proj(idx):
        return _dot(h, w_ref[:, idx * w:(idx + 1) * w].astype(BF16))

    def head_rms(p):
        ms = _dot((p * p).astype(BF16), grp_ref[...])
        return p * lax.rsqrt(ms + EPS)

    f = _dot(h, wf_ref[...])
    ft = jnp.transpose(f)[0:FOX_HEADS, :] + bf_ref[...]
    log_f = -_softplus(-ft)
    fcum = _lane_cumsum(log_f) + carry_scr[:, 0:1]
    fcum_ref[...] = fcum
    carry_scr[...] = jnp.broadcast_to(fcum[:, fcum.shape[1] - 1:], carry_scr.shape)

    fxq_ref[...] = (head_rms(proj(3)) * (gq_ref[...] * (QK_SCALE * LOG2E))).astype(BF16)
    fxk_ref[...] = (head_rms(proj(4)) * gk_ref[...]).astype(BF16)
    fxg_ref[...] = _sigmoid(proj(6)).astype(BF16)
    sbq_ref[...] = (proj(0) * (QK_SCALE * LOG2E)).astype(BF16)
    sbk_ref[...] = proj(1).astype(BF16)
    sbv_ref[...] = proj(2).astype(BF16)
    fxv_ref[...] = proj(5).astype(BF16)


def _in_proj(x, shift, scale, g, w_main, w_f, b_f, gq, gk, grp):
    bsz, s, d = x.shape
    tm = PROJ_TM
    tok = pl.BlockSpec((None, tm, d), lambda b, i: (b, i, 0))
    per_batch = pl.BlockSpec((None, 1, d), lambda b, i: (b, 0, 0))
    head_out = pl.BlockSpec((None, tm, SB_WIDTH), lambda b, i: (b, i, 0))
    head_shape = jax.ShapeDtypeStruct((bsz, s, SB_WIDTH), BF16)
    return pl.pallas_call(
        _proj_kernel,
        grid=(bsz, s // tm),
        in_specs=[tok, per_batch, per_batch, _const_spec((1, d)), _const_spec(w_main.shape),
                  _const_spec(w_f.shape), _const_spec(b_f.shape), _const_spec(gq.shape),
                  _const_spec(gk.shape), _const_spec(grp.shape)],
        out_specs=[head_out] * 7 + [pl.BlockSpec((None, FOX_HEADS, tm), lambda b, i: (b, 0, i))],
        out_shape=[head_shape] * 7 + [jax.ShapeDtypeStruct((bsz, FOX_HEADS, s), F32)],
        scratch_shapes=[pltpu.VMEM((tm, d), BF16), pltpu.VMEM((FOX_HEADS, LANES), F32)],
        compiler_params=pltpu.CompilerParams(
            dimension_semantics=("arbitrary", "arbitrary"), vmem_limit_bytes=VMEM_LIMIT_BYTES),
        name="in_proj",
    )(x, shift, scale, g.reshape(1, d), w_main, w_f, b_f, gq, gk, grp)


def _split_heads(x):
    lane = lax.broadcasted_iota(jnp.int32, x.shape, 1)
    zero = jnp.zeros_like(x)
    return jnp.where(lane < HEAD_DIM, x, zero), jnp.where(lane >= HEAD_DIM, x, zero)


def _pair_lanes(pp):
    return slice(pp * LANES, (pp + 1) * LANES)


def _fill_vexp(v_ref, lanes, vexp_ref, n_blocks, t):
    def body(j, carry):
        v0, v1 = _split_heads(v_ref[pl.ds(pl.multiple_of(j * t, t), t), lanes])
        vexp_ref[j, 0:t, 0:LANES] = v0
        vexp_ref[j, t:2 * t, 0:LANES] = v1
        return carry

    lax.fori_loop(0, n_blocks, body, 0)


def _fill_vexp_ones(vexp_ref, n_blocks, t):
    def body(j, carry):
        lane = lax.broadcasted_iota(jnp.int32, (t, LANES), 1)
        vexp_ref[j, 0:t, LANES:2 * LANES] = (lane < HEAD_DIM).astype(F32).astype(vexp_ref.dtype)
        vexp_ref[j, t:2 * t, LANES:2 * LANES] = (lane >= HEAD_DIM).astype(F32).astype(
            vexp_ref.dtype)
        return carry

    lax.fori_loop(0, n_blocks, body, 0)


def _per_head(lane, a0, a1):
    return jnp.where(lane < HEAD_DIM, a0, a1)


def _lane_tile(a, width):
    return jnp.concatenate([a] * (width // LANES), axis=1)


def _head_rmsnorm(o, lane):
    sq = o * o
    ss0 = jnp.sum(jnp.where(lane < HEAD_DIM, sq, 0.0), axis=1, keepdims=True)
    ss1 = jnp.sum(jnp.where(lane >= HEAD_DIM, sq, 0.0), axis=1, keepdims=True)
    inv = _per_head(lane, lax.rsqrt(ss0 * (1.0 / HEAD_DIM) + EPS),
                    lax.rsqrt(ss1 * (1.0 / HEAD_DIM) + EPS))
    return o * inv


def _task_table(tasks, n_q):
    seen = set()
    rows = []
    for q_blk, k_blk, slot, mask_idx in tasks:
        rows.append((q_blk, k_blk, slot, mask_idx, int(slot not in seen)))
        seen.add(slot)
    return np.asarray(rows, dtype=np.int32).T.copy()


def _run_pipeline(n_tasks, score, middle, value):
    assert n_tasks >= 2
    score(0)
    middle(0)
    score(1)

    def body(j, carry):
        value(j - 2)
        middle(j - 1)
        score(j)
        return carry

    lax.fori_loop(2, n_tasks, body, 0)
    value(n_tasks - 2)
    middle(n_tasks - 1)
    value(n_tasks - 1)


def _fill_qexp(q_ref, lanes, qexp_ref, n_blocks, t):
    def body(j, carry):
        rows = pl.ds(pl.multiple_of(j * t, t), t)
        q0, q1 = _split_heads(q_ref[rows, lanes])
        qexp_ref[0, rows, :] = q0
        qexp_ref[1, rows, :] = q1
        return carry

    lax.fori_loop(0, n_blocks, body, 0)


def _first_grid_step():
    return jnp.logical_and(pl.program_id(0) == 0, pl.program_id(1) == 0)


def _causal_bias(t, strict):
    row = lax.broadcasted_iota(jnp.int32, (t, t), 0)
    col = lax.broadcasted_iota(jnp.int32, (t, t), 1)
    keep = col < row if strict else col <= row
    return jnp.where(keep, 0.0, MASKED).astype(F32)


def _sb_kernel(tab_ref, q_ref, k_ref, v_ref, g_ref, tri_ref, o_ref, vexp_scr, qexp_scr, bias_scr,
               acc_scr, carry_scr, z_scr, zc_scr, suf_scr, sp_scr, w_scr, l_scr):
    t = SB_T
    n_q = q_ref.shape[0] // t
    pairs = range(q_ref.shape[1] // LANES)

    for pp in pairs:
        _fill_vexp(v_ref, _pair_lanes(pp), vexp_scr.at[pp], n_q, t)
        _fill_qexp(q_ref, _pair_lanes(pp), qexp_scr.at[pp], n_q, t)

    @pl.when(_first_grid_step())
    def _():
        acc_scr[...] = jnp.zeros_like(acc_scr)
        carry_scr[...] = jnp.zeros_like(carry_scr)
        bias_scr[0] = jnp.zeros((t, t), F32)
        bias_scr[1] = _causal_bias(t, strict=True)

    def score_stage(pp, q_blk, k_blk, mask_idx):
        rows = pl.ds(pl.multiple_of(q_blk * t, t), t)
        k = k_ref[pl.ds(pl.multiple_of(k_blk * t, t), t), _pair_lanes(pp)]
        bias = bias_scr[mask_idx]
        for hd in range(HEADS_PER_TILE):
            z = _dot_nt(qexp_scr[pp, hd, rows, :], k) + bias
            z_scr[pp, hd] = z
            l_scr[pp, hd] = jnp.log2(1.0 + jnp.exp2(-jnp.abs(z)))

    def keep_stage(pp, slot, first):
        for hd in range(HEADS_PER_TILE):
            carry = jnp.where(first > 0, 0.0, carry_scr[pp, slot, hd])
            z = z_scr[pp, hd]
            sp_scr[pp, hd] = (jnp.maximum(z, 0.0) + l_scr[pp, hd]).astype(BF16)
            zc_scr[pp, hd] = z - _lane_tile(carry, t)
            suf = _dot(sp_scr[pp, hd], tri_ref[...])
            suf_scr[pp, hd] = suf
            carry_scr[pp, slot, hd] = carry + suf[:, 0:1]

    def value_stage(pp, k_blk, slot, first):
        for hd in range(HEADS_PER_TILE):
            w_scr[pp, :, hd * t:(hd + 1) * t] = jnp.exp2(
                zc_scr[pp, hd] - suf_scr[pp, hd]).astype(BF16)
        acc = jnp.where(first > 0, 0.0, acc_scr[pp, slot])
        acc_scr[pp, slot] = acc + _dot(w_scr[pp], vexp_scr[pp, k_blk])

    def score_all(j):
        for pp in pairs:
            score_stage(pp, tab_ref[0, j], tab_ref[1, j], tab_ref[3, j])

    def keep_all(j):
        for pp in pairs:
            keep_stage(pp, tab_ref[2, j], tab_ref[4, j])

    def value_all(j):
        for pp in pairs:
            value_stage(pp, tab_ref[1, j], tab_ref[2, j], tab_ref[4, j])

    _run_pipeline(tab_ref.shape[1], score_all, keep_all, value_all)

    def tail(pp, q_blk):
        def live():
            return (jnp.min(carry_scr[pp, q_blk]) <= -F32_EXP2_ZERO_BELOW).astype(jnp.int32)

        def cond(state):
            k_blk, go = state
            return jnp.logical_and(k_blk >= 0, go > 0)

        def step(state):
            k_blk, _ = state
            score_stage(pp, q_blk, k_blk, 0)
            keep_stage(pp, q_blk, 0)
            value_stage(pp, k_blk, q_blk, 0)
            return k_blk - 1, live()

        lax.while_loop(cond, step, (q_blk - 2, live()))

    lane = lax.broadcasted_iota(jnp.int32, (t, LANES), 1)
    for pp in pairs:
        def tail_body(q_blk, carry, pp=pp):
            tail(pp, q_blk)
            return carry

        @pl.when(jnp.min(carry_scr[pp, 2:n_q]) <= -F32_EXP2_ZERO_BELOW)
        def _():
            lax.fori_loop(2, n_q, tail_body, 0)

        lanes = _pair_lanes(pp)
        for q_blk in range(n_q):
            o = _head_rmsnorm(acc_scr[pp, q_blk], lane) * g_ref[:, lanes]
            o_ref[q_blk * t:(q_blk + 1) * t, lanes] = o.astype(o_ref.dtype)


def _sb_attention(q, k, v, g, tri):
    bsz, s, width = q.shape
    t = SB_T
    n_q = s // t
    n_tiles = width // LANES
    tasks = []
    for qb in range(n_q):
        tasks.append((qb, qb, qb, 1))
        if qb >= 1:
            tasks.append((qb, qb - 1, qb, 0))
    table = _task_table(tasks, n_q)
    pp, hh = SB_PAIRS, HEADS_PER_TILE
    seq_spec = pl.BlockSpec((None, s, pp * LANES), lambda b, hp: (b, 0, hp))
    return pl.pallas_call(
        _sb_kernel,
        grid=(bsz, n_tiles // pp),
        in_specs=[pl.BlockSpec(memory_space=pltpu.SMEM), seq_spec, seq_spec, seq_spec,
                  pl.BlockSpec((1, pp * LANES), lambda b, hp: (0, hp)), _const_spec(tri.shape)],
        out_specs=seq_spec,
        out_shape=jax.ShapeDtypeStruct((bsz, s, width), BF16),
        scratch_shapes=[pltpu.VMEM((pp, n_q, 2 * t, LANES), BF16),
                        pltpu.VMEM((pp, hh, s, LANES), BF16),
                        pltpu.VMEM((2, t, t), F32),
                        pltpu.VMEM((pp, n_q, t, LANES), F32),
                        pltpu.VMEM((pp, n_q, hh, t, LANES), F32),
                        pltpu.VMEM((pp, hh, t, t), F32),
                        pltpu.VMEM((pp, hh, t, t), F32),
                        pltpu.VMEM((pp, hh, t, t), F32),
                        pltpu.VMEM((pp, hh, t, t), BF16),
                        pltpu.VMEM((pp, t, hh * t), BF16),
                        pltpu.VMEM((pp, hh, t, t), F32)],
        compiler_params=pltpu.CompilerParams(
            dimension_semantics=("arbitrary", "arbitrary"), vmem_limit_bytes=VMEM_LIMIT_BYTES),
        name="sb_attn",
    )(jnp.asarray(table), q, k, v, g.reshape(1, width), tri)


def _fox_kernel(tab_ref, q_ref, k_ref, v_ref, f_ref, gate_ref, g_ref, o_ref, vexp_scr, qexp_scr,
                bias_scr, acc_scr, m_scr, s_scr, p_scr, alpha_scr, bmax_scr):
    t = FOX_T
    n_q = q_ref.shape[0] // t
    pairs = range(q_ref.shape[1] // LANES)
    lane = lax.broadcasted_iota(jnp.int32, (t, LANES), 1)

    for pp in pairs:
        _fill_vexp(v_ref, _pair_lanes(pp), vexp_scr.at[pp], n_q, t)
        _fill_qexp(q_ref, _pair_lanes(pp), qexp_scr.at[pp], n_q, t)

    @pl.when(_first_grid_step())
    def _():
        acc_scr[...] = jnp.zeros_like(acc_scr)
        m_scr[...] = jnp.full_like(m_scr, -jnp.inf)
        for pp in pairs:
            _fill_vexp_ones(vexp_scr.at[pp], n_q, t)
        bias_scr[0] = jnp.zeros((t, t), F32)
        bias_scr[1] = _causal_bias(t, strict=False)

    def score_stage(pp, q_blk, k_blk, mask_idx):
        rows = pl.ds(pl.multiple_of(q_blk * t, t), t)
        cols = pl.ds(pl.multiple_of(k_blk * t, t), t)
        k = k_ref[cols, _pair_lanes(pp)]
        f_keys = f_ref[pp, :, cols] * LOG2E
        bias = bias_scr[mask_idx]
        for hd in range(HEADS_PER_TILE):
            sc = _dot_nt(qexp_scr[pp, hd, rows, :], k) - f_keys[hd:hd + 1, :] + bias
            s_scr[pp, hd] = sc
            bmax_scr[pp, hd] = jnp.broadcast_to(jnp.max(sc, axis=1, keepdims=True), (t, LANES))

    def softmax_stage(pp, slot, first):
        for hd in range(HEADS_PER_TILE):
            m_old = jnp.where(first > 0, -jnp.inf, m_scr[pp, slot, hd])
            m_new = jnp.maximum(m_old, bmax_scr[pp, hd])
            alpha_scr[pp, hd] = jnp.exp2(m_old - m_new)
            m_scr[pp, slot, hd] = m_new
            p_scr[pp, :, hd * t:(hd + 1) * t] = jnp.exp2(
                s_scr[pp, hd] - _lane_tile(m_new, t)).astype(BF16)

    def value_stage(pp, k_blk, slot):
        pv = _dot(p_scr[pp], vexp_scr[pp, k_blk])
        alpha = _per_head(lane, alpha_scr[pp, 0], alpha_scr[pp, 1])
        acc_scr[pp, slot] = acc_scr[pp, slot] * _lane_tile(alpha, 2 * LANES) + pv

    def score_all(j):
        for pp in pairs:
            score_stage(pp, tab_ref[0, j], tab_ref[1, j], tab_ref[3, j])

    def softmax_all(j):
        for pp in pairs:
            softmax_stage(pp, tab_ref[2, j], tab_ref[4, j])

    def value_all(j):
        for pp in pairs:
            value_stage(pp, tab_ref[1, j], tab_ref[2, j])

    _run_pipeline(tab_ref.shape[1], score_all, softmax_all, value_all)

    for pp in pairs:
        lanes = _pair_lanes(pp)
        for q_blk in range(n_q):
            rows = slice(q_blk * t, (q_blk + 1) * t)
            acc = acc_scr[pp, q_blk]
            o = acc[:, 0:LANES] / acc[:, LANES:2 * LANES]
            o = _head_rmsnorm(o, lane) * g_ref[:, lanes] * gate_ref[rows, lanes].astype(F32)
            o_ref[rows, lanes] = o.astype(o_ref.dtype)


def _fox_attention(q, k, v, fcum, gate, g):
    bsz, s, width = q.shape
    t = FOX_T
    n_q = s // t
    n_tiles = width // LANES
    tasks = [(qb, kb, qb, int(kb == qb)) for qb in range(n_q) for kb in range(qb + 1)]
    table = _task_table(tasks, n_q)
    pp, hh = FOX_PAIRS, HEADS_PER_TILE
    seq_spec = pl.BlockSpec((None, s, pp * LANES), lambda b, hp: (b, 0, hp))
    f_tiles = fcum.reshape(bsz, n_tiles, hh, s)
    return pl.pallas_call(
        _fox_kernel,
        grid=(bsz, n_tiles // pp),
        in_specs=[pl.BlockSpec(memory_space=pltpu.SMEM), seq_spec, seq_spec, seq_spec,
                  pl.BlockSpec((None, pp, hh, s), lambda b, hp: (b, hp, 0, 0)),
                  seq_spec, pl.BlockSpec((1, pp * LANES), lambda b, hp: (0, hp))],
        out_specs=seq_spec,
        out_shape=jax.ShapeDtypeStruct((bsz, s, width), BF16),
        scratch_shapes=[pltpu.VMEM((pp, n_q, 2 * t, 2 * LANES), BF16),
                        pltpu.VMEM((pp, hh, s, LANES), BF16),
                        pltpu.VMEM((2, t, t), F32),
                        pltpu.VMEM((pp, n_q, t, 2 * LANES), F32),
                        pltpu.VMEM((pp, n_q, hh, t, LANES), F32),
                        pltpu.VMEM((pp, hh, t, t), F32),
                        pltpu.VMEM((pp, t, hh * t), BF16),
                        pltpu.VMEM((pp, hh, t, LANES), F32),
                        pltpu.VMEM((pp, hh, t, LANES), F32)],
        compiler_params=pltpu.CompilerParams(
            dimension_semantics=("arbitrary", "arbitrary"), vmem_limit_bytes=VMEM_LIMIT_BYTES),
        name="fox_attn",
    )(jnp.asarray(table), q, k, v, f_tiles, gate, g.reshape(1, width))


def kernel(x, c, w_ada, b_ada, g_ffn1, w_ffn1_up, w_ffn1_down, g_mix, w_in, b_forget, g_fox_q,
           g_fox_k, g_sb_out, g_fox_out, w_out, g_ffn2, w_ffn2_up, w_ffn2_down, g_final):
    bsz, s, d = x.shape
    depth = w_ada.shape[0]
    t = SB_T

    head_of = jnp.arange(FOX_WIDTH) // HEAD_DIM
    grp = jnp.where(head_of[:, None] == head_of[None, :], 1.0 / HEAD_DIM, 0.0).astype(BF16)
    key = jnp.arange(t)
    tri = (key[:, None] >= key[None, :]).astype(BF16)

    n_main = 3 * SB_WIDTH + 4 * FOX_WIDTH
    for l in range(depth):
        ada = _ada(c, w_ada[l], b_ada[l]).reshape(bsz, N_SUBLAYERS, 3, 1, d)
        shift, scale, gate = ada[:, :, 0], ada[:, :, 1], ada[:, :, 2]

        x = _ffn(x, shift[:, 0], scale[:, 0], gate[:, 0], g_ffn1[l],
                 w_ffn1_up[l], w_ffn1_down[l])

        w_f = jnp.pad(w_in[l][:, n_main:], ((0, 0), (0, LANES - FOX_HEADS))).astype(BF16)
        sbq, sbk, sbv, fxq, fxk, fxv, fxg, fcum = _in_proj(
            x, shift[:, 1], scale[:, 1], g_mix[l], w_in[l], w_f,
            b_forget[l].reshape(FOX_HEADS, 1),
            jnp.tile(g_fox_q[l], FOX_HEADS).reshape(1, FOX_WIDTH),
            jnp.tile(g_fox_k[l], FOX_HEADS).reshape(1, FOX_WIDTH), grp)

        o_sb = _sb_attention(sbq, sbk, sbv, g_sb_out[l], tri)
        o_fx = _fox_attention(fxq, fxk, fxv, fcum, fxg, g_fox_out[l])

        x = _ffn(x, shift[:, 2], scale[:, 2], gate[:, 2], g_ffn2[l],
                 w_ffn2_up[l], w_ffn2_down[l],
                 mix=(o_sb, o_fx, w_out[l], gate[:, 1]),
                 g_final=g_final if l == depth - 1 else None)
    return x
```

```python
import functools

import numpy as np

import jax
import jax.numpy as jnp
from jax import lax
from jax.experimental import pallas as pl
from jax.experimental.pallas import tpu as pltpu

HEAD_DIM = 64
SB_HEADS = 8
FOX_HEADS = 8
SB_WIDTH = SB_HEADS * HEAD_DIM
FOX_WIDTH = FOX_HEADS * HEAD_DIM
N_SUBLAYERS = 3
EPS = 1e-6
QK_SCALE = HEAD_DIM ** -0.5

LANES = 128
HEADS_PER_TILE = LANES // HEAD_DIM
VMEM_LIMIT_BYTES = 56 * 1024 * 1024

ADA_TN = 1152
FFN_TM = 512
FFN_HC = 256
PROJ_TM = 512
SB_T = 256
FOX_T = 512
SB_PAIRS = 2
FOX_PAIRS = 2

LOG2E = 1.4426950408889634
F32_EXP2_ZERO_BELOW = -150.0
MASKED = -1e30

BF16 = jnp.bfloat16
F32 = jnp.float32


def _dot(a, b):
    return jnp.dot(a, b, preferred_element_type=F32)


def _dot_nt(a, b):
    return lax.dot_general(a, b, (((1,), (1,)), ((), ())), preferred_element_type=F32)


def _sigmoid(x):
    return 1.0 / (1.0 + jnp.exp(-x))


def _softplus(x):
    return jnp.maximum(x, 0.0) + jnp.log(1.0 + jnp.exp(-jnp.abs(x)))


def _modulate(x, g, shift, scale):
    ms = jnp.mean(x * x, axis=-1, keepdims=True)
    return (x * lax.rsqrt(ms + EPS)) * (g * (1.0 + scale)) + shift


def _const_spec(shape):
    return pl.BlockSpec(shape, lambda *_: (0,) * len(shape), pipeline_mode=pl.Buffered(1))


def _ada_kernel(c_ref, w_ref, b_ref, o_ref):
    c = c_ref[...]
    cond = (c * _sigmoid(c)).astype(BF16)
    o_ref[...] = _dot(cond, w_ref[...].astype(BF16)) + b_ref[...]


def _ada(c, w, b):
    bsz, d = c.shape
    n = w.shape[1]
    return pl.pallas_call(
        _ada_kernel,
        grid=(n // ADA_TN,),
        in_specs=[
            pl.BlockSpec((bsz, d), lambda j: (0, 0)),
            pl.BlockSpec((d, ADA_TN), lambda j: (0, j)),
            pl.BlockSpec((1, ADA_TN), lambda j: (0, j)),
        ],
        out_specs=pl.BlockSpec((bsz, ADA_TN), lambda j: (0, j)),
        out_shape=jax.ShapeDtypeStruct((bsz, n), F32),
        compiler_params=pltpu.CompilerParams(
            dimension_semantics=("arbitrary",), vmem_limit_bytes=VMEM_LIMIT_BYTES),
        name="ada",
    )(c, w, b.reshape(1, n))


def _ffn_kernel(*refs, pre_mix, final_norm):
    it = iter(refs)
    x_ref = next(it)
    if pre_mix:
        osb_ref, ofx_ref, wout_ref, gate_mix_ref = next(it), next(it), next(it), next(it)
    shift_ref, scale_ref, gate_ref, g_ref = next(it), next(it), next(it), next(it)
    wup_ref, wdown_ref = next(it), next(it)
    if final_norm:
        gf_ref = next(it)
    o_ref, h_scr, acc_scr = next(it), next(it), next(it)

    x = x_ref[...]
    if pre_mix:
        mix = (_dot(osb_ref[...], wout_ref[0:SB_WIDTH, :].astype(BF16))
               + _dot(ofx_ref[...], wout_ref[SB_WIDTH:SB_WIDTH + FOX_WIDTH, :].astype(BF16)))
        x = x + gate_mix_ref[...] * mix
    o_ref[...] = x
    h_scr[...] = _modulate(x, g_ref[...], shift_ref[...], scale_ref[...]).astype(BF16)
    acc_scr[...] = jnp.zeros_like(acc_scr)

    hidden = wdown_ref.shape[0]
    for lo in range(0, hidden, FFN_HC):
        h = h_scr[...]
        gt = _dot(h, wup_ref[:, lo:lo + FFN_HC].astype(BF16))
        up = _dot(h, wup_ref[:, hidden + lo:hidden + lo + FFN_HC].astype(BF16))
        a = (gt * _sigmoid(gt) * up).astype(BF16)
        acc_scr[...] += _dot(a, wdown_ref[lo:lo + FFN_HC, :].astype(BF16))
    y = o_ref[...] + (0.5 * gate_ref[...]) * acc_scr[...]
    if final_norm:
        ms = jnp.mean(y * y, axis=-1, keepdims=True)
        y = (y * lax.rsqrt(ms + EPS)) * gf_ref[...]
    o_ref[...] = y


def _ffn(x, shift, scale, gate, g, w_up, w_down, mix=None, g_final=None):
    bsz, s, d = x.shape
    assert w_down.shape[0] % FFN_HC == 0
    tm = FFN_TM
    tok = pl.BlockSpec((None, tm, d), lambda b, i: (b, i, 0))
    per_batch = pl.BlockSpec((None, 1, d), lambda b, i: (b, 0, 0))
    args, specs = [x], [tok]
    if mix is not None:
        o_sb, o_fx, w_out, gate_mix = mix
        args += [o_sb, o_fx, w_out, gate_mix]
        specs += [pl.BlockSpec((None, tm, SB_WIDTH), lambda b, i: (b, i, 0)),
                  pl.BlockSpec((None, tm, FOX_WIDTH), lambda b, i: (b, i, 0)),
                  _const_spec(w_out.shape), per_batch]
    args += [shift, scale, gate, g.reshape(1, d), w_up, w_down]
    specs += [per_batch, per_batch, per_batch, _const_spec((1, d)),
              _const_spec(w_up.shape), _const_spec(w_down.shape)]
    if g_final is not None:
        args.append(g_final.reshape(1, d))
        specs.append(_const_spec((1, d)))
    return pl.pallas_call(
        functools.partial(_ffn_kernel, pre_mix=mix is not None, final_norm=g_final is not None),
        grid=(bsz, s // tm),
        in_specs=specs,
        out_specs=tok,
        out_shape=jax.ShapeDtypeStruct((bsz, s, d), F32),
        scratch_shapes=[pltpu.VMEM((tm, d), BF16), pltpu.VMEM((tm, d), F32)],
        compiler_params=pltpu.CompilerParams(
            dimension_semantics=("arbitrary", "arbitrary"), vmem_limit_bytes=VMEM_LIMIT_BYTES),
        name="ffn_mix" if mix is not None else "ffn",
    )(*args)


def _lane_cumsum(y):
    n = y.shape[-1]
    lane = lax.broadcasted_iota(jnp.int32, y.shape, y.ndim - 1)
    shift = 1
    while shift < n:
        y = y + jnp.where(lane >= shift, pltpu.roll(y, shift, axis=y.ndim - 1), 0.0)
        shift *= 2
    return y


def _proj_kernel(x_ref, shift_ref, scale_ref, g_ref, w_ref, wf_ref, bf_ref, gq_ref, gk_ref,
                 grp_ref, sbq_ref, sbk_ref, sbv_ref, fxq_ref, fxk_ref, fxv_ref, fxg_ref,
                 fcum_ref, h_scr, carry_scr):
    @pl.when(pl.program_id(1) == 0)
    def _():
        carry_scr[...] = jnp.zeros_like(carry_scr)

    x = x_ref[...]
    h_scr[...] = _modulate(x, g_ref[...], shift_ref[...], scale_ref[...]).astype(BF16)
    h = h_scr[...]
    w = SB_WIDTH

    def proj(idx):
        return _dot(h, w_ref[:, idx * w:(idx + 1) * w].astype(BF16))

    def head_rms(p):
        ms = _dot((p * p).astype(BF16), grp_ref[...])
        return p * lax.rsqrt(ms + EPS)

    f = _dot(h, wf_ref[...])
    ft = jnp.transpose(f)[0:FOX_HEADS, :] + bf_ref[...]
    log_f = -_softplus(-ft)
    fcum = _lane_cumsum(log_f) + carry_scr[:, 0:1]
    fcum_ref[...] = fcum
    carry_scr[...] = jnp.broadcast_to(fcum[:, fcum.shape[1] - 1:], carry_scr.shape)

    fxq_ref[...] = (head_rms(proj(3)) * (gq_ref[...] * (QK_SCALE * LOG2E))).astype(BF16)
    fxk_ref[...] = (head_rms(proj(4)) * gk_ref[...]).astype(BF16)
    fxg_ref[...] = _sigmoid(proj(6)).astype(BF16)
    sbq_ref[...] = (proj(0) * (QK_SCALE * LOG2E)).astype(BF16)
    sbk_ref[...] = proj(1).astype(BF16)
    sbv_ref[...] = proj(2).astype(BF16)
    fxv_ref[...] = proj(5).astype(BF16)


def _in_proj(x, shift, scale, g, w_main, w_f, b_f, gq, gk, grp):
    bsz, s, d = x.shape
    tm = PROJ_TM
    tok = pl.BlockSpec((None, tm, d), lambda b, i: (b, i, 0))
    per_batch = pl.BlockSpec((None, 1, d), lambda b, i: (b, 0, 0))
    head_out = pl.BlockSpec((None, tm, SB_WIDTH), lambda b, i: (b, i, 0))
    head_shape = jax.ShapeDtypeStruct((bsz, s, SB_WIDTH), BF16)
    return pl.pallas_call(
        _proj_kernel,
        grid=(bsz, s // tm),
        in_specs=[tok, per_batch, per_batch, _const_spec((1, d)), _const_spec(w_main.shape),
                  _const_spec(w_f.shape), _const_spec(b_f.shape), _const_spec(gq.shape),
                  _const_spec(gk.shape), _const_spec(grp.shape)],
        out_specs=[head_out] * 7 + [pl.BlockSpec((None, FOX_HEADS, tm), lambda b, i: (b, 0, i))],
        out_shape=[head_shape] * 7 + [jax.ShapeDtypeStruct((bsz, FOX_HEADS, s), F32)],
        scratch_shapes=[pltpu.VMEM((tm, d), BF16), pltpu.VMEM((FOX_HEADS, LANES), F32)],
        compiler_params=pltpu.CompilerParams(
            dimension_semantics=("arbitrary", "arbitrary"), vmem_limit_bytes=VMEM_LIMIT_BYTES),
        name="in_proj",
    )(x, shift, scale, g.reshape(1, d), w_main, w_f, b_f, gq, gk, grp)


def _split_heads(x):
    lane = lax.broadcasted_iota(jnp.int32, x.shape, 1)
    zero = jnp.zeros_like(x)
    return jnp.where(lane < HEAD_DIM, x, zero), jnp.where(lane >= HEAD_DIM, x, zero)


def _pair_lanes(pp):
    return slice(pp * LANES, (pp + 1) * LANES)


def _fill_vexp(v_ref, lanes, vexp_ref, n_blocks, t):
    def body(j, carry):
        v0, v1 = _split_heads(v_ref[pl.ds(pl.multiple_of(j * t, t), t), lanes])
        vexp_ref[j, 0:t, 0:LANES] = v0
        vexp_ref[j, t:2 * t, 0:LANES] = v1
        return carry

    lax.fori_loop(0, n_blocks, body, 0)


def _fill_vexp_ones(vexp_ref, n_blocks, t):
    def body(j, carry):
        lane = lax.broadcasted_iota(jnp.int32, (t, LANES), 1)
        vexp_ref[j, 0:t, LANES:2 * LANES] = (lane < HEAD_DIM).astype(F32).astype(vexp_ref.dtype)
        vexp_ref[j, t:2 * t, LANES:2 * LANES] = (lane >= HEAD_DIM).astype(F32).astype(
            vexp_ref.dtype)
        return carry

    lax.fori_loop(0, n_blocks, body, 0)


def _per_head(lane, a0, a1):
    return jnp.where(lane < HEAD_DIM, a0, a1)


def _lane_tile(a, width):
    return jnp.concatenate([a] * (width // LANES), axis=1)


def _head_rmsnorm(o, lane):
    sq = o * o
    ss0 = jnp.sum(jnp.where(lane < HEAD_DIM, sq, 0.0), axis=1, keepdims=True)
    ss1 = jnp.sum(jnp.where(lane >= HEAD_DIM, sq, 0.0), axis=1, keepdims=True)
    inv = _per_head(lane, lax.rsqrt(ss0 * (1.0 / HEAD_DIM) + EPS),
                    lax.rsqrt(ss1 * (1.0 / HEAD_DIM) + EPS))
    return o * inv


def _task_table(tasks, n_q):
    seen = set()
    rows = []
    for q_blk, k_blk, slot, mask_idx in tasks:
        rows.append((q_blk, k_blk, slot, mask_idx, int(slot not in seen)))
        seen.add(slot)
    return np.asarray(rows, dtype=np.int32).T.copy()


def _run_pipeline(n_tasks, score, middle, value, unroll=1):
    assert n_tasks >= 2
    score(0)
    middle(0)
    score(1)

    def body(j, carry):
        value(j - 2)
        middle(j - 1)
        score(j)
        return carry

    lax.fori_loop(2, n_tasks, body, 0, unroll=unroll)
    value(n_tasks - 2)
    middle(n_tasks - 1)
    value(n_tasks - 1)


def _fill_qexp(q_ref, lanes, qexp_ref, n_blocks, t):
    def body(j, carry):
        rows = pl.ds(pl.multiple_of(j * t, t), t)
        q0, q1 = _split_heads(q_ref[rows, lanes])
        qexp_ref[0, rows, :] = q0
        qexp_ref[1, rows, :] = q1
        return carry

    lax.fori_loop(0, n_blocks, body, 0)


def _first_grid_step():
    return jnp.logical_and(pl.program_id(0) == 0, pl.program_id(1) == 0)


def _causal_bias(t, strict):
    row = lax.broadcasted_iota(jnp.int32, (t, t), 0)
    col = lax.broadcasted_iota(jnp.int32, (t, t), 1)
    keep = col < row if strict else col <= row
    return jnp.where(keep, 0.0, MASKED).astype(F32)


def _sb_kernel(tab_ref, q_ref, k_ref, v_ref, g_ref, tri_ref, o_ref, vexp_scr, qexp_scr, bias_scr,
               acc_scr, carry_scr, z_scr, zc_scr, suf_scr, sp_scr, w_scr, l_scr):
    t = SB_T
    n_q = q_ref.shape[0] // t
    pairs = range(q_ref.shape[1] // LANES)

    for pp in pairs:
        _fill_vexp(v_ref, _pair_lanes(pp), vexp_scr.at[pp], n_q, t)
        _fill_qexp(q_ref, _pair_lanes(pp), qexp_scr.at[pp], n_q, t)

    @pl.when(_first_grid_step())
    def _():
        acc_scr[...] = jnp.zeros_like(acc_scr)
        carry_scr[...] = jnp.zeros_like(carry_scr)
        bias_scr[0] = jnp.zeros((t, t), F32)
        bias_scr[1] = _causal_bias(t, strict=True)

    def score_stage(pp, q_blk, k_blk, mask_idx):
        rows = pl.ds(pl.multiple_of(q_blk * t, t), t)
        k = k_ref[pl.ds(pl.multiple_of(k_blk * t, t), t), _pair_lanes(pp)]
        bias = bias_scr[mask_idx]
        for hd in range(HEADS_PER_TILE):
            z = _dot_nt(qexp_scr[pp, hd, rows, :], k) + bias
            z_scr[pp, hd] = z
            l_scr[pp, hd] = jnp.log2(1.0 + jnp.exp2(-jnp.abs(z)))

    def keep_stage(pp, slot, first):
        for hd in range(HEADS_PER_TILE):
            carry = jnp.where(first > 0, 0.0, carry_scr[pp, slot, hd])
            z = z_scr[pp, hd]
            sp_scr[pp, hd] = (jnp.maximum(z, 0.0) + l_scr[pp, hd]).astype(BF16)
            zc_scr[pp, hd] = z - _lane_tile(carry, t)
            suf = _dot(sp_scr[pp, hd], tri_ref[...])
            suf_scr[pp, hd] = suf
            carry_scr[pp, slot, hd] = carry + suf[:, 0:1]

    def value_stage(pp, k_blk, slot, first):
        for hd in range(HEADS_PER_TILE):
            w_scr[pp, :, hd * t:(hd + 1) * t] = jnp.exp2(
                zc_scr[pp, hd] - suf_scr[pp, hd]).astype(BF16)
        acc = jnp.where(first > 0, 0.0, acc_scr[pp, slot])
        acc_scr[pp, slot] = acc + _dot(w_scr[pp], vexp_scr[pp, k_blk])

    def score_all(j):
        for pp in pairs:
            score_stage(pp, tab_ref[0, j], tab_ref[1, j], tab_ref[3, j])

    def keep_all(j):
        for pp in pairs:
            keep_stage(pp, tab_ref[2, j], tab_ref[4, j])

    def value_all(j):
        for pp in pairs:
            value_stage(pp, tab_ref[1, j], tab_ref[2, j], tab_ref[4, j])

    _run_pipeline(tab_ref.shape[1], score_all, keep_all, value_all)

    def tail(pp, q_blk):
        def live():
            return (jnp.min(carry_scr[pp, q_blk]) <= -F32_EXP2_ZERO_BELOW).astype(jnp.int32)

        def cond(state):
            k_blk, go = state
            return jnp.logical_and(k_blk >= 0, go > 0)

        def step(state):
            k_blk, _ = state
            score_stage(pp, q_blk, k_blk, 0)
            keep_stage(pp, q_blk, 0)
            value_stage(pp, k_blk, q_blk, 0)
            return k_blk - 1, live()

        lax.while_loop(cond, step, (q_blk - 2, live()))

    lane = lax.broadcasted_iota(jnp.int32, (t, LANES), 1)
    for pp in pairs:
        def tail_body(q_blk, carry, pp=pp):
            tail(pp, q_blk)
            return carry

        @pl.when(jnp.min(carry_scr[pp, 2:n_q]) <= -F32_EXP2_ZERO_BELOW)
        def _():
            lax.fori_loop(2, n_q, tail_body, 0)

        lanes = _pair_lanes(pp)
        for q_blk in range(n_q):
            o = _head_rmsnorm(acc_scr[pp, q_blk], lane) * g_ref[:, lanes]
            o_ref[q_blk * t:(q_blk + 1) * t, lanes] = o.astype(o_ref.dtype)


def _sb_attention(q, k, v, g, tri):
    bsz, s, width = q.shape
    t = SB_T
    n_q = s // t
    n_tiles = width // LANES
    tasks = []
    for qb in range(n_q):
        tasks.append((qb, qb, qb, 1))
        if qb >= 1:
            tasks.append((qb, qb - 1, qb, 0))
    table = _task_table(tasks, n_q)
    pp, hh = SB_PAIRS, HEADS_PER_TILE
    seq_spec = pl.BlockSpec((None, s, pp * LANES), lambda b, hp: (b, 0, hp))
    return pl.pallas_call(
        _sb_kernel,
        grid=(bsz, n_tiles // pp),
        in_specs=[pl.BlockSpec(memory_space=pltpu.SMEM), seq_spec, seq_spec, seq_spec,
                  pl.BlockSpec((1, pp * LANES), lambda b, hp: (0, hp)), _const_spec(tri.shape)],
        out_specs=seq_spec,
        out_shape=jax.ShapeDtypeStruct((bsz, s, width), BF16),
        scratch_shapes=[pltpu.VMEM((pp, n_q, 2 * t, LANES), BF16),
                        pltpu.VMEM((pp, hh, s, LANES), BF16),
                        pltpu.VMEM((2, t, t), F32),
                        pltpu.VMEM((pp, n_q, t, LANES), F32),
                        pltpu.VMEM((pp, n_q, hh, t, LANES), F32),
                        pltpu.VMEM((pp, hh, t, t), F32),
                        pltpu.VMEM((pp, hh, t, t), F32),
                        pltpu.VMEM((pp, hh, t, t), F32),
                        pltpu.VMEM((pp, hh, t, t), BF16),
                        pltpu.VMEM((pp, t, hh * t), BF16),
                        pltpu.VMEM((pp, hh, t, t), F32)],
        compiler_params=pltpu.CompilerParams(
            dimension_semantics=("arbitrary", "arbitrary"), vmem_limit_bytes=VMEM_LIMIT_BYTES),
        name="sb_attn",
    )(jnp.asarray(table), q, k, v, g.reshape(1, width), tri)


def _fox_kernel(tab_ref, q_ref, k_ref, v_ref, f_ref, gate_ref, g_ref, o_ref, vexp_scr, qexp_scr,
                bias_scr, acc_scr, m_scr, s_scr, p_scr, alpha_scr, bmax_scr):
    t = FOX_T
    n_q = q_ref.shape[0] // t
    pairs = range(q_ref.shape[1] // LANES)
    lane = lax.broadcasted_iota(jnp.int32, (t, LANES), 1)

    for pp in pairs:
        _fill_vexp(v_ref, _pair_lanes(pp), vexp_scr.at[pp], n_q, t)
        _fill_qexp(q_ref, _pair_lanes(pp), qexp_scr.at[pp], n_q, t)

    @pl.when(_first_grid_step())
    def _():
        acc_scr[...] = jnp.zeros_like(acc_scr)
        m_scr[...] = jnp.full_like(m_scr, -jnp.inf)
        for pp in pairs:
            _fill_vexp_ones(vexp_scr.at[pp], n_q, t)
        bias_scr[0] = jnp.zeros((t, t), F32)
        bias_scr[1] = _causal_bias(t, strict=False)

    def score_stage(pp, q_blk, k_blk, mask_idx):
        rows = pl.ds(pl.multiple_of(q_blk * t, t), t)
        cols = pl.ds(pl.multiple_of(k_blk * t, t), t)
        k = k_ref[cols, _pair_lanes(pp)]
        f_keys = f_ref[pp, :, cols] * LOG2E
        bias = bias_scr[mask_idx]
        for hd in range(HEADS_PER_TILE):
            sc = _dot_nt(qexp_scr[pp, hd, rows, :], k) - f_keys[hd:hd + 1, :] + bias
            s_scr[pp, hd] = sc
            bmax_scr[pp, hd] = jnp.broadcast_to(jnp.max(sc, axis=1, keepdims=True), (t, LANES))

    def softmax_stage(pp, slot, first):
        for hd in range(HEADS_PER_TILE):
            m_old = jnp.where(first > 0, -jnp.inf, m_scr[pp, slot, hd])
            m_new = jnp.maximum(m_old, bmax_scr[pp, hd])
            alpha_scr[pp, hd] = jnp.exp2(m_old - m_new)
            m_scr[pp, slot, hd] = m_new
            p_scr[pp, :, hd * t:(hd + 1) * t] = jnp.exp2(
                s_scr[pp, hd] - _lane_tile(m_new, t)).astype(BF16)

    def value_stage(pp, k_blk, slot):
        pv = _dot(p_scr[pp], vexp_scr[pp, k_blk])
        alpha = _per_head(lane, alpha_scr[pp, 0], alpha_scr[pp, 1])
        acc_scr[pp, slot] = acc_scr[pp, slot] * _lane_tile(alpha, 2 * LANES) + pv

    def score_all(j):
        for pp in pairs:
            score_stage(pp, tab_ref[0, j], tab_ref[1, j], tab_ref[3, j])

    def softmax_all(j):
        for pp in pairs:
            softmax_stage(pp, tab_ref[2, j], tab_ref[4, j])

    def value_all(j):
        for pp in pairs:
            value_stage(pp, tab_ref[1, j], tab_ref[2, j])

    _run_pipeline(tab_ref.shape[1], score_all, softmax_all, value_all, unroll=2)

    for pp in pairs:
        lanes = _pair_lanes(pp)
        for q_blk in range(n_q):
            rows = slice(q_blk * t, (q_blk + 1) * t)
            acc = acc_scr[pp, q_blk]
            o = acc[:, 0:LANES] / acc[:, LANES:2 * LANES]
            o = _head_rmsnorm(o, lane) * g_ref[:, lanes] * gate_ref[rows, lanes].astype(F32)
            o_ref[rows, lanes] = o.astype(o_ref.dtype)


def _fox_attention(q, k, v, fcum, gate, g):
    bsz, s, width = q.shape
    t = FOX_T
    n_q = s // t
    n_tiles = width // LANES
    tasks = [(qb, kb, qb, int(kb == qb)) for qb in range(n_q) for kb in range(qb + 1)]
    table = _task_table(tasks, n_q)
    pp, hh = FOX_PAIRS, HEADS_PER_TILE
    seq_spec = pl.BlockSpec((None, s, pp * LANES), lambda b, hp: (b, 0, hp))
    f_tiles = fcum.reshape(bsz, n_tiles, hh, s)
    return pl.pallas_call(
        _fox_kernel,
        grid=(bsz, n_tiles // pp),
        in_specs=[pl.BlockSpec(memory_space=pltpu.SMEM), seq_spec, seq_spec, seq_spec,
                  pl.BlockSpec((None, pp, hh, s), lambda b, hp: (b, hp, 0, 0)),
                  seq_spec, pl.BlockSpec((1, pp * LANES), lambda b, hp: (0, hp))],
        out_specs=seq_spec,
        out_shape=jax.ShapeDtypeStruct((bsz, s, width), BF16),
        scratch_shapes=[pltpu.VMEM((pp, n_q, 2 * t, 2 * LANES), BF16),
                        pltpu.VMEM((pp, hh, s, LANES), BF16),
                        pltpu.VMEM((2, t, t), F32),
                        pltpu.VMEM((pp, n_q, t, 2 * LANES), F32),
                        pltpu.VMEM((pp, n_q, hh, t, LANES), F32),
                        pltpu.VMEM((pp, hh, t, t), F32),
                        pltpu.VMEM((pp, t, hh * t), BF16),
                        pltpu.VMEM((pp, hh, t, LANES), F32),
                        pltpu.VMEM((pp, hh, t, LANES), F32)],
        compiler_params=pltpu.CompilerParams(
            dimension_semantics=("arbitrary", "arbitrary"), vmem_limit_bytes=VMEM_LIMIT_BYTES),
        name="fox_attn",
    )(jnp.asarray(table), q, k, v, f_tiles, gate, g.reshape(1, width))


def kernel(x, c, w_ada, b_ada, g_ffn1, w_ffn1_up, w_ffn1_down, g_mix, w_in, b_forget, g_fox_q,
           g_fox_k, g_sb_out, g_fox_out, w_out, g_ffn2, w_ffn2_up, w_ffn2_down, g_final):
    bsz, s, d = x.shape
    depth = w_ada.shape[0]
    t = SB_T

    head_of = jnp.arange(FOX_WIDTH) // HEAD_DIM
    grp = jnp.where(head_of[:, None] == head_of[None, :], 1.0 / HEAD_DIM, 0.0).astype(BF16)
    key = jnp.arange(t)
    tri = (key[:, None] >= key[None, :]).astype(BF16)

    n_main = 3 * SB_WIDTH + 4 * FOX_WIDTH
    for l in range(depth):
        ada = _ada(c, w_ada[l], b_ada[l]).reshape(bsz, N_SUBLAYERS, 3, 1, d)
        shift, scale, gate = ada[:, :, 0], ada[:, :, 1], ada[:, :, 2]

        x = _ffn(x, shift[:, 0], scale[:, 0], gate[:, 0], g_ffn1[l],
                 w_ffn1_up[l], w_ffn1_down[l])

        w_f = jnp.pad(w_in[l][:, n_main:], ((0, 0), (0, LANES - FOX_HEADS))).astype(BF16)
        sbq, sbk, sbv, fxq, fxk, fxv, fxg, fcum = _in_proj(
            x, shift[:, 1], scale[:, 1], g_mix[l], w_in[l], w_f,
            b_forget[l].reshape(FOX_HEADS, 1),
            jnp.tile(g_fox_q[l], FOX_HEADS).reshape(1, FOX_WIDTH),
            jnp.tile(g_fox_k[l], FOX_HEADS).reshape(1, FOX_WIDTH), grp)

        o_sb = _sb_attention(sbq, sbk, sbv, g_sb_out[l], tri)
        o_fx = _fox_attention(fxq, fxk, fxv, fcum, fxg, g_fox_out[l])

        x = _ffn(x, shift[:, 2], scale[:, 2], gate[:, 2], g_ffn2[l],
                 w_ffn2_up[l], w_ffn2_down[l],
                 mix=(o_sb, o_fx, w_out[l], gate[:, 1]),
                 g_final=g_final if l == depth - 1 else None)
    return x
```

```python
import functools

import numpy as np

import jax
import jax.numpy as jnp
from jax import lax
from jax.experimental import pallas as pl
from jax.experimental.pallas import tpu as pltpu

HEAD_DIM = 64
SB_HEADS = 8
FOX_HEADS = 8
SB_WIDTH = SB_HEADS * HEAD_DIM
FOX_WIDTH = FOX_HEADS * HEAD_DIM
N_SUBLAYERS = 3
EPS = 1e-6
QK_SCALE = HEAD_DIM ** -0.5

LANES = 128
HEADS_PER_TILE = LANES // HEAD_DIM
VMEM_LIMIT_BYTES = 56 * 1024 * 1024

ADA_TN = 1152
FFN_TM = 512
FFN_HC = 256
PROJ_TM = 512
SB_T = 256
FOX_T = 512
SB_PAIRS = 2
FOX_PAIRS = 2

LOG2E = 1.4426950408889634
F32_EXP2_ZERO_BELOW = -150.0
MASKED = -1e30

BF16 = jnp.bfloat16
F32 = jnp.float32


def _dot(a, b):
    return jnp.dot(a, b, preferred_element_type=F32)


def _dot_nt(a, b):
    return lax.dot_general(a, b, (((1,), (1,)), ((), ())), preferred_element_type=F32)


def _sigmoid(x):
    return 1.0 / (1.0 + jnp.exp(-x))


def _softplus(x):
    return jnp.maximum(x, 0.0) + jnp.log(1.0 + jnp.exp(-jnp.abs(x)))


def _modulate(x, g, shift, scale):
    ms = jnp.mean(x * x, axis=-1, keepdims=True)
    return (x * lax.rsqrt(ms + EPS)) * (g * (1.0 + scale)) + shift


def _const_spec(shape):
    return pl.BlockSpec(shape, lambda *_: (0,) * len(shape), pipeline_mode=pl.Buffered(1))


def _ada_kernel(c_ref, w_ref, b_ref, o_ref):
    c = c_ref[...]
    cond = (c * _sigmoid(c)).astype(BF16)
    o_ref[...] = _dot(cond, w_ref[...].astype(BF16)) + b_ref[...]


def _ada(c, w, b):
    bsz, d = c.shape
    n = w.shape[1]
    return pl.pallas_call(
        _ada_kernel,
        grid=(n // ADA_TN,),
        in_specs=[
            pl.BlockSpec((bsz, d), lambda j: (0, 0)),
            pl.BlockSpec((d, ADA_TN), lambda j: (0, j)),
            pl.BlockSpec((1, ADA_TN), lambda j: (0, j)),
        ],
        out_specs=pl.BlockSpec((bsz, ADA_TN), lambda j: (0, j)),
        out_shape=jax.ShapeDtypeStruct((bsz, n), F32),
        compiler_params=pltpu.CompilerParams(
            dimension_semantics=("arbitrary",), vmem_limit_bytes=VMEM_LIMIT_BYTES),
        name="ada",
    )(c, w, b.reshape(1, n))


def _ffn_kernel(*refs, pre_mix, final_norm):
    it = iter(refs)
    x_ref = next(it)
    if pre_mix:
        osb_ref, ofx_ref, wout_ref, gate_mix_ref = next(it), next(it), next(it), next(it)
    shift_ref, scale_ref, gate_ref, g_ref = next(it), next(it), next(it), next(it)
    wup_ref, wdown_ref = next(it), next(it)
    if final_norm:
        gf_ref = next(it)
    o_ref, h_scr, acc_scr = next(it), next(it), next(it)

    x = x_ref[...]
    if pre_mix:
        mix = (_dot(osb_ref[...], wout_ref[0:SB_WIDTH, :].astype(BF16))
               + _dot(ofx_ref[...], wout_ref[SB_WIDTH:SB_WIDTH + FOX_WIDTH, :].astype(BF16)))
        x = x + gate_mix_ref[...] * mix
    o_ref[...] = x
    h_scr[...] = _modulate(x, g_ref[...], shift_ref[...], scale_ref[...]).astype(BF16)
    acc_scr[...] = jnp.zeros_like(acc_scr)

    hidden = wdown_ref.shape[0]
    for lo in range(0, hidden, FFN_HC):
        h = h_scr[...]
        gt = _dot(h, wup_ref[:, lo:lo + FFN_HC].astype(BF16))
        up = _dot(h, wup_ref[:, hidden + lo:hidden + lo + FFN_HC].astype(BF16))
        a = (gt * _sigmoid(gt) * up).astype(BF16)
        acc_scr[...] += _dot(a, wdown_ref[lo:lo + FFN_HC, :].astype(BF16))
    y = o_ref[...] + (0.5 * gate_ref[...]) * acc_scr[...]
    if final_norm:
        ms = jnp.mean(y * y, axis=-1, keepdims=True)
        y = (y * lax.rsqrt(ms + EPS)) * gf_ref[...]
    o_ref[...] = y


def _ffn(x, shift, scale, gate, g, w_up, w_down, mix=None, g_final=None):
    bsz, s, d = x.shape
    assert w_down.shape[0] % FFN_HC == 0
    tm = FFN_TM
    tok = pl.BlockSpec((None, tm, d), lambda b, i: (b, i, 0))
    per_batch = pl.BlockSpec((None, 1, d), lambda b, i: (b, 0, 0))
    args, specs = [x], [tok]
    if mix is not None:
        o_sb, o_fx, w_out, gate_mix = mix
        args += [o_sb, o_fx, w_out, gate_mix]
        specs += [pl.BlockSpec((None, tm, SB_WIDTH), lambda b, i: (b, i, 0)),
                  pl.BlockSpec((None, tm, FOX_WIDTH), lambda b, i: (b, i, 0)),
                  _const_spec(w_out.shape), per_batch]
    args += [shift, scale, gate, g.reshape(1, d), w_up, w_down]
    specs += [per_batch, per_batch, per_batch, _const_spec((1, d)),
              _const_spec(w_up.shape), _const_spec(w_down.shape)]
    if g_final is not None:
        args.append(g_final.reshape(1, d))
        specs.append(_const_spec((1, d)))
    return pl.pallas_call(
        functools.partial(_ffn_kernel, pre_mix=mix is not None, final_norm=g_final is not None),
        grid=(bsz, s // tm),
        in_specs=specs,
        out_specs=tok,
        out_shape=jax.ShapeDtypeStruct((bsz, s, d), F32),
        scratch_shapes=[pltpu.VMEM((tm, d), BF16), pltpu.VMEM((tm, d), F32)],
        compiler_params=pltpu.CompilerParams(
            dimension_semantics=("arbitrary", "arbitrary"), vmem_limit_bytes=VMEM_LIMIT_BYTES),
        name="ffn_mix" if mix is not None else "ffn",
    )(*args)


def _lane_cumsum(y):
    n = y.shape[-1]
    lane = lax.broadcasted_iota(jnp.int32, y.shape, y.ndim - 1)
    shift = 1
    while shift < n:
        y = y + jnp.where(lane >= shift, pltpu.roll(y, shift, axis=y.ndim - 1), 0.0)
        shift *= 2
    return y


def _proj_kernel(x_ref, shift_ref, scale_ref, g_ref, w_ref, wf_ref, bf_ref, gq_ref, gk_ref,
                 sbq_ref, sbk_ref, sbv_ref, fxq_ref, fxk_ref, fxv_ref, fxg_ref,
                 fcum_ref, h_scr, carry_scr):
    @pl.when(pl.program_id(1) == 0)
    def _():
        carry_scr[...] = jnp.zeros_like(carry_scr)

    x = x_ref[...]
    h_scr[...] = _modulate(x, g_ref[...], shift_ref[...], scale_ref[...]).astype(BF16)
    h = h_scr[...]
    w = SB_WIDTH

    def proj(idx):
        return _dot(h, w_ref[:, idx * w:(idx + 1) * w].astype(BF16))

    def head_rms(p):
        lane = lax.broadcasted_iota(jnp.int32, (p.shape[0], LANES), 1)
        return jnp.concatenate(
            [_head_rmsnorm(p[:, c0:c0 + LANES], lane) for c0 in range(0, p.shape[1], LANES)],
            axis=1)

    f = _dot(h, wf_ref[...])
    ft = jnp.transpose(f)[0:FOX_HEADS, :] + bf_ref[...]
    log_f = -_softplus(-ft)
    fcum = _lane_cumsum(log_f) + carry_scr[:, 0:1]
    fcum_ref[...] = fcum
    carry_scr[...] = jnp.broadcast_to(fcum[:, fcum.shape[1] - 1:], carry_scr.shape)

    fxq_ref[...] = (head_rms(proj(3)) * (gq_ref[...] * (QK_SCALE * LOG2E))).astype(BF16)
    fxk_ref[...] = (head_rms(proj(4)) * gk_ref[...]).astype(BF16)
    fxg_ref[...] = _sigmoid(proj(6)).astype(BF16)
    sbq_ref[...] = (proj(0) * (QK_SCALE * LOG2E)).astype(BF16)
    sbk_ref[...] = proj(1).astype(BF16)
    sbv_ref[...] = proj(2).astype(BF16)
    fxv_ref[...] = proj(5).astype(BF16)


def _in_proj(x, shift, scale, g, w_main, w_f, b_f, gq, gk):
    bsz, s, d = x.shape
    tm = PROJ_TM
    tok = pl.BlockSpec((None, tm, d), lambda b, i: (b, i, 0))
    per_batch = pl.BlockSpec((None, 1, d), lambda b, i: (b, 0, 0))
    head_out = pl.BlockSpec((None, tm, SB_WIDTH), lambda b, i: (b, i, 0))
    head_shape = jax.ShapeDtypeStruct((bsz, s, SB_WIDTH), BF16)
    return pl.pallas_call(
        _proj_kernel,
        grid=(bsz, s // tm),
        in_specs=[tok, per_batch, per_batch, _const_spec((1, d)), _const_spec(w_main.shape),
                  _const_spec(w_f.shape), _const_spec(b_f.shape), _const_spec(gq.shape),
                  _const_spec(gk.shape)],
        out_specs=[head_out] * 7 + [pl.BlockSpec((None, FOX_HEADS, tm), lambda b, i: (b, 0, i))],
        out_shape=[head_shape] * 7 + [jax.ShapeDtypeStruct((bsz, FOX_HEADS, s), F32)],
        scratch_shapes=[pltpu.VMEM((tm, d), BF16), pltpu.VMEM((FOX_HEADS, LANES), F32)],
        compiler_params=pltpu.CompilerParams(
            dimension_semantics=("arbitrary", "arbitrary"), vmem_limit_bytes=VMEM_LIMIT_BYTES),
        name="in_proj",
    )(x, shift, scale, g.reshape(1, d), w_main, w_f, b_f, gq, gk)


def _split_heads(x):
    lane = lax.broadcasted_iota(jnp.int32, x.shape, 1)
    zero = jnp.zeros_like(x)
    return jnp.where(lane < HEAD_DIM, x, zero), jnp.where(lane >= HEAD_DIM, x, zero)


def _pair_lanes(pp):
    return slice(pp * LANES, (pp + 1) * LANES)


def _fill_vexp(v_ref, lanes, vexp_ref, n_blocks, t):
    def body(j, carry):
        v0, v1 = _split_heads(v_ref[pl.ds(pl.multiple_of(j * t, t), t), lanes])
        vexp_ref[j, 0:t, 0:LANES] = v0
        vexp_ref[j, t:2 * t, 0:LANES] = v1
        return carry

    lax.fori_loop(0, n_blocks, body, 0)


def _fill_vexp_ones(vexp_ref, n_blocks, t):
    def body(j, carry):
        lane = lax.broadcasted_iota(jnp.int32, (t, LANES), 1)
        vexp_ref[j, 0:t, LANES:2 * LANES] = (lane < HEAD_DIM).astype(F32).astype(vexp_ref.dtype)
        vexp_ref[j, t:2 * t, LANES:2 * LANES] = (lane >= HEAD_DIM).astype(F32).astype(
            vexp_ref.dtype)
        return carry

    lax.fori_loop(0, n_blocks, body, 0)


def _per_head(lane, a0, a1):
    return jnp.where(lane < HEAD_DIM, a0, a1)


def _lane_tile(a, width):
    return jnp.concatenate([a] * (width // LANES), axis=1)


def _head_rmsnorm(o, lane):
    sq = o * o
    ss0 = jnp.sum(jnp.where(lane < HEAD_DIM, sq, 0.0), axis=1, keepdims=True)
    ss1 = jnp.sum(jnp.where(lane >= HEAD_DIM, sq, 0.0), axis=1, keepdims=True)
    inv = _per_head(lane, lax.rsqrt(ss0 * (1.0 / HEAD_DIM) + EPS),
                    lax.rsqrt(ss1 * (1.0 / HEAD_DIM) + EPS))
    return o * inv


def _task_table(tasks, n_q):
    seen = set()
    rows = []
    for q_blk, k_blk, slot, mask_idx in tasks:
        rows.append((q_blk, k_blk, slot, mask_idx, int(slot not in seen)))
        seen.add(slot)
    return np.asarray(rows, dtype=np.int32).T.copy()


def _run_pipeline(n_tasks, score, middle, value):
    assert n_tasks >= 2
    score(0)
    middle(0)
    score(1)

    def body(j, carry):
        value(j - 2)
        middle(j - 1)
        score(j)
        return carry

    lax.fori_loop(2, n_tasks, body, 0)
    value(n_tasks - 2)
    middle(n_tasks - 1)
    value(n_tasks - 1)


def _fill_qexp(q_ref, lanes, qexp_ref, n_blocks, t):
    def body(j, carry):
        rows = pl.ds(pl.multiple_of(j * t, t), t)
        q0, q1 = _split_heads(q_ref[rows, lanes])
        qexp_ref[0, rows, :] = q0
        qexp_ref[1, rows, :] = q1
        return carry

    lax.fori_loop(0, n_blocks, body, 0)


def _first_grid_step():
    return jnp.logical_and(pl.program_id(0) == 0, pl.program_id(1) == 0)


def _causal_bias(t, strict):
    row = lax.broadcasted_iota(jnp.int32, (t, t), 0)
    col = lax.broadcasted_iota(jnp.int32, (t, t), 1)
    keep = col < row if strict else col <= row
    return jnp.where(keep, 0.0, MASKED).astype(F32)


def _sb_kernel(tab_ref, q_ref, k_ref, v_ref, g_ref, tri_ref, o_ref, vexp_scr, qexp_scr, bias_scr,
               acc_scr, carry_scr, z_scr, zc_scr, suf_scr, sp_scr, w_scr, l_scr):
    t = SB_T
    n_q = q_ref.shape[0] // t
    pairs = range(q_ref.shape[1] // LANES)

    for pp in pairs:
        _fill_vexp(v_ref, _pair_lanes(pp), vexp_scr.at[pp], n_q, t)
        _fill_qexp(q_ref, _pair_lanes(pp), qexp_scr.at[pp], n_q, t)

    @pl.when(_first_grid_step())
    def _():
        acc_scr[...] = jnp.zeros_like(acc_scr)
        carry_scr[...] = jnp.zeros_like(carry_scr)
        bias_scr[0] = jnp.zeros((t, t), F32)
        bias_scr[1] = _causal_bias(t, strict=True)

    def score_stage(pp, q_blk, k_blk, mask_idx):
        rows = pl.ds(pl.multiple_of(q_blk * t, t), t)
        k = k_ref[pl.ds(pl.multiple_of(k_blk * t, t), t), _pair_lanes(pp)]
        bias = bias_scr[mask_idx]
        for hd in range(HEADS_PER_TILE):
            z = _dot_nt(qexp_scr[pp, hd, rows, :], k) + bias
            z_scr[pp, hd] = z
            l_scr[pp, hd] = jnp.log2(1.0 + jnp.exp2(-jnp.abs(z)))

    def keep_stage(pp, slot, first):
        for hd in range(HEADS_PER_TILE):
            carry = jnp.where(first > 0, 0.0, carry_scr[pp, slot, hd])
            z = z_scr[pp, hd]
            sp_scr[pp, hd] = (jnp.maximum(z, 0.0) + l_scr[pp, hd]).astype(BF16)
            zc_scr[pp, hd] = z - _lane_tile(carry, t)
            suf = _dot(sp_scr[pp, hd], tri_ref[...])
            suf_scr[pp, hd] = suf
            carry_scr[pp, slot, hd] = carry + suf[:, 0:1]

    def value_stage(pp, k_blk, slot, first):
        for hd in range(HEADS_PER_TILE):
            w_scr[pp, :, hd * t:(hd + 1) * t] = jnp.exp2(
                zc_scr[pp, hd] - suf_scr[pp, hd]).astype(BF16)
        acc = jnp.where(first > 0, 0.0, acc_scr[pp, slot])
        acc_scr[pp, slot] = acc + _dot(w_scr[pp], vexp_scr[pp, k_blk])

    def score_all(j):
        for pp in pairs:
            score_stage(pp, tab_ref[0, j], tab_ref[1, j], tab_ref[3, j])

    def keep_all(j):
        for pp in pairs:
            keep_stage(pp, tab_ref[2, j], tab_ref[4, j])

    def value_all(j):
        for pp in pairs:
            value_stage(pp, tab_ref[1, j], tab_ref[2, j], tab_ref[4, j])

    _run_pipeline(tab_ref.shape[1], score_all, keep_all, value_all)

    def tail(pp, q_blk):
        def live():
            return (jnp.min(carry_scr[pp, q_blk]) <= -F32_EXP2_ZERO_BELOW).astype(jnp.int32)

        def cond(state):
            k_blk, go = state
            return jnp.logical_and(k_blk >= 0, go > 0)

        def step(state):
            k_blk, _ = state
            score_stage(pp, q_blk, k_blk, 0)
            keep_stage(pp, q_blk, 0)
            value_stage(pp, k_blk, q_blk, 0)
            return k_blk - 1, live()

        lax.while_loop(cond, step, (q_blk - 2, live()))

    lane = lax.broadcasted_iota(jnp.int32, (t, LANES), 1)
    for pp in pairs:
        def tail_body(q_blk, carry, pp=pp):
            tail(pp, q_blk)
            return carry

        @pl.when(jnp.min(carry_scr[pp, 2:n_q]) <= -F32_EXP2_ZERO_BELOW)
        def _():
            lax.fori_loop(2, n_q, tail_body, 0)

        lanes = _pair_lanes(pp)
        for q_blk in range(n_q):
            o = _head_rmsnorm(acc_scr[pp, q_blk], lane) * g_ref[:, lanes]
            o_ref[q_blk * t:(q_blk + 1) * t, lanes] = o.astype(o_ref.dtype)


def _sb_attention(q, k, v, g, tri):
    bsz, s, width = q.shape
    t = SB_T
    n_q = s // t
    n_tiles = width // LANES
    tasks = []
    for qb in range(n_q):
        tasks.append((qb, qb, qb, 1))
        if qb >= 1:
            tasks.append((qb, qb - 1, qb, 0))
    table = _task_table(tasks, n_q)
    pp, hh = SB_PAIRS, HEADS_PER_TILE
    seq_spec = pl.BlockSpec((None, s, pp * LANES), lambda b, hp: (b, 0, hp))
    return pl.pallas_call(
        _sb_kernel,
        grid=(bsz, n_tiles // pp),
        in_specs=[pl.BlockSpec(memory_space=pltpu.SMEM), seq_spec, seq_spec, seq_spec,
                  pl.BlockSpec((1, pp * LANES), lambda b, hp: (0, hp)), _const_spec(tri.shape)],
        out_specs=seq_spec,
        out_shape=jax.ShapeDtypeStruct((bsz, s, width), BF16),
        scratch_shapes=[pltpu.VMEM((pp, n_q, 2 * t, LANES), BF16),
                        pltpu.VMEM((pp, hh, s, LANES), BF16),
                        pltpu.VMEM((2, t, t), F32),
                        pltpu.VMEM((pp, n_q, t, LANES), F32),
                        pltpu.VMEM((pp, n_q, hh, t, LANES), F32),
                        pltpu.VMEM((pp, hh, t, t), F32),
                        pltpu.VMEM((pp, hh, t, t), F32),
                        pltpu.VMEM((pp, hh, t, t), F32),
                        pltpu.VMEM((pp, hh, t, t), BF16),
                        pltpu.VMEM((pp, t, hh * t), BF16),
                        pltpu.VMEM((pp, hh, t, t), F32)],
        compiler_params=pltpu.CompilerParams(
            dimension_semantics=("arbitrary", "arbitrary"), vmem_limit_bytes=VMEM_LIMIT_BYTES),
        name="sb_attn",
    )(jnp.asarray(table), q, k, v, g.reshape(1, width), tri)


def _fox_kernel(tab_ref, q_ref, k_ref, v_ref, f_ref, gate_ref, g_ref, o_ref, vexp_scr, qexp_scr,
                bias_scr, acc_scr, m_scr, s_scr, p_scr, alpha_scr, bmax_scr):
    t = FOX_T
    n_q = q_ref.shape[0] // t
    pairs = range(q_ref.shape[1] // LANES)
    lane = lax.broadcasted_iota(jnp.int32, (t, LANES), 1)

    for pp in pairs:
        _fill_vexp(v_ref, _pair_lanes(pp), vexp_scr.at[pp], n_q, t)
        _fill_qexp(q_ref, _pair_lanes(pp), qexp_scr.at[pp], n_q, t)

    @pl.when(_first_grid_step())
    def _():
        acc_scr[...] = jnp.zeros_like(acc_scr)
        m_scr[...] = jnp.full_like(m_scr, -jnp.inf)
        for pp in pairs:
            _fill_vexp_ones(vexp_scr.at[pp], n_q, t)
        bias_scr[0] = jnp.zeros((t, t), F32)
        bias_scr[1] = _causal_bias(t, strict=False)

    def score_stage(pp, q_blk, k_blk, mask_idx):
        rows = pl.ds(pl.multiple_of(q_blk * t, t), t)
        cols = pl.ds(pl.multiple_of(k_blk * t, t), t)
        k = k_ref[cols, _pair_lanes(pp)]
        f_keys = f_ref[pp, :, cols] * LOG2E
        bias = bias_scr[mask_idx]
        for hd in range(HEADS_PER_TILE):
            sc = _dot_nt(qexp_scr[pp, hd, rows, :], k) - f_keys[hd:hd + 1, :] + bias
            s_scr[pp, hd] = sc
            bmax_scr[pp, hd] = jnp.broadcast_to(jnp.max(sc, axis=1, keepdims=True), (t, LANES))

    def softmax_stage(pp, slot, first):
        for hd in range(HEADS_PER_TILE):
            m_old = jnp.where(first > 0, -jnp.inf, m_scr[pp, slot, hd])
            m_new = jnp.maximum(m_old, bmax_scr[pp, hd])
            alpha_scr[pp, hd] = jnp.exp2(m_old - m_new)
            m_scr[pp, slot, hd] = m_new
            p_scr[pp, :, hd * t:(hd + 1) * t] = jnp.exp2(
                s_scr[pp, hd] - _lane_tile(m_new, t)).astype(BF16)

    def value_stage(pp, k_blk, slot):
        pv = _dot(p_scr[pp], vexp_scr[pp, k_blk])
        alpha = _per_head(lane, alpha_scr[pp, 0], alpha_scr[pp, 1])
        acc_scr[pp, slot] = acc_scr[pp, slot] * _lane_tile(alpha, 2 * LANES) + pv

    def score_all(j):
        for pp in pairs:
            score_stage(pp, tab_ref[0, j], tab_ref[1, j], tab_ref[3, j])

    def softmax_all(j):
        for pp in pairs:
            softmax_stage(pp, tab_ref[2, j], tab_ref[4, j])

    def value_all(j):
        for pp in pairs:
            value_stage(pp, tab_ref[1, j], tab_ref[2, j])

    _run_pipeline(tab_ref.shape[1], score_all, softmax_all, value_all)

    for pp in pairs:
        lanes = _pair_lanes(pp)
        for q_blk in range(n_q):
            rows = slice(q_blk * t, (q_blk + 1) * t)
            acc = acc_scr[pp, q_blk]
            o = acc[:, 0:LANES] / acc[:, LANES:2 * LANES]
            o = _head_rmsnorm(o, lane) * g_ref[:, lanes] * gate_ref[rows, lanes].astype(F32)
            o_ref[rows, lanes] = o.astype(o_ref.dtype)


def _fox_attention(q, k, v, fcum, gate, g):
    bsz, s, width = q.shape
    t = FOX_T
    n_q = s // t
    n_tiles = width // LANES
    tasks = [(qb, kb, qb, int(kb == qb)) for qb in range(n_q) for kb in range(qb + 1)]
    table = _task_table(tasks, n_q)
    pp, hh = FOX_PAIRS, HEADS_PER_TILE
    seq_spec = pl.BlockSpec((None, s, pp * LANES), lambda b, hp: (b, 0, hp))
    f_tiles = fcum.reshape(bsz, n_tiles, hh, s)
    return pl.pallas_call(
        _fox_kernel,
        grid=(bsz, n_tiles // pp),
        in_specs=[pl.BlockSpec(memory_space=pltpu.SMEM), seq_spec, seq_spec, seq_spec,
                  pl.BlockSpec((None, pp, hh, s), lambda b, hp: (b, hp, 0, 0)),
                  seq_spec, pl.BlockSpec((1, pp * LANES), lambda b, hp: (0, hp))],
        out_specs=seq_spec,
        out_shape=jax.ShapeDtypeStruct((bsz, s, width), BF16),
        scratch_shapes=[pltpu.VMEM((pp, n_q, 2 * t, 2 * LANES), BF16),
                        pltpu.VMEM((pp, hh, s, LANES), BF16),
                        pltpu.VMEM((2, t, t), F32),
                        pltpu.VMEM((pp, n_q, t, 2 * LANES), F32),
                        pltpu.VMEM((pp, n_q, hh, t, LANES), F32),
                        pltpu.VMEM((pp, hh, t, t), F32),
                        pltpu.VMEM((pp, t, hh * t), BF16),
                        pltpu.VMEM((pp, hh, t, LANES), F32),
                        pltpu.VMEM((pp, hh, t, LANES), F32)],
        compiler_params=pltpu.CompilerParams(
            dimension_semantics=("arbitrary", "arbitrary"), vmem_limit_bytes=VMEM_LIMIT_BYTES),
        name="fox_attn",
    )(jnp.asarray(table), q, k, v, f_tiles, gate, g.reshape(1, width))


def kernel(x, c, w_ada, b_ada, g_ffn1, w_ffn1_up, w_ffn1_down, g_mix, w_in, b_forget, g_fox_q,
           g_fox_k, g_sb_out, g_fox_out, w_out, g_ffn2, w_ffn2_up, w_ffn2_down, g_final):
    bsz, s, d = x.shape
    depth = w_ada.shape[0]
    key = jnp.arange(SB_T)
    tri = (key[:, None] >= key[None, :]).astype(BF16)

    n_main = 3 * SB_WIDTH + 4 * FOX_WIDTH
    for l in range(depth):
        ada = _ada(c, w_ada[l], b_ada[l]).reshape(bsz, N_SUBLAYERS, 3, 1, d)
        shift, scale, gate = ada[:, :, 0], ada[:, :, 1], ada[:, :, 2]

        x = _ffn(x, shift[:, 0], scale[:, 0], gate[:, 0], g_ffn1[l],
                 w_ffn1_up[l], w_ffn1_down[l])

        w_f = jnp.pad(w_in[l][:, n_main:], ((0, 0), (0, LANES - FOX_HEADS))).astype(BF16)
        sbq, sbk, sbv, fxq, fxk, fxv, fxg, fcum = _in_proj(
            x, shift[:, 1], scale[:, 1], g_mix[l], w_in[l], w_f,
            b_forget[l].reshape(FOX_HEADS, 1),
            jnp.tile(g_fox_q[l], FOX_HEADS).reshape(1, FOX_WIDTH),
            jnp.tile(g_fox_k[l], FOX_HEADS).reshape(1, FOX_WIDTH))

        o_sb = _sb_attention(sbq, sbk, sbv, g_sb_out[l], tri)
        o_fx = _fox_attention(fxq, fxk, fxv, fcum, fxg, g_fox_out[l])

        x = _ffn(x, shift[:, 2], scale[:, 2], gate[:, 2], g_ffn2[l],
                 w_ffn2_up[l], w_ffn2_down[l],
                 mix=(o_sb, o_fx, w_out[l], gate[:, 1]),
                 g_final=g_final if l == depth - 1 else None)
    return x
```

```python
import functools

import numpy as np

import jax
import jax.numpy as jnp
from jax import lax
from jax.experimental import pallas as pl
from jax.experimental.pallas import tpu as pltpu

HEAD_DIM = 64
SB_HEADS = 8
FOX_HEADS = 8
SB_WIDTH = SB_HEADS * HEAD_DIM
FOX_WIDTH = FOX_HEADS * HEAD_DIM
N_SUBLAYERS = 3
EPS = 1e-6
QK_SCALE = HEAD_DIM ** -0.5

LANES = 128
HEADS_PER_TILE = LANES // HEAD_DIM
VMEM_LIMIT_BYTES = 56 * 1024 * 1024

ADA_TN = 1152
FFN_TM = 512
FFN_HC = 256
PROJ_TM = 512
SB_T = 256
FOX_T = 512
SB_PAIRS = 2
FOX_PAIRS = 2

LOG2E = 1.4426950408889634
F32_EXP2_ZERO_BELOW = -150.0
MASKED = -1e30

BF16 = jnp.bfloat16
F32 = jnp.float32


def _dot(a, b):
    return jnp.dot(a, b, preferred_element_type=F32)


def _dot_nt(a, b):
    return lax.dot_general(a, b, (((1,), (1,)), ((), ())), preferred_element_type=F32)


def _sigmoid(x):
    return 1.0 / (1.0 + jnp.exp(-x))


def _softplus(x):
    return jnp.maximum(x, 0.0) + jnp.log(1.0 + jnp.exp(-jnp.abs(x)))


def _modulate(x, g, shift, scale):
    ms = jnp.mean(x * x, axis=-1, keepdims=True)
    return (x * lax.rsqrt(ms + EPS)) * (g * (1.0 + scale)) + shift


def _const_spec(shape):
    return pl.BlockSpec(shape, lambda *_: (0,) * len(shape), pipeline_mode=pl.Buffered(1))


def _ada_kernel(c_ref, w_ref, b_ref, o_ref):
    c = c_ref[...]
    cond = (c * _sigmoid(c)).astype(BF16)
    o_ref[...] = _dot(cond, w_ref[...].astype(BF16)) + b_ref[...]


def _ada(c, w, b):
    bsz, d = c.shape
    n = w.shape[1]
    return pl.pallas_call(
        _ada_kernel,
        grid=(n // ADA_TN,),
        in_specs=[
            pl.BlockSpec((bsz, d), lambda j: (0, 0)),
            pl.BlockSpec((d, ADA_TN), lambda j: (0, j)),
            pl.BlockSpec((1, ADA_TN), lambda j: (0, j)),
        ],
        out_specs=pl.BlockSpec((bsz, ADA_TN), lambda j: (0, j)),
        out_shape=jax.ShapeDtypeStruct((bsz, n), F32),
        compiler_params=pltpu.CompilerParams(
            dimension_semantics=("arbitrary",), vmem_limit_bytes=VMEM_LIMIT_BYTES),
        name="ada",
    )(c, w, b.reshape(1, n))


def _ffn_kernel(*refs, pre_mix, final_norm):
    it = iter(refs)
    x_ref = next(it)
    if pre_mix:
        osb_ref, ofx_ref, wout_ref, gate_mix_ref = next(it), next(it), next(it), next(it)
    shift_ref, scale_ref, gate_ref, g_ref = next(it), next(it), next(it), next(it)
    wup_ref, wdown_ref = next(it), next(it)
    if final_norm:
        gf_ref = next(it)
    o_ref, h_scr, acc_scr = next(it), next(it), next(it)

    x = x_ref[...]
    if pre_mix:
        mix = (_dot(osb_ref[...], wout_ref[0:SB_WIDTH, :].astype(BF16))
               + _dot(ofx_ref[...], wout_ref[SB_WIDTH:SB_WIDTH + FOX_WIDTH, :].astype(BF16)))
        x = x + gate_mix_ref[...] * mix
    o_ref[...] = x
    h_scr[...] = _modulate(x, g_ref[...], shift_ref[...], scale_ref[...]).astype(BF16)
    acc_scr[...] = jnp.zeros_like(acc_scr)

    hidden = wdown_ref.shape[0]
    for lo in range(0, hidden, FFN_HC):
        h = h_scr[...]
        gt = _dot(h, wup_ref[:, lo:lo + FFN_HC].astype(BF16))
        up = _dot(h, wup_ref[:, hidden + lo:hidden + lo + FFN_HC].astype(BF16))
        a = (gt * _sigmoid(gt) * up).astype(BF16)
        acc_scr[...] += _dot(a, wdown_ref[lo:lo + FFN_HC, :].astype(BF16))
    y = o_ref[...] + (0.5 * gate_ref[...]) * acc_scr[...]
    if final_norm:
        ms = jnp.mean(y * y, axis=-1, keepdims=True)
        y = (y * lax.rsqrt(ms + EPS)) * gf_ref[...]
    o_ref[...] = y


def _ffn(x, shift, scale, gate, g, w_up, w_down, mix=None, g_final=None):
    bsz, s, d = x.shape
    assert w_down.shape[0] % FFN_HC == 0
    tm = FFN_TM
    tok = pl.BlockSpec((None, tm, d), lambda b, i: (b, i, 0))
    per_batch = pl.BlockSpec((None, 1, d), lambda b, i: (b, 0, 0))
    args, specs = [x], [tok]
    if mix is not None:
        o_sb, o_fx, w_out, gate_mix = mix
        args += [o_sb, o_fx, w_out, gate_mix]
        specs += [pl.BlockSpec((None, tm, SB_WIDTH), lambda b, i: (b, i, 0)),
                  pl.BlockSpec((None, tm, FOX_WIDTH), lambda b, i: (b, i, 0)),
                  _const_spec(w_out.shape), per_batch]
    args += [shift, scale, gate, g.reshape(1, d), w_up, w_down]
    specs += [per_batch, per_batch, per_batch, _const_spec((1, d)),
              _const_spec(w_up.shape), _const_spec(w_down.shape)]
    if g_final is not None:
        args.append(g_final.reshape(1, d))
        specs.append(_const_spec((1, d)))
    return pl.pallas_call(
        functools.partial(_ffn_kernel, pre_mix=mix is not None, final_norm=g_final is not None),
        grid=(bsz, s // tm),
        in_specs=specs,
        out_specs=tok,
        out_shape=jax.ShapeDtypeStruct((bsz, s, d), F32),
        scratch_shapes=[pltpu.VMEM((tm, d), BF16), pltpu.VMEM((tm, d), F32)],
        compiler_params=pltpu.CompilerParams(
            dimension_semantics=("arbitrary", "arbitrary"), vmem_limit_bytes=VMEM_LIMIT_BYTES),
        name="ffn_mix" if mix is not None else "ffn",
    )(*args)


def _lane_cumsum(y):
    n = y.shape[-1]
    lane = lax.broadcasted_iota(jnp.int32, y.shape, y.ndim - 1)
    shift = 1
    while shift < n:
        y = y + jnp.where(lane >= shift, pltpu.roll(y, shift, axis=y.ndim - 1), 0.0)
        shift *= 2
    return y


def _proj_kernel(x_ref, shift_ref, scale_ref, g_ref, wt_ref, bf_ref, gq_ref, gk_ref,
                 sbq_ref, sbk_ref, sbv_ref, fxq_ref, fxk_ref, fxv_ref, fxg_ref,
                 fcum_ref, h_scr, carry_scr):
    @pl.when(pl.program_id(1) == 0)
    def _():
        carry_scr[...] = jnp.zeros_like(carry_scr)

    x = x_ref[...]
    h_scr[...] = _modulate(x, g_ref[...], shift_ref[...], scale_ref[...]).astype(BF16)
    h = h_scr[...]
    w = SB_WIDTH

    def proj(idx):
        return _dot_nt(h, wt_ref[idx * w:(idx + 1) * w, :].astype(BF16))

    def head_rms(p):
        lane = lax.broadcasted_iota(jnp.int32, (p.shape[0], LANES), 1)
        return jnp.concatenate(
            [_head_rmsnorm(p[:, c0:c0 + LANES], lane) for c0 in range(0, p.shape[1], LANES)],
            axis=1)

    n_main = 7 * w
    ft = _dot_nt(wt_ref[n_main:n_main + FOX_HEADS, :].astype(BF16), h) + bf_ref[...]
    log_f = -_softplus(-ft)
    fcum = _lane_cumsum(log_f) + carry_scr[:, 0:1]
    fcum_ref[...] = fcum
    carry_scr[...] = jnp.broadcast_to(fcum[:, fcum.shape[1] - 1:], carry_scr.shape)

    fxq_ref[...] = (head_rms(proj(3)) * (gq_ref[...] * (QK_SCALE * LOG2E))).astype(BF16)
    fxk_ref[...] = (head_rms(proj(4)) * gk_ref[...]).astype(BF16)
    fxg_ref[...] = _sigmoid(proj(6)).astype(BF16)
    sbq_ref[...] = (proj(0) * (QK_SCALE * LOG2E)).astype(BF16)
    sbk_ref[...] = proj(1).astype(BF16)
    sbv_ref[...] = proj(2).astype(BF16)
    fxv_ref[...] = proj(5).astype(BF16)


def _in_proj(x, shift, scale, g, w_t, b_f, gq, gk):
    bsz, s, d = x.shape
    tm = PROJ_TM
    tok = pl.BlockSpec((None, tm, d), lambda b, i: (b, i, 0))
    per_batch = pl.BlockSpec((None, 1, d), lambda b, i: (b, 0, 0))
    head_out = pl.BlockSpec((None, tm, SB_WIDTH), lambda b, i: (b, i, 0))
    head_shape = jax.ShapeDtypeStruct((bsz, s, SB_WIDTH), BF16)
    return pl.pallas_call(
        _proj_kernel,
        grid=(bsz, s // tm),
        in_specs=[tok, per_batch, per_batch, _const_spec((1, d)), _const_spec(w_t.shape),
                  _const_spec(b_f.shape), _const_spec(gq.shape), _const_spec(gk.shape)],
        out_specs=[head_out] * 7 + [pl.BlockSpec((None, FOX_HEADS, tm), lambda b, i: (b, 0, i))],
        out_shape=[head_shape] * 7 + [jax.ShapeDtypeStruct((bsz, FOX_HEADS, s), F32)],
        scratch_shapes=[pltpu.VMEM((tm, d), BF16), pltpu.VMEM((FOX_HEADS, LANES), F32)],
        compiler_params=pltpu.CompilerParams(
            dimension_semantics=("arbitrary", "arbitrary"), vmem_limit_bytes=VMEM_LIMIT_BYTES),
        name="in_proj",
    )(x, shift, scale, g.reshape(1, d), w_t, b_f, gq, gk)


def _split_heads(x):
    lane = lax.broadcasted_iota(jnp.int32, x.shape, 1)
    zero = jnp.zeros_like(x)
    return jnp.where(lane < HEAD_DIM, x, zero), jnp.where(lane >= HEAD_DIM, x, zero)


def _pair_lanes(pp):
    return slice(pp * LANES, (pp + 1) * LANES)


def _fill_vexp(v_ref, lanes, vexp_ref, n_blocks, t):
    def body(j, carry):
        v0, v1 = _split_heads(v_ref[pl.ds(pl.multiple_of(j * t, t), t), lanes])
        vexp_ref[j, 0:t, 0:LANES] = v0
        vexp_ref[j, t:2 * t, 0:LANES] = v1
        return carry

    lax.fori_loop(0, n_blocks, body, 0)


def _fill_vexp_ones(vexp_ref, n_blocks, t):
    def body(j, carry):
        lane = lax.broadcasted_iota(jnp.int32, (t, LANES), 1)
        vexp_ref[j, 0:t, LANES:2 * LANES] = (lane < HEAD_DIM).astype(F32).astype(vexp_ref.dtype)
        vexp_ref[j, t:2 * t, LANES:2 * LANES] = (lane >= HEAD_DIM).astype(F32).astype(
            vexp_ref.dtype)
        return carry

    lax.fori_loop(0, n_blocks, body, 0)


def _per_head(lane, a0, a1):
    return jnp.where(lane < HEAD_DIM, a0, a1)


def _lane_tile(a, width):
    return jnp.concatenate([a] * (width // LANES), axis=1)


def _head_rmsnorm(o, lane):
    sq = o * o
    ss0 = jnp.sum(jnp.where(lane < HEAD_DIM, sq, 0.0), axis=1, keepdims=True)
    ss1 = jnp.sum(jnp.where(lane >= HEAD_DIM, sq, 0.0), axis=1, keepdims=True)
    inv = _per_head(lane, lax.rsqrt(ss0 * (1.0 / HEAD_DIM) + EPS),
                    lax.rsqrt(ss1 * (1.0 / HEAD_DIM) + EPS))
    return o * inv


def _task_table(tasks, n_q):
    seen = set()
    rows = []
    for q_blk, k_blk, slot, mask_idx in tasks:
        rows.append((q_blk, k_blk, slot, mask_idx, int(slot not in seen)))
        seen.add(slot)
    return np.asarray(rows, dtype=np.int32).T.copy()


def _run_pipeline(n_tasks, score, middle, value):
    assert n_tasks >= 2
    score(0)
    middle(0)
    score(1)

    def body(j, carry):
        value(j - 2)
        middle(j - 1)
        score(j)
        return carry

    lax.fori_loop(2, n_tasks, body, 0)
    value(n_tasks - 2)
    middle(n_tasks - 1)
    value(n_tasks - 1)


def _fill_qexp(q_ref, lanes, qexp_ref, n_blocks, t):
    def body(j, carry):
        rows = pl.ds(pl.multiple_of(j * t, t), t)
        q0, q1 = _split_heads(q_ref[rows, lanes])
        qexp_ref[0, rows, :] = q0
        qexp_ref[1, rows, :] = q1
        return carry

    lax.fori_loop(0, n_blocks, body, 0)


def _first_grid_step():
    return jnp.logical_and(pl.program_id(0) == 0, pl.program_id(1) == 0)


def _causal_bias(t, strict):
    row = lax.broadcasted_iota(jnp.int32, (t, t), 0)
    col = lax.broadcasted_iota(jnp.int32, (t, t), 1)
    keep = col < row if strict else col <= row
    return jnp.where(keep, 0.0, MASKED).astype(F32)


def _sb_kernel(tab_ref, q_ref, k_ref, v_ref, g_ref, tri_ref, o_ref, vexp_scr, qexp_scr, bias_scr,
               acc_scr, carry_scr, z_scr, zc_scr, suf_scr, sp_scr, w_scr, l_scr):
    t = SB_T
    n_q = q_ref.shape[0] // t
    pairs = range(q_ref.shape[1] // LANES)

    for pp in pairs:
        _fill_vexp(v_ref, _pair_lanes(pp), vexp_scr.at[pp], n_q, t)
        _fill_qexp(q_ref, _pair_lanes(pp), qexp_scr.at[pp], n_q, t)

    @pl.when(_first_grid_step())
    def _():
        acc_scr[...] = jnp.zeros_like(acc_scr)
        carry_scr[...] = jnp.zeros_like(carry_scr)
        bias_scr[0] = jnp.zeros((t, t), F32)
        bias_scr[1] = _causal_bias(t, strict=True)

    def score_stage(pp, q_blk, k_blk, mask_idx):
        rows = pl.ds(pl.multiple_of(q_blk * t, t), t)
        k = k_ref[pl.ds(pl.multiple_of(k_blk * t, t), t), _pair_lanes(pp)]
        bias = bias_scr[mask_idx]
        for hd in range(HEADS_PER_TILE):
            z = _dot_nt(qexp_scr[pp, hd, rows, :], k) + bias
            z_scr[pp, hd] = z
            l_scr[pp, hd] = jnp.log2(1.0 + jnp.exp2(-jnp.abs(z)))

    def keep_stage(pp, slot, first):
        for hd in range(HEADS_PER_TILE):
            carry = jnp.where(first > 0, 0.0, carry_scr[pp, slot, hd])
            z = z_scr[pp, hd]
            sp_scr[pp, hd] = (jnp.maximum(z, 0.0) + l_scr[pp, hd]).astype(BF16)
            zc_scr[pp, hd] = z - _lane_tile(carry, t)
            suf = _dot(sp_scr[pp, hd], tri_ref[...])
            suf_scr[pp, hd] = suf
            carry_scr[pp, slot, hd] = carry + suf[:, 0:1]

    def value_stage(pp, k_blk, slot, first):
        for hd in range(HEADS_PER_TILE):
            w_scr[pp, :, hd * t:(hd + 1) * t] = jnp.exp2(
                zc_scr[pp, hd] - suf_scr[pp, hd]).astype(BF16)
        acc = jnp.where(first > 0, 0.0, acc_scr[pp, slot])
        acc_scr[pp, slot] = acc + _dot(w_scr[pp], vexp_scr[pp, k_blk])

    def score_all(j):
        for pp in pairs:
            score_stage(pp, tab_ref[0, j], tab_ref[1, j], tab_ref[3, j])

    def keep_all(j):
        for pp in pairs:
            keep_stage(pp, tab_ref[2, j], tab_ref[4, j])

    def value_all(j):
        for pp in pairs:
            value_stage(pp, tab_ref[1, j], tab_ref[2, j], tab_ref[4, j])

    _run_pipeline(tab_ref.shape[1], score_all, keep_all, value_all)

    def tail(pp, q_blk):
        def live():
            return (jnp.min(carry_scr[pp, q_blk]) <= -F32_EXP2_ZERO_BELOW).astype(jnp.int32)

        def cond(state):
            k_blk, go = state
            return jnp.logical_and(k_blk >= 0, go > 0)

        def step(state):
            k_blk, _ = state
            score_stage(pp, q_blk, k_blk, 0)
            keep_stage(pp, q_blk, 0)
            value_stage(pp, k_blk, q_blk, 0)
            return k_blk - 1, live()

        lax.while_loop(cond, step, (q_blk - 2, live()))

    lane = lax.broadcasted_iota(jnp.int32, (t, LANES), 1)
    for pp in pairs:
        def tail_body(q_blk, carry, pp=pp):
            tail(pp, q_blk)
            return carry

        @pl.when(jnp.min(carry_scr[pp, 2:n_q]) <= -F32_EXP2_ZERO_BELOW)
        def _():
            lax.fori_loop(2, n_q, tail_body, 0)

        lanes = _pair_lanes(pp)
        for q_blk in range(n_q):
            o = _head_rmsnorm(acc_scr[pp, q_blk], lane) * g_ref[:, lanes]
            o_ref[q_blk * t:(q_blk + 1) * t, lanes] = o.astype(o_ref.dtype)


def _sb_attention(q, k, v, g, tri):
    bsz, s, width = q.shape
    t = SB_T
    n_q = s // t
    n_tiles = width // LANES
    tasks = []
    for qb in range(n_q):
        tasks.append((qb, qb, qb, 1))
        if qb >= 1:
            tasks.append((qb, qb - 1, qb, 0))
    table = _task_table(tasks, n_q)
    pp, hh = SB_PAIRS, HEADS_PER_TILE
    seq_spec = pl.BlockSpec((None, s, pp * LANES), lambda b, hp: (b, 0, hp))
    return pl.pallas_call(
        _sb_kernel,
        grid=(bsz, n_tiles // pp),
        in_specs=[pl.BlockSpec(memory_space=pltpu.SMEM), seq_spec, seq_spec, seq_spec,
                  pl.BlockSpec((1, pp * LANES), lambda b, hp: (0, hp)), _const_spec(tri.shape)],
        out_specs=seq_spec,
        out_shape=jax.ShapeDtypeStruct((bsz, s, width), BF16),
        scratch_shapes=[pltpu.VMEM((pp, n_q, 2 * t, LANES), BF16),
                        pltpu.VMEM((pp, hh, s, LANES), BF16),
                        pltpu.VMEM((2, t, t), F32),
                        pltpu.VMEM((pp, n_q, t, LANES), F32),
                        pltpu.VMEM((pp, n_q, hh, t, LANES), F32),
                        pltpu.VMEM((pp, hh, t, t), F32),
                        pltpu.VMEM((pp, hh, t, t), F32),
                        pltpu.VMEM((pp, hh, t, t), F32),
                        pltpu.VMEM((pp, hh, t, t), BF16),
                        pltpu.VMEM((pp, t, hh * t), BF16),
                        pltpu.VMEM((pp, hh, t, t), F32)],
        compiler_params=pltpu.CompilerParams(
            dimension_semantics=("arbitrary", "arbitrary"), vmem_limit_bytes=VMEM_LIMIT_BYTES),
        name="sb_attn",
    )(jnp.asarray(table), q, k, v, g.reshape(1, width), tri)


def _fox_kernel(tab_ref, q_ref, k_ref, v_ref, f_ref, gate_ref, g_ref, o_ref, vexp_scr, qexp_scr,
                bias_scr, acc_scr, m_scr, s_scr, p_scr, alpha_scr, bmax_scr):
    t = FOX_T
    n_q = q_ref.shape[0] // t
    pairs = range(q_ref.shape[1] // LANES)
    lane = lax.broadcasted_iota(jnp.int32, (t, LANES), 1)

    for pp in pairs:
        _fill_vexp(v_ref, _pair_lanes(pp), vexp_scr.at[pp], n_q, t)
        _fill_qexp(q_ref, _pair_lanes(pp), qexp_scr.at[pp], n_q, t)

    @pl.when(_first_grid_step())
    def _():
        acc_scr[...] = jnp.zeros_like(acc_scr)
        m_scr[...] = jnp.full_like(m_scr, -jnp.inf)
        for pp in pairs:
            _fill_vexp_ones(vexp_scr.at[pp], n_q, t)
        bias_scr[0] = jnp.zeros((t, t), F32)
        bias_scr[1] = _causal_bias(t, strict=False)

    def score_stage(pp, q_blk, k_blk, mask_idx):
        rows = pl.ds(pl.multiple_of(q_blk * t, t), t)
        cols = pl.ds(pl.multiple_of(k_blk * t, t), t)
        k = k_ref[cols, _pair_lanes(pp)]
        f_keys = f_ref[pp, :, cols] * LOG2E
        bias = bias_scr[mask_idx]
        for hd in range(HEADS_PER_TILE):
            sc = _dot_nt(qexp_scr[pp, hd, rows, :], k) - f_keys[hd:hd + 1, :] + bias
            s_scr[pp, hd] = sc
            bmax_scr[pp, hd] = jnp.broadcast_to(jnp.max(sc, axis=1, keepdims=True), (t, LANES))

    def softmax_stage(pp, slot, first):
        for hd in range(HEADS_PER_TILE):
            m_old = jnp.where(first > 0, -jnp.inf, m_scr[pp, slot, hd])
            m_new = jnp.maximum(m_old, bmax_scr[pp, hd])
            alpha_scr[pp, hd] = jnp.exp2(m_old - m_new)
            m_scr[pp, slot, hd] = m_new
            p_scr[pp, :, hd * t:(hd + 1) * t] = jnp.exp2(
                s_scr[pp, hd] - _lane_tile(m_new, t)).astype(BF16)

    def value_stage(pp, k_blk, slot):
        pv = _dot(p_scr[pp], vexp_scr[pp, k_blk])
        alpha = _per_head(lane, alpha_scr[pp, 0], alpha_scr[pp, 1])
        acc_scr[pp, slot] = acc_scr[pp, slot] * _lane_tile(alpha, 2 * LANES) + pv

    def score_all(j):
        for pp in pairs:
            score_stage(pp, tab_ref[0, j], tab_ref[1, j], tab_ref[3, j])

    def softmax_all(j):
        for pp in pairs:
            softmax_stage(pp, tab_ref[2, j], tab_ref[4, j])

    def value_all(j):
        for pp in pairs:
            value_stage(pp, tab_ref[1, j], tab_ref[2, j])

    _run_pipeline(tab_ref.shape[1], score_all, softmax_all, value_all)

    for pp in pairs:
        lanes = _pair_lanes(pp)
        for q_blk in range(n_q):
            rows = slice(q_blk * t, (q_blk + 1) * t)
            acc = acc_scr[pp, q_blk]
            o = acc[:, 0:LANES] / acc[:, LANES:2 * LANES]
            o = _head_rmsnorm(o, lane) * g_ref[:, lanes] * gate_ref[rows, lanes].astype(F32)
            o_ref[rows, lanes] = o.astype(o_ref.dtype)


def _fox_attention(q, k, v, fcum, gate, g):
    bsz, s, width = q.shape
    t = FOX_T
    n_q = s // t
    n_tiles = width // LANES
    tasks = [(qb, kb, qb, int(kb == qb)) for qb in range(n_q) for kb in range(qb + 1)]
    table = _task_table(tasks, n_q)
    pp, hh = FOX_PAIRS, HEADS_PER_TILE
    seq_spec = pl.BlockSpec((None, s, pp * LANES), lambda b, hp: (b, 0, hp))
    f_tiles = fcum.reshape(bsz, n_tiles, hh, s)
    return pl.pallas_call(
        _fox_kernel,
        grid=(bsz, n_tiles // pp),
        in_specs=[pl.BlockSpec(memory_space=pltpu.SMEM), seq_spec, seq_spec, seq_spec,
                  pl.BlockSpec((None, pp, hh, s), lambda b, hp: (b, hp, 0, 0)),
                  seq_spec, pl.BlockSpec((1, pp * LANES), lambda b, hp: (0, hp))],
        out_specs=seq_spec,
        out_shape=jax.ShapeDtypeStruct((bsz, s, width), BF16),
        scratch_shapes=[pltpu.VMEM((pp, n_q, 2 * t, 2 * LANES), BF16),
                        pltpu.VMEM((pp, hh, s, LANES), BF16),
                        pltpu.VMEM((2, t, t), F32),
                        pltpu.VMEM((pp, n_q, t, 2 * LANES), F32),
                        pltpu.VMEM((pp, n_q, hh, t, LANES), F32),
                        pltpu.VMEM((pp, hh, t, t), F32),
                        pltpu.VMEM((pp, t, hh * t), BF16),
                        pltpu.VMEM((pp, hh, t, LANES), F32),
                        pltpu.VMEM((pp, hh, t, LANES), F32)],
        compiler_params=pltpu.CompilerParams(
            dimension_semantics=("arbitrary", "arbitrary"), vmem_limit_bytes=VMEM_LIMIT_BYTES),
        name="fox_attn",
    )(jnp.asarray(table), q, k, v, f_tiles, gate, g.reshape(1, width))


def kernel(x, c, w_ada, b_ada, g_ffn1, w_ffn1_up, w_ffn1_down, g_mix, w_in, b_forget, g_fox_q,
           g_fox_k, g_sb_out, g_fox_out, w_out, g_ffn2, w_ffn2_up, w_ffn2_down, g_final):
    bsz, s, d = x.shape
    depth = w_ada.shape[0]
    key = jnp.arange(SB_T)
    tri = (key[:, None] >= key[None, :]).astype(BF16)

    assert w_in.shape[2] == 3 * SB_WIDTH + 4 * FOX_WIDTH + FOX_HEADS
    for l in range(depth):
        ada = _ada(c, w_ada[l], b_ada[l]).reshape(bsz, N_SUBLAYERS, 3, 1, d)
        shift, scale, gate = ada[:, :, 0], ada[:, :, 1], ada[:, :, 2]

        x = _ffn(x, shift[:, 0], scale[:, 0], gate[:, 0], g_ffn1[l],
                 w_ffn1_up[l], w_ffn1_down[l])

        sbq, sbk, sbv, fxq, fxk, fxv, fxg, fcum = _in_proj(
            x, shift[:, 1], scale[:, 1], g_mix[l], jnp.swapaxes(w_in[l], 0, 1),
            b_forget[l].reshape(FOX_HEADS, 1),
            jnp.tile(g_fox_q[l], FOX_HEADS).reshape(1, FOX_WIDTH),
            jnp.tile(g_fox_k[l], FOX_HEADS).reshape(1, FOX_WIDTH))

        o_sb = _sb_attention(sbq, sbk, sbv, g_sb_out[l], tri)
        o_fx = _fox_attention(fxq, fxk, fxv, fcum, fxg, g_fox_out[l])

        x = _ffn(x, shift[:, 2], scale[:, 2], gate[:, 2], g_ffn2[l],
                 w_ffn2_up[l], w_ffn2_down[l],
                 mix=(o_sb, o_fx, w_out[l], gate[:, 1]),
                 g_final=g_final if l == depth - 1 else None)
    return x
```

```python
import functools

import numpy as np

import jax
import jax.numpy as jnp
from jax import lax
from jax.experimental import pallas as pl
from jax.experimental.pallas import tpu as pltpu

HEAD_DIM = 64
SB_HEADS = 8
FOX_HEADS = 8
SB_WIDTH = SB_HEADS * HEAD_DIM
FOX_WIDTH = FOX_HEADS * HEAD_DIM
N_SUBLAYERS = 3
EPS = 1e-6
QK_SCALE = HEAD_DIM ** -0.5

LANES = 128
HEADS_PER_TILE = LANES // HEAD_DIM
VMEM_LIMIT_BYTES = 56 * 1024 * 1024

ADA_TN = 1152
FFN_TM = 512
FFN_HC = 256
PROJ_TM = 512
SB_T = 256
FOX_T = 512
SB_PAIRS = 2
FOX_PAIRS = 2

LOG2E = 1.4426950408889634
F32_EXP2_ZERO_BELOW = -150.0
MASKED = -1e30

BF16 = jnp.bfloat16
F32 = jnp.float32


def _dot(a, b):
    return jnp.dot(a, b, preferred_element_type=F32)


def _dot_nt(a, b):
    return lax.dot_general(a, b, (((1,), (1,)), ((), ())), preferred_element_type=F32)


def _sigmoid(x):
    return 1.0 / (1.0 + jnp.exp(-x))


def _softplus(x):
    return jnp.maximum(x, 0.0) + jnp.log(1.0 + jnp.exp(-jnp.abs(x)))


def _modulate(x, g, shift, scale):
    ms = jnp.mean(x * x, axis=-1, keepdims=True)
    return (x * lax.rsqrt(ms + EPS)) * (g * (1.0 + scale)) + shift


def _const_spec(shape):
    return pl.BlockSpec(shape, lambda *_: (0,) * len(shape), pipeline_mode=pl.Buffered(1))


def _ada_kernel(c_ref, w_ref, b_ref, o_ref):
    c = c_ref[...]
    cond = (c * _sigmoid(c)).astype(BF16)
    o_ref[...] = _dot(cond, w_ref[...].astype(BF16)) + b_ref[...]


def _ada(c, w, b):
    bsz, d = c.shape
    n = w.shape[1]
    return pl.pallas_call(
        _ada_kernel,
        grid=(n // ADA_TN,),
        in_specs=[
            pl.BlockSpec((bsz, d), lambda j: (0, 0)),
            pl.BlockSpec((d, ADA_TN), lambda j: (0, j)),
            pl.BlockSpec((1, ADA_TN), lambda j: (0, j)),
        ],
        out_specs=pl.BlockSpec((bsz, ADA_TN), lambda j: (0, j)),
        out_shape=jax.ShapeDtypeStruct((bsz, n), F32),
        compiler_params=pltpu.CompilerParams(
            dimension_semantics=("arbitrary",), vmem_limit_bytes=VMEM_LIMIT_BYTES),
        name="ada",
    )(c, w, b.reshape(1, n))


def _ffn_kernel(*refs, pre_mix, final_norm):
    it = iter(refs)
    x_ref = next(it)
    if pre_mix:
        osb_ref, ofx_ref, wout_ref, gate_mix_ref = next(it), next(it), next(it), next(it)
    shift_ref, scale_ref, gate_ref, g_ref = next(it), next(it), next(it), next(it)
    wup_ref, wdown_ref = next(it), next(it)
    if final_norm:
        gf_ref = next(it)
    o_ref, h_scr, act_scr = next(it), next(it), next(it)

    x = x_ref[...]
    if pre_mix:
        mix = (_dot(osb_ref[...], wout_ref[0:SB_WIDTH, :].astype(BF16))
               + _dot(ofx_ref[...], wout_ref[SB_WIDTH:SB_WIDTH + FOX_WIDTH, :].astype(BF16)))
        x = x + gate_mix_ref[...] * mix
    o_ref[...] = x
    h_scr[...] = _modulate(x, g_ref[...], shift_ref[...], scale_ref[...]).astype(BF16)
    hidden = wdown_ref.shape[0]
    for lo in range(0, hidden, FFN_HC):
        h = h_scr[...]
        gt = _dot(h, wup_ref[:, lo:lo + FFN_HC].astype(BF16))
        up = _dot(h, wup_ref[:, hidden + lo:hidden + lo + FFN_HC].astype(BF16))
        act_scr[:, lo:lo + FFN_HC] = (gt * _sigmoid(gt) * up).astype(BF16)
    y = o_ref[...] + (0.5 * gate_ref[...]) * _dot(act_scr[...], wdown_ref[...].astype(BF16))
    if final_norm:
        ms = jnp.mean(y * y, axis=-1, keepdims=True)
        y = (y * lax.rsqrt(ms + EPS)) * gf_ref[...]
    o_ref[...] = y


def _ffn(x, shift, scale, gate, g, w_up, w_down, mix=None, g_final=None):
    bsz, s, d = x.shape
    assert w_down.shape[0] % FFN_HC == 0
    tm = FFN_TM
    tok = pl.BlockSpec((None, tm, d), lambda b, i: (b, i, 0))
    per_batch = pl.BlockSpec((None, 1, d), lambda b, i: (b, 0, 0))
    args, specs = [x], [tok]
    if mix is not None:
        o_sb, o_fx, w_out, gate_mix = mix
        args += [o_sb, o_fx, w_out, gate_mix]
        specs += [pl.BlockSpec((None, tm, SB_WIDTH), lambda b, i: (b, i, 0)),
                  pl.BlockSpec((None, tm, FOX_WIDTH), lambda b, i: (b, i, 0)),
                  _const_spec(w_out.shape), per_batch]
    args += [shift, scale, gate, g.reshape(1, d), w_up, w_down]
    specs += [per_batch, per_batch, per_batch, _const_spec((1, d)),
              _const_spec(w_up.shape), _const_spec(w_down.shape)]
    if g_final is not None:
        args.append(g_final.reshape(1, d))
        specs.append(_const_spec((1, d)))
    return pl.pallas_call(
        functools.partial(_ffn_kernel, pre_mix=mix is not None, final_norm=g_final is not None),
        grid=(bsz, s // tm),
        in_specs=specs,
        out_specs=tok,
        out_shape=jax.ShapeDtypeStruct((bsz, s, d), F32),
        scratch_shapes=[pltpu.VMEM((tm, d), BF16), pltpu.VMEM((tm, w_down.shape[0]), BF16)],
        compiler_params=pltpu.CompilerParams(
            dimension_semantics=("arbitrary", "arbitrary"), vmem_limit_bytes=VMEM_LIMIT_BYTES),
        name="ffn_mix" if mix is not None else "ffn",
    )(*args)


def _lane_cumsum(y):
    n = y.shape[-1]
    lane = lax.broadcasted_iota(jnp.int32, y.shape, y.ndim - 1)
    shift = 1
    while shift < n:
        y = y + jnp.where(lane >= shift, pltpu.roll(y, shift, axis=y.ndim - 1), 0.0)
        shift *= 2
    return y


def _proj_kernel(x_ref, shift_ref, scale_ref, g_ref, wt_ref, bf_ref, gq_ref, gk_ref,
                 sbq_ref, sbk_ref, sbv_ref, fxq_ref, fxk_ref, fxv_ref, fxg_ref,
                 fcum_ref, h_scr, carry_scr):
    @pl.when(pl.program_id(1) == 0)
    def _():
        carry_scr[...] = jnp.zeros_like(carry_scr)

    x = x_ref[...]
    h_scr[...] = _modulate(x, g_ref[...], shift_ref[...], scale_ref[...]).astype(BF16)
    h = h_scr[...]
    w = SB_WIDTH

    def proj(idx):
        return _dot_nt(h, wt_ref[idx * w:(idx + 1) * w, :].astype(BF16))

    def head_rms(p):
        lane = lax.broadcasted_iota(jnp.int32, (p.shape[0], LANES), 1)
        return jnp.concatenate(
            [_head_rmsnorm(p[:, c0:c0 + LANES], lane) for c0 in range(0, p.shape[1], LANES)],
            axis=1)

    n_main = 7 * w
    ft = _dot_nt(wt_ref[n_main:n_main + FOX_HEADS, :].astype(BF16), h) + bf_ref[...]
    log_f = -_softplus(-ft)
    fcum = _lane_cumsum(log_f) + carry_scr[:, 0:1]
    fcum_ref[...] = fcum
    carry_scr[...] = jnp.broadcast_to(fcum[:, fcum.shape[1] - 1:], carry_scr.shape)

    fxq_ref[...] = (head_rms(proj(3)) * (gq_ref[...] * (QK_SCALE * LOG2E))).astype(BF16)
    fxk_ref[...] = (head_rms(proj(4)) * gk_ref[...]).astype(BF16)
    fxg_ref[...] = _sigmoid(proj(6)).astype(BF16)
    sbq_ref[...] = (proj(0) * (QK_SCALE * LOG2E)).astype(BF16)
    sbk_ref[...] = proj(1).astype(BF16)
    sbv_ref[...] = proj(2).astype(BF16)
    fxv_ref[...] = proj(5).astype(BF16)


def _in_proj(x, shift, scale, g, w_t, b_f, gq, gk):
    bsz, s, d = x.shape
    tm = PROJ_TM
    tok = pl.BlockSpec((None, tm, d), lambda b, i: (b, i, 0))
    per_batch = pl.BlockSpec((None, 1, d), lambda b, i: (b, 0, 0))
    head_out = pl.BlockSpec((None, tm, SB_WIDTH), lambda b, i: (b, i, 0))
    head_shape = jax.ShapeDtypeStruct((bsz, s, SB_WIDTH), BF16)
    return pl.pallas_call(
        _proj_kernel,
        grid=(bsz, s // tm),
        in_specs=[tok, per_batch, per_batch, _const_spec((1, d)), _const_spec(w_t.shape),
                  _const_spec(b_f.shape), _const_spec(gq.shape), _const_spec(gk.shape)],
        out_specs=[head_out] * 7 + [pl.BlockSpec((None, FOX_HEADS, tm), lambda b, i: (b, 0, i))],
        out_shape=[head_shape] * 7 + [jax.ShapeDtypeStruct((bsz, FOX_HEADS, s), F32)],
        scratch_shapes=[pltpu.VMEM((tm, d), BF16), pltpu.VMEM((FOX_HEADS, LANES), F32)],
        compiler_params=pltpu.CompilerParams(
            dimension_semantics=("arbitrary", "arbitrary"), vmem_limit_bytes=VMEM_LIMIT_BYTES),
        name="in_proj",
    )(x, shift, scale, g.reshape(1, d), w_t, b_f, gq, gk)


def _split_heads(x):
    lane = lax.broadcasted_iota(jnp.int32, x.shape, 1)
    zero = jnp.zeros_like(x)
    return jnp.where(lane < HEAD_DIM, x, zero), jnp.where(lane >= HEAD_DIM, x, zero)


def _pair_lanes(pp):
    return slice(pp * LANES, (pp + 1) * LANES)


def _fill_vexp(v_ref, lanes, vexp_ref, n_blocks, t):
    def body(j, carry):
        v0, v1 = _split_heads(v_ref[pl.ds(pl.multiple_of(j * t, t), t), lanes])
        vexp_ref[j, 0:t, 0:LANES] = v0
        vexp_ref[j, t:2 * t, 0:LANES] = v1
        return carry

    lax.fori_loop(0, n_blocks, body, 0)


def _fill_vexp_ones(vexp_ref, n_blocks, t):
    def body(j, carry):
        lane = lax.broadcasted_iota(jnp.int32, (t, LANES), 1)
        vexp_ref[j, 0:t, LANES:2 * LANES] = (lane < HEAD_DIM).astype(F32).astype(vexp_ref.dtype)
        vexp_ref[j, t:2 * t, LANES:2 * LANES] = (lane >= HEAD_DIM).astype(F32).astype(
            vexp_ref.dtype)
        return carry

    lax.fori_loop(0, n_blocks, body, 0)


def _per_head(lane, a0, a1):
    return jnp.where(lane < HEAD_DIM, a0, a1)


def _lane_tile(a, width):
    return jnp.concatenate([a] * (width // LANES), axis=1)


def _head_rmsnorm(o, lane):
    sq = o * o
    ss0 = jnp.sum(jnp.where(lane < HEAD_DIM, sq, 0.0), axis=1, keepdims=True)
    ss1 = jnp.sum(jnp.where(lane >= HEAD_DIM, sq, 0.0), axis=1, keepdims=True)
    inv = _per_head(lane, lax.rsqrt(ss0 * (1.0 / HEAD_DIM) + EPS),
                    lax.rsqrt(ss1 * (1.0 / HEAD_DIM) + EPS))
    return o * inv


def _task_table(tasks, n_q):
    seen = set()
    rows = []
    for q_blk, k_blk, slot, mask_idx in tasks:
        rows.append((q_blk, k_blk, slot, mask_idx, int(slot not in seen)))
        seen.add(slot)
    return np.asarray(rows, dtype=np.int32).T.copy()


def _run_pipeline(n_tasks, score, middle, value):
    assert n_tasks >= 2
    score(0)
    middle(0)
    score(1)

    def body(j, carry):
        value(j - 2)
        middle(j - 1)
        score(j)
        return carry

    lax.fori_loop(2, n_tasks, body, 0)
    value(n_tasks - 2)
    middle(n_tasks - 1)
    value(n_tasks - 1)


def _fill_qexp(q_ref, lanes, qexp_ref, n_blocks, t):
    def body(j, carry):
        rows = pl.ds(pl.multiple_of(j * t, t), t)
        q0, q1 = _split_heads(q_ref[rows, lanes])
        qexp_ref[0, rows, :] = q0
        qexp_ref[1, rows, :] = q1
        return carry

    lax.fori_loop(0, n_blocks, body, 0)


def _first_grid_step():
    return jnp.logical_and(pl.program_id(0) == 0, pl.program_id(1) == 0)


def _causal_bias(t, strict):
    row = lax.broadcasted_iota(jnp.int32, (t, t), 0)
    col = lax.broadcasted_iota(jnp.int32, (t, t), 1)
    keep = col < row if strict else col <= row
    return jnp.where(keep, 0.0, MASKED).astype(F32)


def _sb_kernel(tab_ref, q_ref, k_ref, v_ref, g_ref, tri_ref, o_ref, vexp_scr, qexp_scr, bias_scr,
               acc_scr, carry_scr, z_scr, zc_scr, suf_scr, sp_scr, w_scr, l_scr):
    t = SB_T
    n_q = q_ref.shape[0] // t
    pairs = range(q_ref.shape[1] // LANES)

    for pp in pairs:
        _fill_vexp(v_ref, _pair_lanes(pp), vexp_scr.at[pp], n_q, t)
        _fill_qexp(q_ref, _pair_lanes(pp), qexp_scr.at[pp], n_q, t)

    @pl.when(_first_grid_step())
    def _():
        acc_scr[...] = jnp.zeros_like(acc_scr)
        carry_scr[...] = jnp.zeros_like(carry_scr)
        bias_scr[0] = jnp.zeros((t, t), F32)
        bias_scr[1] = _causal_bias(t, strict=True)

    def score_stage(pp, q_blk, k_blk, mask_idx):
        rows = pl.ds(pl.multiple_of(q_blk * t, t), t)
        k = k_ref[pl.ds(pl.multiple_of(k_blk * t, t), t), _pair_lanes(pp)]
        bias = bias_scr[mask_idx]
        for hd in range(HEADS_PER_TILE):
            z = _dot_nt(qexp_scr[pp, hd, rows, :], k) + bias
            z_scr[pp, hd] = z
            l_scr[pp, hd] = jnp.log2(1.0 + jnp.exp2(-jnp.abs(z)))

    def keep_stage(pp, slot, first):
        for hd in range(HEADS_PER_TILE):
            carry = jnp.where(first > 0, 0.0, carry_scr[pp, slot, hd])
            z = z_scr[pp, hd]
            sp_scr[pp, hd] = (jnp.maximum(z, 0.0) + l_scr[pp, hd]).astype(BF16)
            zc_scr[pp, hd] = z - _lane_tile(carry, t)
            suf = _dot(sp_scr[pp, hd], tri_ref[...])
            suf_scr[pp, hd] = suf
            carry_scr[pp, slot, hd] = carry + suf[:, 0:1]

    def value_stage(pp, k_blk, slot, first):
        for hd in range(HEADS_PER_TILE):
            w_scr[pp, :, hd * t:(hd + 1) * t] = jnp.exp2(
                zc_scr[pp, hd] - suf_scr[pp, hd]).astype(BF16)
        acc = jnp.where(first > 0, 0.0, acc_scr[pp, slot])
        acc_scr[pp, slot] = acc + _dot(w_scr[pp], vexp_scr[pp, k_blk])

    def score_all(j):
        for pp in pairs:
            score_stage(pp, tab_ref[0, j], tab_ref[1, j], tab_ref[3, j])

    def keep_all(j):
        for pp in pairs:
            keep_stage(pp, tab_ref[2, j], tab_ref[4, j])

    def value_all(j):
        for pp in pairs:
            value_stage(pp, tab_ref[1, j], tab_ref[2, j], tab_ref[4, j])

    _run_pipeline(tab_ref.shape[1], score_all, keep_all, value_all)

    def tail(pp, q_blk):
        def live():
            return (jnp.min(carry_scr[pp, q_blk]) <= -F32_EXP2_ZERO_BELOW).astype(jnp.int32)

        def cond(state):
            k_blk, go = state
            return jnp.logical_and(k_blk >= 0, go > 0)

        def step(state):
            k_blk, _ = state
            score_stage(pp, q_blk, k_blk, 0)
            keep_stage(pp, q_blk, 0)
            value_stage(pp, k_blk, q_blk, 0)
            return k_blk - 1, live()

        lax.while_loop(cond, step, (q_blk - 2, live()))

    lane = lax.broadcasted_iota(jnp.int32, (t, LANES), 1)
    for pp in pairs:
        def tail_body(q_blk, carry, pp=pp):
            tail(pp, q_blk)
            return carry

        @pl.when(jnp.min(carry_scr[pp, 2:n_q]) <= -F32_EXP2_ZERO_BELOW)
        def _():
            lax.fori_loop(2, n_q, tail_body, 0)

        lanes = _pair_lanes(pp)
        for q_blk in range(n_q):
            o = _head_rmsnorm(acc_scr[pp, q_blk], lane) * g_ref[:, lanes]
            o_ref[q_blk * t:(q_blk + 1) * t, lanes] = o.astype(o_ref.dtype)


def _sb_attention(q, k, v, g, tri):
    bsz, s, width = q.shape
    t = SB_T
    n_q = s // t
    n_tiles = width // LANES
    tasks = []
    for qb in range(n_q):
        tasks.append((qb, qb, qb, 1))
        if qb >= 1:
            tasks.append((qb, qb - 1, qb, 0))
    table = _task_table(tasks, n_q)
    pp, hh = SB_PAIRS, HEADS_PER_TILE
    seq_spec = pl.BlockSpec((None, s, pp * LANES), lambda b, hp: (b, 0, hp))
    return pl.pallas_call(
        _sb_kernel,
        grid=(bsz, n_tiles // pp),
        in_specs=[pl.BlockSpec(memory_space=pltpu.SMEM), seq_spec, seq_spec, seq_spec,
                  pl.BlockSpec((1, pp * LANES), lambda b, hp: (0, hp)), _const_spec(tri.shape)],
        out_specs=seq_spec,
        out_shape=jax.ShapeDtypeStruct((bsz, s, width), BF16),
        scratch_shapes=[pltpu.VMEM((pp, n_q, 2 * t, LANES), BF16),
                        pltpu.VMEM((pp, hh, s, LANES), BF16),
                        pltpu.VMEM((2, t, t), F32),
                        pltpu.VMEM((pp, n_q, t, LANES), F32),
                        pltpu.VMEM((pp, n_q, hh, t, LANES), F32),
                        pltpu.VMEM((pp, hh, t, t), F32),
                        pltpu.VMEM((pp, hh, t, t), F32),
                        pltpu.VMEM((pp, hh, t, t), F32),
                        pltpu.VMEM((pp, hh, t, t), BF16),
                        pltpu.VMEM((pp, t, hh * t), BF16),
                        pltpu.VMEM((pp, hh, t, t), F32)],
        compiler_params=pltpu.CompilerParams(
            dimension_semantics=("arbitrary", "arbitrary"), vmem_limit_bytes=VMEM_LIMIT_BYTES),
        name="sb_attn",
    )(jnp.asarray(table), q, k, v, g.reshape(1, width), tri)


def _fox_kernel(tab_ref, q_ref, k_ref, v_ref, f_ref, gate_ref, g_ref, o_ref, vexp_scr, qexp_scr,
                bias_scr, acc_scr, m_scr, s_scr, p_scr, alpha_scr, bmax_scr):
    t = FOX_T
    n_q = q_ref.shape[0] // t
    pairs = range(q_ref.shape[1] // LANES)
    lane = lax.broadcasted_iota(jnp.int32, (t, LANES), 1)

    for pp in pairs:
        _fill_vexp(v_ref, _pair_lanes(pp), vexp_scr.at[pp], n_q, t)
        _fill_qexp(q_ref, _pair_lanes(pp), qexp_scr.at[pp], n_q, t)

    @pl.when(_first_grid_step())
    def _():
        acc_scr[...] = jnp.zeros_like(acc_scr)
        m_scr[...] = jnp.full_like(m_scr, -jnp.inf)
        for pp in pairs:
            _fill_vexp_ones(vexp_scr.at[pp], n_q, t)
        bias_scr[0] = jnp.zeros((t, t), F32)
        bias_scr[1] = _causal_bias(t, strict=False)

    def score_stage(pp, q_blk, k_blk, mask_idx):
        rows = pl.ds(pl.multiple_of(q_blk * t, t), t)
        cols = pl.ds(pl.multiple_of(k_blk * t, t), t)
        k = k_ref[cols, _pair_lanes(pp)]
        f_keys = f_ref[pp, :, cols] * LOG2E
        bias = bias_scr[mask_idx]
        for hd in range(HEADS_PER_TILE):
            sc = _dot_nt(qexp_scr[pp, hd, rows, :], k) - f_keys[hd:hd + 1, :] + bias
            s_scr[pp, hd] = sc
            bmax_scr[pp, hd] = jnp.broadcast_to(jnp.max(sc, axis=1, keepdims=True), (t, LANES))

    def softmax_stage(pp, slot, first):
        for hd in range(HEADS_PER_TILE):
            m_old = jnp.where(first > 0, -jnp.inf, m_scr[pp, slot, hd])
            m_new = jnp.maximum(m_old, bmax_scr[pp, hd])
            alpha_scr[pp, hd] = jnp.exp2(m_old - m_new)
            m_scr[pp, slot, hd] = m_new
            p_scr[pp, :, hd * t:(hd + 1) * t] = jnp.exp2(
                s_scr[pp, hd] - _lane_tile(m_new, t)).astype(BF16)

    def value_stage(pp, k_blk, slot):
        pv = _dot(p_scr[pp], vexp_scr[pp, k_blk])
        alpha = _per_head(lane, alpha_scr[pp, 0], alpha_scr[pp, 1])
        acc_scr[pp, slot] = acc_scr[pp, slot] * _lane_tile(alpha, 2 * LANES) + pv

    def score_all(j):
        for pp in pairs:
            score_stage(pp, tab_ref[0, j], tab_ref[1, j], tab_ref[3, j])

    def softmax_all(j):
        for pp in pairs:
            softmax_stage(pp, tab_ref[2, j], tab_ref[4, j])

    def value_all(j):
        for pp in pairs:
            value_stage(pp, tab_ref[1, j], tab_ref[2, j])

    _run_pipeline(tab_ref.shape[1], score_all, softmax_all, value_all)

    for pp in pairs:
        lanes = _pair_lanes(pp)
        for q_blk in range(n_q):
            rows = slice(q_blk * t, (q_blk + 1) * t)
            acc = acc_scr[pp, q_blk]
            o = acc[:, 0:LANES] / acc[:, LANES:2 * LANES]
            o = _head_rmsnorm(o, lane) * g_ref[:, lanes] * gate_ref[rows, lanes].astype(F32)
            o_ref[rows, lanes] = o.astype(o_ref.dtype)


def _fox_attention(q, k, v, fcum, gate, g):
    bsz, s, width = q.shape
    t = FOX_T
    n_q = s // t
    n_tiles = width // LANES
    tasks = [(qb, kb, qb, int(kb == qb)) for qb in range(n_q) for kb in range(qb + 1)]
    table = _task_table(tasks, n_q)
    pp, hh = FOX_PAIRS, HEADS_PER_TILE
    seq_spec = pl.BlockSpec((None, s, pp * LANES), lambda b, hp: (b, 0, hp))
    f_tiles = fcum.reshape(bsz, n_tiles, hh, s)
    return pl.pallas_call(
        _fox_kernel,
        grid=(bsz, n_tiles // pp),
        in_specs=[pl.BlockSpec(memory_space=pltpu.SMEM), seq_spec, seq_spec, seq_spec,
                  pl.BlockSpec((None, pp, hh, s), lambda b, hp: (b, hp, 0, 0)),
                  seq_spec, pl.BlockSpec((1, pp * LANES), lambda b, hp: (0, hp))],
        out_specs=seq_spec,
        out_shape=jax.ShapeDtypeStruct((bsz, s, width), BF16),
        scratch_shapes=[pltpu.VMEM((pp, n_q, 2 * t, 2 * LANES), BF16),
                        pltpu.VMEM((pp, hh, s, LANES), BF16),
                        pltpu.VMEM((2, t, t), F32),
                        pltpu.VMEM((pp, n_q, t, 2 * LANES), F32),
                        pltpu.VMEM((pp, n_q, hh, t, LANES), F32),
                        pltpu.VMEM((pp, hh, t, t), F32),
                        pltpu.VMEM((pp, t, hh * t), BF16),
                        pltpu.VMEM((pp, hh, t, LANES), F32),
                        pltpu.VMEM((pp, hh, t, LANES), F32)],
        compiler_params=pltpu.CompilerParams(
            dimension_semantics=("arbitrary", "arbitrary"), vmem_limit_bytes=VMEM_LIMIT_BYTES),
        name="fox_attn",
    )(jnp.asarray(table), q, k, v, f_tiles, gate, g.reshape(1, width))


def kernel(x, c, w_ada, b_ada, g_ffn1, w_ffn1_up, w_ffn1_down, g_mix, w_in, b_forget, g_fox_q,
           g_fox_k, g_sb_out, g_fox_out, w_out, g_ffn2, w_ffn2_up, w_ffn2_down, g_final):
    bsz, s, d = x.shape
    depth = w_ada.shape[0]
    key = jnp.arange(SB_T)
    tri = (key[:, None] >= key[None, :]).astype(BF16)

    assert w_in.shape[2] == 3 * SB_WIDTH + 4 * FOX_WIDTH + FOX_HEADS
    for l in range(depth):
        ada = _ada(c, w_ada[l], b_ada[l]).reshape(bsz, N_SUBLAYERS, 3, 1, d)
        shift, scale, gate = ada[:, :, 0], ada[:, :, 1], ada[:, :, 2]

        x = _ffn(x, shift[:, 0], scale[:, 0], gate[:, 0], g_ffn1[l],
                 w_ffn1_up[l], w_ffn1_down[l])

        sbq, sbk, sbv, fxq, fxk, fxv, fxg, fcum = _in_proj(
            x, shift[:, 1], scale[:, 1], g_mix[l], jnp.swapaxes(w_in[l], 0, 1),
            b_forget[l].reshape(FOX_HEADS, 1),
            jnp.tile(g_fox_q[l], FOX_HEADS).reshape(1, FOX_WIDTH),
            jnp.tile(g_fox_k[l], FOX_HEADS).reshape(1, FOX_WIDTH))

        o_sb = _sb_attention(sbq, sbk, sbv, g_sb_out[l], tri)
        o_fx = _fox_attention(fxq, fxk, fxv, fcum, fxg, g_fox_out[l])

        x = _ffn(x, shift[:, 2], scale[:, 2], gate[:, 2], g_ffn2[l],
                 w_ffn2_up[l], w_ffn2_down[l],
                 mix=(o_sb, o_fx, w_out[l], gate[:, 1]),
                 g_final=g_final if l == depth - 1 else None)
    return x
```

```python
import functools

import numpy as np

import jax
import jax.numpy as jnp
from jax import lax
from jax.experimental import pallas as pl
from jax.experimental.pallas import tpu as pltpu

HEAD_DIM = 64
SB_HEADS = 8
FOX_HEADS = 8
SB_WIDTH = SB_HEADS * HEAD_DIM
FOX_WIDTH = FOX_HEADS * HEAD_DIM
N_SUBLAYERS = 3
EPS = 1e-6
QK_SCALE = HEAD_DIM ** -0.5

LANES = 128
HEADS_PER_TILE = LANES // HEAD_DIM
VMEM_LIMIT_BYTES = 56 * 1024 * 1024

ADA_TN = 2304
FFN_TM = 512
FFN_HC = 256
PROJ_TM = 512
SB_T = 256
FOX_T = 512
SB_PAIRS = 2
FOX_PAIRS = 2

LOG2E = 1.4426950408889634
F32_EXP2_ZERO_BELOW = -150.0
MASKED = -1e30

BF16 = jnp.bfloat16
F32 = jnp.float32


def _dot(a, b):
    return jnp.dot(a, b, preferred_element_type=F32)


def _dot_nt(a, b):
    return lax.dot_general(a, b, (((1,), (1,)), ((), ())), preferred_element_type=F32)


def _sigmoid(x):
    return 1.0 / (1.0 + jnp.exp(-x))


def _softplus(x):
    return jnp.maximum(x, 0.0) + jnp.log(1.0 + jnp.exp(-jnp.abs(x)))


def _modulate(x, g, shift, scale):
    ms = jnp.mean(x * x, axis=-1, keepdims=True)
    return (x * lax.rsqrt(ms + EPS)) * (g * (1.0 + scale)) + shift


def _const_spec(shape):
    return pl.BlockSpec(shape, lambda *_: (0,) * len(shape), pipeline_mode=pl.Buffered(1))


def _ada_kernel(c_ref, w_ref, b_ref, o_ref):
    c = c_ref[...]
    cond = (c * _sigmoid(c)).astype(BF16)
    o_ref[...] = _dot(cond, w_ref[...].astype(BF16)) + b_ref[...]


def _ada(c, w, b):
    bsz, d = c.shape
    n = w.shape[1]
    return pl.pallas_call(
        _ada_kernel,
        grid=(n // ADA_TN,),
        in_specs=[
            pl.BlockSpec((bsz, d), lambda j: (0, 0)),
            pl.BlockSpec((d, ADA_TN), lambda j: (0, j)),
            pl.BlockSpec((1, ADA_TN), lambda j: (0, j)),
        ],
        out_specs=pl.BlockSpec((bsz, ADA_TN), lambda j: (0, j)),
        out_shape=jax.ShapeDtypeStruct((bsz, n), F32),
        compiler_params=pltpu.CompilerParams(
            dimension_semantics=("arbitrary",), vmem_limit_bytes=VMEM_LIMIT_BYTES),
        name="ada",
    )(c, w, b.reshape(1, n))


def _ffn_kernel(*refs, pre_mix, final_norm):
    it = iter(refs)
    x_ref = next(it)
    if pre_mix:
        osb_ref, ofx_ref, wout_ref, gate_mix_ref = next(it), next(it), next(it), next(it)
    shift_ref, scale_ref, gate_ref, g_ref = next(it), next(it), next(it), next(it)
    wup_ref, wdown_ref = next(it), next(it)
    if final_norm:
        gf_ref = next(it)
    o_ref, h_scr, act_scr = next(it), next(it), next(it)

    x = x_ref[...]
    if pre_mix:
        mix = (_dot(osb_ref[...], wout_ref[0:SB_WIDTH, :].astype(BF16))
               + _dot(ofx_ref[...], wout_ref[SB_WIDTH:SB_WIDTH + FOX_WIDTH, :].astype(BF16)))
        x = x + gate_mix_ref[...] * mix
    o_ref[...] = x
    h_scr[...] = _modulate(x, g_ref[...], shift_ref[...], scale_ref[...]).astype(BF16)
    hidden = wdown_ref.shape[0]
    for lo in range(0, hidden, FFN_HC):
        h = h_scr[...]
        gt = _dot(h, wup_ref[:, lo:lo + FFN_HC].astype(BF16))
        up = _dot(h, wup_ref[:, hidden + lo:hidden + lo + FFN_HC].astype(BF16))
        act_scr[:, lo:lo + FFN_HC] = (gt * _sigmoid(gt) * up).astype(BF16)
    y = o_ref[...] + (0.5 * gate_ref[...]) * _dot(act_scr[...], wdown_ref[...].astype(BF16))
    if final_norm:
        ms = jnp.mean(y * y, axis=-1, keepdims=True)
        y = (y * lax.rsqrt(ms + EPS)) * gf_ref[...]
    o_ref[...] = y


def _ffn(x, shift, scale, gate, g, w_up, w_down, mix=None, g_final=None):
    bsz, s, d = x.shape
    assert w_down.shape[0] % FFN_HC == 0
    tm = FFN_TM
    tok = pl.BlockSpec((None, tm, d), lambda b, i: (b, i, 0))
    per_batch = pl.BlockSpec((None, 1, d), lambda b, i: (b, 0, 0))
    args, specs = [x], [tok]
    if mix is not None:
        o_sb, o_fx, w_out, gate_mix = mix
        args += [o_sb, o_fx, w_out, gate_mix]
        specs += [pl.BlockSpec((None, tm, SB_WIDTH), lambda b, i: (b, i, 0)),
                  pl.BlockSpec((None, tm, FOX_WIDTH), lambda b, i: (b, i, 0)),
                  _const_spec(w_out.shape), per_batch]
    args += [shift, scale, gate, g.reshape(1, d), w_up, w_down]
    specs += [per_batch, per_batch, per_batch, _const_spec((1, d)),
              _const_spec(w_up.shape), _const_spec(w_down.shape)]
    if g_final is not None:
        args.append(g_final.reshape(1, d))
        specs.append(_const_spec((1, d)))
    return pl.pallas_call(
        functools.partial(_ffn_kernel, pre_mix=mix is not None, final_norm=g_final is not None),
        grid=(bsz, s // tm),
        in_specs=specs,
        out_specs=tok,
        out_shape=jax.ShapeDtypeStruct((bsz, s, d), F32),
        scratch_shapes=[pltpu.VMEM((tm, d), BF16), pltpu.VMEM((tm, w_down.shape[0]), BF16)],
        compiler_params=pltpu.CompilerParams(
            dimension_semantics=("arbitrary", "arbitrary"), vmem_limit_bytes=VMEM_LIMIT_BYTES),
        name="ffn_mix" if mix is not None else "ffn",
    )(*args)


def _lane_cumsum(y):
    n = y.shape[-1]
    lane = lax.broadcasted_iota(jnp.int32, y.shape, y.ndim - 1)
    shift = 1
    while shift < n:
        y = y + jnp.where(lane >= shift, pltpu.roll(y, shift, axis=y.ndim - 1), 0.0)
        shift *= 2
    return y


def _proj_kernel(x_ref, shift_ref, scale_ref, g_ref, wt_ref, bf_ref, gq_ref, gk_ref,
                 sbq_ref, sbk_ref, sbv_ref, fxq_ref, fxk_ref, fxv_ref, fxg_ref,
                 fcum_ref, h_scr, carry_scr):
    @pl.when(pl.program_id(1) == 0)
    def _():
        carry_scr[...] = jnp.zeros_like(carry_scr)

    x = x_ref[...]
    h_scr[...] = _modulate(x, g_ref[...], shift_ref[...], scale_ref[...]).astype(BF16)
    h = h_scr[...]
    w = SB_WIDTH

    def proj(idx):
        return _dot_nt(h, wt_ref[idx * w:(idx + 1) * w, :].astype(BF16))

    def head_rms(p):
        lane = lax.broadcasted_iota(jnp.int32, (p.shape[0], LANES), 1)
        return jnp.concatenate(
            [_head_rmsnorm(p[:, c0:c0 + LANES], lane) for c0 in range(0, p.shape[1], LANES)],
            axis=1)

    n_main = 7 * w
    ft = _dot_nt(wt_ref[n_main:n_main + FOX_HEADS, :].astype(BF16), h) + bf_ref[...]
    log_f = -_softplus(-ft)
    fcum = _lane_cumsum(log_f) + carry_scr[:, 0:1]
    fcum_ref[...] = fcum
    carry_scr[...] = jnp.broadcast_to(fcum[:, fcum.shape[1] - 1:], carry_scr.shape)

    fxq_ref[...] = (head_rms(proj(3)) * (gq_ref[...] * (QK_SCALE * LOG2E))).astype(BF16)
    fxk_ref[...] = (head_rms(proj(4)) * gk_ref[...]).astype(BF16)
    fxg_ref[...] = _sigmoid(proj(6)).astype(BF16)
    sbq_ref[...] = (proj(0) * (QK_SCALE * LOG2E)).astype(BF16)
    sbk_ref[...] = proj(1).astype(BF16)
    sbv_ref[...] = proj(2).astype(BF16)
    fxv_ref[...] = proj(5).astype(BF16)


def _in_proj(x, shift, scale, g, w_t, b_f, gq, gk):
    bsz, s, d = x.shape
    tm = PROJ_TM
    tok = pl.BlockSpec((None, tm, d), lambda b, i: (b, i, 0))
    per_batch = pl.BlockSpec((None, 1, d), lambda b, i: (b, 0, 0))
    head_out = pl.BlockSpec((None, tm, SB_WIDTH), lambda b, i: (b, i, 0))
    head_shape = jax.ShapeDtypeStruct((bsz, s, SB_WIDTH), BF16)
    return pl.pallas_call(
        _proj_kernel,
        grid=(bsz, s // tm),
        in_specs=[tok, per_batch, per_batch, _const_spec((1, d)), _const_spec(w_t.shape),
                  _const_spec(b_f.shape), _const_spec(gq.shape), _const_spec(gk.shape)],
        out_specs=[head_out] * 7 + [pl.BlockSpec((None, FOX_HEADS, tm), lambda b, i: (b, 0, i))],
        out_shape=[head_shape] * 7 + [jax.ShapeDtypeStruct((bsz, FOX_HEADS, s), F32)],
        scratch_shapes=[pltpu.VMEM((tm, d), BF16), pltpu.VMEM((FOX_HEADS, LANES), F32)],
        compiler_params=pltpu.CompilerParams(
            dimension_semantics=("arbitrary", "arbitrary"), vmem_limit_bytes=VMEM_LIMIT_BYTES),
        name="in_proj",
    )(x, shift, scale, g.reshape(1, d), w_t, b_f, gq, gk)


def _split_heads(x):
    lane = lax.broadcasted_iota(jnp.int32, x.shape, 1)
    zero = jnp.zeros_like(x)
    return jnp.where(lane < HEAD_DIM, x, zero), jnp.where(lane >= HEAD_DIM, x, zero)


def _pair_lanes(pp):
    return slice(pp * LANES, (pp + 1) * LANES)


def _fill_vexp(v_ref, lanes, vexp_ref, n_blocks, t):
    def body(j, carry):
        v0, v1 = _split_heads(v_ref[pl.ds(pl.multiple_of(j * t, t), t), lanes])
        vexp_ref[j, 0:t, 0:LANES] = v0
        vexp_ref[j, t:2 * t, 0:LANES] = v1
        return carry

    lax.fori_loop(0, n_blocks, body, 0)


def _fill_vexp_ones(vexp_ref, n_blocks, t):
    def body(j, carry):
        lane = lax.broadcasted_iota(jnp.int32, (t, LANES), 1)
        vexp_ref[j, 0:t, LANES:2 * LANES] = (lane < HEAD_DIM).astype(F32).astype(vexp_ref.dtype)
        vexp_ref[j, t:2 * t, LANES:2 * LANES] = (lane >= HEAD_DIM).astype(F32).astype(
            vexp_ref.dtype)
        return carry

    lax.fori_loop(0, n_blocks, body, 0)


def _per_head(lane, a0, a1):
    return jnp.where(lane < HEAD_DIM, a0, a1)


def _lane_tile(a, width):
    return jnp.concatenate([a] * (width // LANES), axis=1)


def _head_rmsnorm(o, lane):
    sq = o * o
    ss0 = jnp.sum(jnp.where(lane < HEAD_DIM, sq, 0.0), axis=1, keepdims=True)
    ss1 = jnp.sum(jnp.where(lane >= HEAD_DIM, sq, 0.0), axis=1, keepdims=True)
    inv = _per_head(lane, lax.rsqrt(ss0 * (1.0 / HEAD_DIM) + EPS),
                    lax.rsqrt(ss1 * (1.0 / HEAD_DIM) + EPS))
    return o * inv


def _task_table(tasks, n_q):
    seen = set()
    rows = []
    for q_blk, k_blk, slot, mask_idx in tasks:
        rows.append((q_blk, k_blk, slot, mask_idx, int(slot not in seen)))
        seen.add(slot)
    return np.asarray(rows, dtype=np.int32).T.copy()


def _run_pipeline(n_tasks, score, middle, value):
    assert n_tasks >= 2
    score(0)
    middle(0)
    score(1)

    def body(j, carry):
        value(j - 2)
        middle(j - 1)
        score(j)
        return carry

    lax.fori_loop(2, n_tasks, body, 0)
    value(n_tasks - 2)
    middle(n_tasks - 1)
    value(n_tasks - 1)


def _fill_qexp(q_ref, lanes, qexp_ref, n_blocks, t):
    def body(j, carry):
        rows = pl.ds(pl.multiple_of(j * t, t), t)
        q0, q1 = _split_heads(q_ref[rows, lanes])
        qexp_ref[0, rows, :] = q0
        qexp_ref[1, rows, :] = q1
        return carry

    lax.fori_loop(0, n_blocks, body, 0)


def _first_grid_step():
    return jnp.logical_and(pl.program_id(0) == 0, pl.program_id(1) == 0)


def _causal_bias(t, strict):
    row = lax.broadcasted_iota(jnp.int32, (t, t), 0)
    col = lax.broadcasted_iota(jnp.int32, (t, t), 1)
    keep = col < row if strict else col <= row
    return jnp.where(keep, 0.0, MASKED).astype(F32)


def _sb_kernel(tab_ref, q_ref, k_ref, v_ref, g_ref, tri_ref, o_ref, vexp_scr, qexp_scr, bias_scr,
               acc_scr, carry_scr, z_scr, zc_scr, suf_scr, sp_scr, w_scr, l_scr):
    t = SB_T
    n_q = q_ref.shape[0] // t
    pairs = range(q_ref.shape[1] // LANES)

    for pp in pairs:
        _fill_vexp(v_ref, _pair_lanes(pp), vexp_scr.at[pp], n_q, t)
        _fill_qexp(q_ref, _pair_lanes(pp), qexp_scr.at[pp], n_q, t)

    @pl.when(_first_grid_step())
    def _():
        acc_scr[...] = jnp.zeros_like(acc_scr)
        carry_scr[...] = jnp.zeros_like(carry_scr)
        bias_scr[0] = jnp.zeros((t, t), F32)
        bias_scr[1] = _causal_bias(t, strict=True)

    def score_stage(pp, q_blk, k_blk, mask_idx):
        rows = pl.ds(pl.multiple_of(q_blk * t, t), t)
        k = k_ref[pl.ds(pl.multiple_of(k_blk * t, t), t), _pair_lanes(pp)]
        bias = bias_scr[mask_idx]
        for hd in range(HEADS_PER_TILE):
            z = _dot_nt(qexp_scr[pp, hd, rows, :], k) + bias
            z_scr[pp, hd] = z
            l_scr[pp, hd] = jnp.log2(1.0 + jnp.exp2(-jnp.abs(z)))

    def keep_stage(pp, slot, first):
        for hd in range(HEADS_PER_TILE):
            carry = jnp.where(first > 0, 0.0, carry_scr[pp, slot, hd])
            z = z_scr[pp, hd]
            sp_scr[pp, hd] = (jnp.maximum(z, 0.0) + l_scr[pp, hd]).astype(BF16)
            zc_scr[pp, hd] = z - _lane_tile(carry, t)
            suf = _dot(sp_scr[pp, hd], tri_ref[...])
            suf_scr[pp, hd] = suf
            carry_scr[pp, slot, hd] = carry + suf[:, 0:1]

    def value_stage(pp, k_blk, slot, first):
        for hd in range(HEADS_PER_TILE):
            w_scr[pp, :, hd * t:(hd + 1) * t] = jnp.exp2(
                zc_scr[pp, hd] - suf_scr[pp, hd]).astype(BF16)
        acc = jnp.where(first > 0, 0.0, acc_scr[pp, slot])
        acc_scr[pp, slot] = acc + _dot(w_scr[pp], vexp_scr[pp, k_blk])

    def score_all(j):
        for pp in pairs:
            score_stage(pp, tab_ref[0, j], tab_ref[1, j], tab_ref[3, j])

    def keep_all(j):
        for pp in pairs:
            keep_stage(pp, tab_ref[2, j], tab_ref[4, j])

    def value_all(j):
        for pp in pairs:
            value_stage(pp, tab_ref[1, j], tab_ref[2, j], tab_ref[4, j])

    _run_pipeline(tab_ref.shape[1], score_all, keep_all, value_all)

    def tail(pp, q_blk):
        def live():
            return (jnp.min(carry_scr[pp, q_blk]) <= -F32_EXP2_ZERO_BELOW).astype(jnp.int32)

        def cond(state):
            k_blk, go = state
            return jnp.logical_and(k_blk >= 0, go > 0)

        def step(state):
            k_blk, _ = state
            score_stage(pp, q_blk, k_blk, 0)
            keep_stage(pp, q_blk, 0)
            value_stage(pp, k_blk, q_blk, 0)
            return k_blk - 1, live()

        lax.while_loop(cond, step, (q_blk - 2, live()))

    @pl.when(jnp.min(carry_scr[:, 2:n_q]) <= -F32_EXP2_ZERO_BELOW)
    def _():
        for pp in pairs:
            def tail_body(q_blk, carry, pp=pp):
                tail(pp, q_blk)
                return carry

            lax.fori_loop(2, n_q, tail_body, 0)

    lane = lax.broadcasted_iota(jnp.int32, (t, LANES), 1)
    for pp in pairs:
        lanes = _pair_lanes(pp)
        for q_blk in range(n_q):
            o = _head_rmsnorm(acc_scr[pp, q_blk], lane) * g_ref[:, lanes]
            o_ref[q_blk * t:(q_blk + 1) * t, lanes] = o.astype(o_ref.dtype)


def _sb_attention(q, k, v, g, tri):
    bsz, s, width = q.shape
    t = SB_T
    n_q = s // t
    n_tiles = width // LANES
    tasks = []
    for qb in range(n_q):
        tasks.append((qb, qb, qb, 1))
        if qb >= 1:
            tasks.append((qb, qb - 1, qb, 0))
    table = _task_table(tasks, n_q)
    pp, hh = SB_PAIRS, HEADS_PER_TILE
    seq_spec = pl.BlockSpec((None, s, pp * LANES), lambda b, hp: (b, 0, hp))
    return pl.pallas_call(
        _sb_kernel,
        grid=(bsz, n_tiles // pp),
        in_specs=[pl.BlockSpec(memory_space=pltpu.SMEM), seq_spec, seq_spec, seq_spec,
                  pl.BlockSpec((1, pp * LANES), lambda b, hp: (0, hp)), _const_spec(tri.shape)],
        out_specs=seq_spec,
        out_shape=jax.ShapeDtypeStruct((bsz, s, width), BF16),
        scratch_shapes=[pltpu.VMEM((pp, n_q, 2 * t, LANES), BF16),
                        pltpu.VMEM((pp, hh, s, LANES), BF16),
                        pltpu.VMEM((2, t, t), F32),
                        pltpu.VMEM((pp, n_q, t, LANES), F32),
                        pltpu.VMEM((pp, n_q, hh, t, LANES), F32),
                        pltpu.VMEM((pp, hh, t, t), F32),
                        pltpu.VMEM((pp, hh, t, t), F32),
                        pltpu.VMEM((pp, hh, t, t), F32),
                        pltpu.VMEM((pp, hh, t, t), BF16),
                        pltpu.VMEM((pp, t, hh * t), BF16),
                        pltpu.VMEM((pp, hh, t, t), F32)],
        compiler_params=pltpu.CompilerParams(
            dimension_semantics=("arbitrary", "arbitrary"), vmem_limit_bytes=VMEM_LIMIT_BYTES),
        name="sb_attn",
    )(jnp.asarray(table), q, k, v, g.reshape(1, width), tri)


def _fox_kernel(tab_ref, q_ref, k_ref, v_ref, f_ref, gate_ref, g_ref, o_ref, vexp_scr, qexp_scr,
                bias_scr, acc_scr, m_scr, s_scr, p_scr, alpha_scr, bmax_scr):
    t = FOX_T
    n_q = q_ref.shape[0] // t
    pairs = range(q_ref.shape[1] // LANES)
    lane = lax.broadcasted_iota(jnp.int32, (t, LANES), 1)

    for pp in pairs:
        _fill_vexp(v_ref, _pair_lanes(pp), vexp_scr.at[pp], n_q, t)
        _fill_qexp(q_ref, _pair_lanes(pp), qexp_scr.at[pp], n_q, t)

    @pl.when(_first_grid_step())
    def _():
        acc_scr[...] = jnp.zeros_like(acc_scr)
        m_scr[...] = jnp.full_like(m_scr, -jnp.inf)
        for pp in pairs:
            _fill_vexp_ones(vexp_scr.at[pp], n_q, t)
        bias_scr[0] = jnp.zeros((t, t), F32)
        bias_scr[1] = _causal_bias(t, strict=False)

    def score_stage(pp, q_blk, k_blk, mask_idx):
        rows = pl.ds(pl.multiple_of(q_blk * t, t), t)
        cols = pl.ds(pl.multiple_of(k_blk * t, t), t)
        k = k_ref[cols, _pair_lanes(pp)]
        f_keys = f_ref[pp, :, cols] * LOG2E
        bias = bias_scr[mask_idx]
        for hd in range(HEADS_PER_TILE):
            sc = _dot_nt(qexp_scr[pp, hd, rows, :], k) - f_keys[hd:hd + 1, :] + bias
            s_scr[pp, hd] = sc
            bmax_scr[pp, hd] = jnp.broadcast_to(jnp.max(sc, axis=1, keepdims=True), (t, LANES))

    def softmax_stage(pp, slot, first):
        for hd in range(HEADS_PER_TILE):
            m_old = jnp.where(first > 0, -jnp.inf, m_scr[pp, slot, hd])
            m_new = jnp.maximum(m_old, bmax_scr[pp, hd])
            alpha_scr[pp, hd] = jnp.exp2(m_old - m_new)
            m_scr[pp, slot, hd] = m_new
            p_scr[pp, :, hd * t:(hd + 1) * t] = jnp.exp2(
                s_scr[pp, hd] - _lane_tile(m_new, t)).astype(BF16)

    def value_stage(pp, k_blk, slot):
        pv = _dot(p_scr[pp], vexp_scr[pp, k_blk])
        alpha = _per_head(lane, alpha_scr[pp, 0], alpha_scr[pp, 1])
        acc_scr[pp, slot] = acc_scr[pp, slot] * _lane_tile(alpha, 2 * LANES) + pv

    def score_all(j):
        for pp in pairs:
            score_stage(pp, tab_ref[0, j], tab_ref[1, j], tab_ref[3, j])

    def softmax_all(j):
        for pp in pairs:
            softmax_stage(pp, tab_ref[2, j], tab_ref[4, j])

    def value_all(j):
        for pp in pairs:
            value_stage(pp, tab_ref[1, j], tab_ref[2, j])

    _run_pipeline(tab_ref.shape[1], score_all, softmax_all, value_all)

    for pp in pairs:
        lanes = _pair_lanes(pp)
        for q_blk in range(n_q):
            rows = slice(q_blk * t, (q_blk + 1) * t)
            acc = acc_scr[pp, q_blk]
            o = acc[:, 0:LANES] / acc[:, LANES:2 * LANES]
            o = _head_rmsnorm(o, lane) * g_ref[:, lanes] * gate_ref[rows, lanes].astype(F32)
            o_ref[rows, lanes] = o.astype(o_ref.dtype)


def _fox_attention(q, k, v, fcum, gate, g):
    bsz, s, width = q.shape
    t = FOX_T
    n_q = s // t
    n_tiles = width // LANES
    tasks = [(qb, kb, qb, int(kb == qb)) for qb in range(n_q) for kb in range(qb + 1)]
    table = _task_table(tasks, n_q)
    pp, hh = FOX_PAIRS, HEADS_PER_TILE
    seq_spec = pl.BlockSpec((None, s, pp * LANES), lambda b, hp: (b, 0, hp))
    f_tiles = fcum.reshape(bsz, n_tiles, hh, s)
    return pl.pallas_call(
        _fox_kernel,
        grid=(bsz, n_tiles // pp),
        in_specs=[pl.BlockSpec(memory_space=pltpu.SMEM), seq_spec, seq_spec, seq_spec,
                  pl.BlockSpec((None, pp, hh, s), lambda b, hp: (b, hp, 0, 0)),
                  seq_spec, pl.BlockSpec((1, pp * LANES), lambda b, hp: (0, hp))],
        out_specs=seq_spec,
        out_shape=jax.ShapeDtypeStruct((bsz, s, width), BF16),
        scratch_shapes=[pltpu.VMEM((pp, n_q, 2 * t, 2 * LANES), BF16),
                        pltpu.VMEM((pp, hh, s, LANES), BF16),
                        pltpu.VMEM((2, t, t), F32),
                        pltpu.VMEM((pp, n_q, t, 2 * LANES), F32),
                        pltpu.VMEM((pp, n_q, hh, t, LANES), F32),
                        pltpu.VMEM((pp, hh, t, t), F32),
                        pltpu.VMEM((pp, t, hh * t), BF16),
                        pltpu.VMEM((pp, hh, t, LANES), F32),
                        pltpu.VMEM((pp, hh, t, LANES), F32)],
        compiler_params=pltpu.CompilerParams(
            dimension_semantics=("arbitrary", "arbitrary"), vmem_limit_bytes=VMEM_LIMIT_BYTES),
        name="fox_attn",
    )(jnp.asarray(table), q, k, v, f_tiles, gate, g.reshape(1, width))


def kernel(x, c, w_ada, b_ada, g_ffn1, w_ffn1_up, w_ffn1_down, g_mix, w_in, b_forget, g_fox_q,
           g_fox_k, g_sb_out, g_fox_out, w_out, g_ffn2, w_ffn2_up, w_ffn2_down, g_final):
    bsz, s, d = x.shape
    depth = w_ada.shape[0]
    key = jnp.arange(SB_T)
    tri = (key[:, None] >= key[None, :]).astype(BF16)

    assert w_in.shape[2] == 3 * SB_WIDTH + 4 * FOX_WIDTH + FOX_HEADS
    for l in range(depth):
        ada = _ada(c, w_ada[l], b_ada[l]).reshape(bsz, N_SUBLAYERS, 3, 1, d)
        shift, scale, gate = ada[:, :, 0], ada[:, :, 1], ada[:, :, 2]

        x = _ffn(x, shift[:, 0], scale[:, 0], gate[:, 0], g_ffn1[l],
                 w_ffn1_up[l], w_ffn1_down[l])

        sbq, sbk, sbv, fxq, fxk, fxv, fxg, fcum = _in_proj(
            x, shift[:, 1], scale[:, 1], g_mix[l], jnp.swapaxes(w_in[l], 0, 1),
            b_forget[l].reshape(FOX_HEADS, 1),
            jnp.tile(g_fox_q[l], FOX_HEADS).reshape(1, FOX_WIDTH),
            jnp.tile(g_fox_k[l], FOX_HEADS).reshape(1, FOX_WIDTH))

        o_sb = _sb_attention(sbq, sbk, sbv, g_sb_out[l], tri)
        o_fx = _fox_attention(fxq, fxk, fxv, fcum, fxg, g_fox_out[l])

        x = _ffn(x, shift[:, 2], scale[:, 2], gate[:, 2], g_ffn2[l],
                 w_ffn2_up[l], w_ffn2_down[l],
                 mix=(o_sb, o_fx, w_out[l], gate[:, 1]),
                 g_final=g_final if l == depth - 1 else None)
    return x
```

```python
import functools

import numpy as np

import jax
import jax.numpy as jnp
from jax import lax
from jax.experimental import pallas as pl
from jax.experimental.pallas import tpu as pltpu

HEAD_DIM = 64
SB_HEADS = 8
FOX_HEADS = 8
SB_WIDTH = SB_HEADS * HEAD_DIM
FOX_WIDTH = FOX_HEADS * HEAD_DIM
N_SUBLAYERS = 3
EPS = 1e-6
QK_SCALE = HEAD_DIM ** -0.5

LANES = 128
HEADS_PER_TILE = LANES // HEAD_DIM
VMEM_LIMIT_BYTES = 56 * 1024 * 1024

ADA_TN = 1152
FFN_TM = 512
FFN_HC = 256
PROJ_TM = 1024
SB_T = 256
FOX_T = 512
SB_PAIRS = 2
FOX_PAIRS = 2

LOG2E = 1.4426950408889634
F32_EXP2_ZERO_BELOW = -150.0
MASKED = -1e30

BF16 = jnp.bfloat16
F32 = jnp.float32


def _dot(a, b):
    return jnp.dot(a, b, preferred_element_type=F32)


def _dot_nt(a, b):
    return lax.dot_general(a, b, (((1,), (1,)), ((), ())), preferred_element_type=F32)


def _sigmoid(x):
    return 1.0 / (1.0 + jnp.exp(-x))


def _softplus(x):
    return jnp.maximum(x, 0.0) + jnp.log(1.0 + jnp.exp(-jnp.abs(x)))


def _modulate(x, g, shift, scale):
    ms = jnp.mean(x * x, axis=-1, keepdims=True)
    return (x * lax.rsqrt(ms + EPS)) * (g * (1.0 + scale)) + shift


def _const_spec(shape):
    return pl.BlockSpec(shape, lambda *_: (0,) * len(shape), pipeline_mode=pl.Buffered(1))


def _ada_kernel(c_ref, w_ref, b_ref, o_ref):
    c = c_ref[...]
    cond = (c * _sigmoid(c)).astype(BF16)
    o_ref[...] = _dot(cond, w_ref[...].astype(BF16)) + b_ref[...]


def _ada(c, w, b):
    bsz, d = c.shape
    n = w.shape[1]
    return pl.pallas_call(
        _ada_kernel,
        grid=(n // ADA_TN,),
        in_specs=[
            pl.BlockSpec((bsz, d), lambda j: (0, 0)),
            pl.BlockSpec((d, ADA_TN), lambda j: (0, j)),
            pl.BlockSpec((1, ADA_TN), lambda j: (0, j)),
        ],
        out_specs=pl.BlockSpec((bsz, ADA_TN), lambda j: (0, j)),
        out_shape=jax.ShapeDtypeStruct((bsz, n), F32),
        compiler_params=pltpu.CompilerParams(
            dimension_semantics=("arbitrary",), vmem_limit_bytes=VMEM_LIMIT_BYTES),
        name="ada",
    )(c, w, b.reshape(1, n))


def _ffn_kernel(*refs, pre_mix, final_norm):
    it = iter(refs)
    x_ref = next(it)
    if pre_mix:
        osb_ref, ofx_ref, wout_ref, gate_mix_ref = next(it), next(it), next(it), next(it)
    shift_ref, scale_ref, gate_ref, g_ref = next(it), next(it), next(it), next(it)
    wup_ref, wdown_ref = next(it), next(it)
    if final_norm:
        gf_ref = next(it)
    o_ref, h_scr, act_scr = next(it), next(it), next(it)

    x = x_ref[...]
    if pre_mix:
        mix = (_dot(osb_ref[...], wout_ref[0:SB_WIDTH, :].astype(BF16))
               + _dot(ofx_ref[...], wout_ref[SB_WIDTH:SB_WIDTH + FOX_WIDTH, :].astype(BF16)))
        x = x + gate_mix_ref[...] * mix
    o_ref[...] = x
    h_scr[...] = _modulate(x, g_ref[...], shift_ref[...], scale_ref[...]).astype(BF16)
    hidden = wdown_ref.shape[0]
    for lo in range(0, hidden, FFN_HC):
        h = h_scr[...]
        gt = _dot(h, wup_ref[:, lo:lo + FFN_HC].astype(BF16))
        up = _dot(h, wup_ref[:, hidden + lo:hidden + lo + FFN_HC].astype(BF16))
        act_scr[:, lo:lo + FFN_HC] = (gt * _sigmoid(gt) * up).astype(BF16)
    y = o_ref[...] + (0.5 * gate_ref[...]) * _dot(act_scr[...], wdown_ref[...].astype(BF16))
    if final_norm:
        ms = jnp.mean(y * y, axis=-1, keepdims=True)
        y = (y * lax.rsqrt(ms + EPS)) * gf_ref[...]
    o_ref[...] = y


def _ffn(x, shift, scale, gate, g, w_up, w_down, mix=None, g_final=None):
    bsz, s, d = x.shape
    assert w_down.shape[0] % FFN_HC == 0
    tm = FFN_TM
    tok = pl.BlockSpec((None, tm, d), lambda b, i: (b, i, 0))
    per_batch = pl.BlockSpec((None, 1, d), lambda b, i: (b, 0, 0))
    args, specs = [x], [tok]
    if mix is not None:
        o_sb, o_fx, w_out, gate_mix = mix
        args += [o_sb, o_fx, w_out, gate_mix]
        specs += [pl.BlockSpec((None, tm, SB_WIDTH), lambda b, i: (b, i, 0)),
                  pl.BlockSpec((None, tm, FOX_WIDTH), lambda b, i: (b, i, 0)),
                  _const_spec(w_out.shape), per_batch]
    args += [shift, scale, gate, g.reshape(1, d), w_up, w_down]
    specs += [per_batch, per_batch, per_batch, _const_spec((1, d)),
              _const_spec(w_up.shape), _const_spec(w_down.shape)]
    if g_final is not None:
        args.append(g_final.reshape(1, d))
        specs.append(_const_spec((1, d)))
    return pl.pallas_call(
        functools.partial(_ffn_kernel, pre_mix=mix is not None, final_norm=g_final is not None),
        grid=(bsz, s // tm),
        in_specs=specs,
        out_specs=tok,
        out_shape=jax.ShapeDtypeStruct((bsz, s, d), F32),
        scratch_shapes=[pltpu.VMEM((tm, d), BF16), pltpu.VMEM((tm, w_down.shape[0]), BF16)],
        compiler_params=pltpu.CompilerParams(
            dimension_semantics=("arbitrary", "arbitrary"), vmem_limit_bytes=VMEM_LIMIT_BYTES),
        name="ffn_mix" if mix is not None else "ffn",
    )(*args)


def _lane_cumsum(y):
    n = y.shape[-1]
    lane = lax.broadcasted_iota(jnp.int32, y.shape, y.ndim - 1)
    shift = 1
    while shift < n:
        y = y + jnp.where(lane >= shift, pltpu.roll(y, shift, axis=y.ndim - 1), 0.0)
        shift *= 2
    return y


def _proj_kernel(x_ref, shift_ref, scale_ref, g_ref, wt_ref, bf_ref, gq_ref, gk_ref,
                 sbq_ref, sbk_ref, sbv_ref, fxq_ref, fxk_ref, fxv_ref, fxg_ref,
                 fcum_ref, h_scr, carry_scr):
    @pl.when(pl.program_id(1) == 0)
    def _():
        carry_scr[...] = jnp.zeros_like(carry_scr)

    x = x_ref[...]
    h_scr[...] = _modulate(x, g_ref[...], shift_ref[...], scale_ref[...]).astype(BF16)
    h = h_scr[...]
    w = SB_WIDTH

    def proj(idx):
        return _dot_nt(h, wt_ref[idx * w:(idx + 1) * w, :].astype(BF16))

    def head_rms(p):
        lane = lax.broadcasted_iota(jnp.int32, (p.shape[0], LANES), 1)
        return jnp.concatenate(
            [_head_rmsnorm(p[:, c0:c0 + LANES], lane) for c0 in range(0, p.shape[1], LANES)],
            axis=1)

    n_main = 7 * w
    ft = _dot_nt(wt_ref[n_main:n_main + FOX_HEADS, :].astype(BF16), h) + bf_ref[...]
    log_f = -_softplus(-ft)
    fcum = _lane_cumsum(log_f) + carry_scr[:, 0:1]
    fcum_ref[...] = fcum
    carry_scr[...] = jnp.broadcast_to(fcum[:, fcum.shape[1] - 1:], carry_scr.shape)

    fxq_ref[...] = (head_rms(proj(3)) * (gq_ref[...] * (QK_SCALE * LOG2E))).astype(BF16)
    fxk_ref[...] = (head_rms(proj(4)) * gk_ref[...]).astype(BF16)
    fxg_ref[...] = _sigmoid(proj(6)).astype(BF16)
    sbq_ref[...] = (proj(0) * (QK_SCALE * LOG2E)).astype(BF16)
    sbk_ref[...] = proj(1).astype(BF16)
    sbv_ref[...] = proj(2).astype(BF16)
    fxv_ref[...] = proj(5).astype(BF16)


def _in_proj(x, shift, scale, g, w_t, b_f, gq, gk):
    bsz, s, d = x.shape
    tm = PROJ_TM
    tok = pl.BlockSpec((None, tm, d), lambda b, i: (b, i, 0))
    per_batch = pl.BlockSpec((None, 1, d), lambda b, i: (b, 0, 0))
    head_out = pl.BlockSpec((None, tm, SB_WIDTH), lambda b, i: (b, i, 0))
    head_shape = jax.ShapeDtypeStruct((bsz, s, SB_WIDTH), BF16)
    return pl.pallas_call(
        _proj_kernel,
        grid=(bsz, s // tm),
        in_specs=[tok, per_batch, per_batch, _const_spec((1, d)), _const_spec(w_t.shape),
                  _const_spec(b_f.shape), _const_spec(gq.shape), _const_spec(gk.shape)],
        out_specs=[head_out] * 7 + [pl.BlockSpec((None, FOX_HEADS, tm), lambda b, i: (b, 0, i))],
        out_shape=[head_shape] * 7 + [jax.ShapeDtypeStruct((bsz, FOX_HEADS, s), F32)],
        scratch_shapes=[pltpu.VMEM((tm, d), BF16), pltpu.VMEM((FOX_HEADS, LANES), F32)],
        compiler_params=pltpu.CompilerParams(
            dimension_semantics=("arbitrary", "arbitrary"), vmem_limit_bytes=VMEM_LIMIT_BYTES),
        name="in_proj",
    )(x, shift, scale, g.reshape(1, d), w_t, b_f, gq, gk)


def _split_heads(x):
    lane = lax.broadcasted_iota(jnp.int32, x.shape, 1)
    zero = jnp.zeros_like(x)
    return jnp.where(lane < HEAD_DIM, x, zero), jnp.where(lane >= HEAD_DIM, x, zero)


def _pair_lanes(pp):
    return slice(pp * LANES, (pp + 1) * LANES)


def _fill_vexp(v_ref, lanes, vexp_ref, n_blocks, t):
    def body(j, carry):
        v0, v1 = _split_heads(v_ref[pl.ds(pl.multiple_of(j * t, t), t), lanes])
        vexp_ref[j, 0:t, 0:LANES] = v0
        vexp_ref[j, t:2 * t, 0:LANES] = v1
        return carry

    lax.fori_loop(0, n_blocks, body, 0)


def _fill_vexp_ones(vexp_ref, n_blocks, t):
    def body(j, carry):
        lane = lax.broadcasted_iota(jnp.int32, (t, LANES), 1)
        vexp_ref[j, 0:t, LANES:2 * LANES] = (lane < HEAD_DIM).astype(F32).astype(vexp_ref.dtype)
        vexp_ref[j, t:2 * t, LANES:2 * LANES] = (lane >= HEAD_DIM).astype(F32).astype(
            vexp_ref.dtype)
        return carry

    lax.fori_loop(0, n_blocks, body, 0)


def _per_head(lane, a0, a1):
    return jnp.where(lane < HEAD_DIM, a0, a1)


def _lane_tile(a, width):
    return jnp.concatenate([a] * (width // LANES), axis=1)


def _head_rmsnorm(o, lane):
    sq = o * o
    ss0 = jnp.sum(jnp.where(lane < HEAD_DIM, sq, 0.0), axis=1, keepdims=True)
    ss1 = jnp.sum(jnp.where(lane >= HEAD_DIM, sq, 0.0), axis=1, keepdims=True)
    inv = _per_head(lane, lax.rsqrt(ss0 * (1.0 / HEAD_DIM) + EPS),
                    lax.rsqrt(ss1 * (1.0 / HEAD_DIM) + EPS))
    return o * inv


def _task_table(tasks, n_q):
    seen = set()
    rows = []
    for q_blk, k_blk, slot, mask_idx in tasks:
        rows.append((q_blk, k_blk, slot, mask_idx, int(slot not in seen)))
        seen.add(slot)
    return np.asarray(rows, dtype=np.int32).T.copy()


def _run_pipeline(n_tasks, score, middle, value):
    assert n_tasks >= 2
    score(0)
    middle(0)
    score(1)

    def body(j, carry):
        value(j - 2)
        middle(j - 1)
        score(j)
        return carry

    lax.fori_loop(2, n_tasks, body, 0)
    value(n_tasks - 2)
    middle(n_tasks - 1)
    value(n_tasks - 1)


def _fill_qexp(q_ref, lanes, qexp_ref, n_blocks, t):
    def body(j, carry):
        rows = pl.ds(pl.multiple_of(j * t, t), t)
        q0, q1 = _split_heads(q_ref[rows, lanes])
        qexp_ref[0, rows, :] = q0
        qexp_ref[1, rows, :] = q1
        return carry

    lax.fori_loop(0, n_blocks, body, 0)


def _first_grid_step():
    return jnp.logical_and(pl.program_id(0) == 0, pl.program_id(1) == 0)


def _causal_bias(t, strict):
    row = lax.broadcasted_iota(jnp.int32, (t, t), 0)
    col = lax.broadcasted_iota(jnp.int32, (t, t), 1)
    keep = col < row if strict else col <= row
    return jnp.where(keep, 0.0, MASKED).astype(F32)


def _sb_kernel(tab_ref, q_ref, k_ref, v_ref, g_ref, tri_ref, o_ref, vexp_scr, qexp_scr, bias_scr,
               acc_scr, carry_scr, z_scr, zc_scr, suf_scr, sp_scr, w_scr, l_scr):
    t = SB_T
    n_q = q_ref.shape[0] // t
    pairs = range(q_ref.shape[1] // LANES)

    for pp in pairs:
        _fill_vexp(v_ref, _pair_lanes(pp), vexp_scr.at[pp], n_q, t)
        _fill_qexp(q_ref, _pair_lanes(pp), qexp_scr.at[pp], n_q, t)

    @pl.when(_first_grid_step())
    def _():
        acc_scr[...] = jnp.zeros_like(acc_scr)
        carry_scr[...] = jnp.zeros_like(carry_scr)
        bias_scr[0] = jnp.zeros((t, t), F32)
        bias_scr[1] = _causal_bias(t, strict=True)

    def score_stage(pp, q_blk, k_blk, mask_idx):
        rows = pl.ds(pl.multiple_of(q_blk * t, t), t)
        k = k_ref[pl.ds(pl.multiple_of(k_blk * t, t), t), _pair_lanes(pp)]
        bias = bias_scr[mask_idx]
        for hd in range(HEADS_PER_TILE):
            z = _dot_nt(qexp_scr[pp, hd, rows, :], k) + bias
            z_scr[pp, hd] = z
            l_scr[pp, hd] = jnp.log2(1.0 + jnp.exp2(-jnp.abs(z)))

    def keep_stage(pp, slot, first):
        for hd in range(HEADS_PER_TILE):
            carry = jnp.where(first > 0, 0.0, carry_scr[pp, slot, hd])
            z = z_scr[pp, hd]
            sp_scr[pp, hd] = (jnp.maximum(z, 0.0) + l_scr[pp, hd]).astype(BF16)
            zc_scr[pp, hd] = z - _lane_tile(carry, t)
            suf = _dot(sp_scr[pp, hd], tri_ref[...])
            suf_scr[pp, hd] = suf
            carry_scr[pp, slot, hd] = carry + suf[:, 0:1]

    def value_stage(pp, k_blk, slot, first):
        for hd in range(HEADS_PER_TILE):
            w_scr[pp, :, hd * t:(hd + 1) * t] = jnp.exp2(
                zc_scr[pp, hd] - suf_scr[pp, hd]).astype(BF16)
        acc = jnp.where(first > 0, 0.0, acc_scr[pp, slot])
        acc_scr[pp, slot] = acc + _dot(w_scr[pp], vexp_scr[pp, k_blk])

    def score_all(j):
        for pp in pairs:
            score_stage(pp, tab_ref[0, j], tab_ref[1, j], tab_ref[3, j])

    def keep_all(j):
        for pp in pairs:
            keep_stage(pp, tab_ref[2, j], tab_ref[4, j])

    def value_all(j):
        for pp in pairs:
            value_stage(pp, tab_ref[1, j], tab_ref[2, j], tab_ref[4, j])

    _run_pipeline(tab_ref.shape[1], score_all, keep_all, value_all)

    def tail(pp, q_blk):
        def live():
            return (jnp.min(carry_scr[pp, q_blk]) <= -F32_EXP2_ZERO_BELOW).astype(jnp.int32)

        def cond(state):
            k_blk, go = state
            return jnp.logical_and(k_blk >= 0, go > 0)

        def step(state):
            k_blk, _ = state
            score_stage(pp, q_blk, k_blk, 0)
            keep_stage(pp, q_blk, 0)
            value_stage(pp, k_blk, q_blk, 0)
            return k_blk - 1, live()

        lax.while_loop(cond, step, (q_blk - 2, live()))

    lane = lax.broadcasted_iota(jnp.int32, (t, LANES), 1)
    for pp in pairs:
        def tail_body(q_blk, carry, pp=pp):
            tail(pp, q_blk)
            return carry

        @pl.when(jnp.min(carry_scr[pp, 2:n_q]) <= -F32_EXP2_ZERO_BELOW)
        def _():
            lax.fori_loop(2, n_q, tail_body, 0)

        lanes = _pair_lanes(pp)
        for q_blk in range(n_q):
            o = _head_rmsnorm(acc_scr[pp, q_blk], lane) * g_ref[:, lanes]
            o_ref[q_blk * t:(q_blk + 1) * t, lanes] = o.astype(o_ref.dtype)


def _sb_attention(q, k, v, g, tri):
    bsz, s, width = q.shape
    t = SB_T
    n_q = s // t
    n_tiles = width // LANES
    tasks = []
    for qb in range(n_q):
        tasks.append((qb, qb, qb, 1))
        if qb >= 1:
            tasks.append((qb, qb - 1, qb, 0))
    table = _task_table(tasks, n_q)
    pp, hh = SB_PAIRS, HEADS_PER_TILE
    seq_spec = pl.BlockSpec((None, s, pp * LANES), lambda b, hp: (b, 0, hp))
    return pl.pallas_call(
        _sb_kernel,
        grid=(bsz, n_tiles // pp),
        in_specs=[pl.BlockSpec(memory_space=pltpu.SMEM), seq_spec, seq_spec, seq_spec,
                  pl.BlockSpec((1, pp * LANES), lambda b, hp: (0, hp)), _const_spec(tri.shape)],
        out_specs=seq_spec,
        out_shape=jax.ShapeDtypeStruct((bsz, s, width), BF16),
        scratch_shapes=[pltpu.VMEM((pp, n_q, 2 * t, LANES), BF16),
                        pltpu.VMEM((pp, hh, s, LANES), BF16),
                        pltpu.VMEM((2, t, t), F32),
                        pltpu.VMEM((pp, n_q, t, LANES), F32),
                        pltpu.VMEM((pp, n_q, hh, t, LANES), F32),
                        pltpu.VMEM((pp, hh, t, t), F32),
                        pltpu.VMEM((pp, hh, t, t), F32),
                        pltpu.VMEM((pp, hh, t, t), F32),
                        pltpu.VMEM((pp, hh, t, t), BF16),
                        pltpu.VMEM((pp, t, hh * t), BF16),
                        pltpu.VMEM((pp, hh, t, t), F32)],
        compiler_params=pltpu.CompilerParams(
            dimension_semantics=("arbitrary", "arbitrary"), vmem_limit_bytes=VMEM_LIMIT_BYTES),
        name="sb_attn",
    )(jnp.asarray(table), q, k, v, g.reshape(1, width), tri)


def _fox_kernel(tab_ref, q_ref, k_ref, v_ref, f_ref, gate_ref, g_ref, o_ref, vexp_scr, qexp_scr,
                bias_scr, acc_scr, m_scr, s_scr, p_scr, alpha_scr, bmax_scr):
    t = FOX_T
    n_q = q_ref.shape[0] // t
    pairs = range(q_ref.shape[1] // LANES)
    lane = lax.broadcasted_iota(jnp.int32, (t, LANES), 1)

    for pp in pairs:
        _fill_vexp(v_ref, _pair_lanes(pp), vexp_scr.at[pp], n_q, t)
        _fill_qexp(q_ref, _pair_lanes(pp), qexp_scr.at[pp], n_q, t)

    @pl.when(_first_grid_step())
    def _():
        acc_scr[...] = jnp.zeros_like(acc_scr)
        m_scr[...] = jnp.full_like(m_scr, -jnp.inf)
        for pp in pairs:
            _fill_vexp_ones(vexp_scr.at[pp], n_q, t)
        bias_scr[0] = jnp.zeros((t, t), F32)
        bias_scr[1] = _causal_bias(t, strict=False)

    def score_stage(pp, q_blk, k_blk, mask_idx):
        rows = pl.ds(pl.multiple_of(q_blk * t, t), t)
        cols = pl.ds(pl.multiple_of(k_blk * t, t), t)
        k = k_ref[cols, _pair_lanes(pp)]
        f_keys = f_ref[pp, :, cols] * LOG2E
        bias = bias_scr[mask_idx]
        for hd in range(HEADS_PER_TILE):
            sc = _dot_nt(qexp_scr[pp, hd, rows, :], k) - f_keys[hd:hd + 1, :] + bias
            s_scr[pp, hd] = sc
            bmax_scr[pp, hd] = jnp.broadcast_to(jnp.max(sc, axis=1, keepdims=True), (t, LANES))

    def softmax_stage(pp, slot, first):
        for hd in range(HEADS_PER_TILE):
            m_old = jnp.where(first > 0, -jnp.inf, m_scr[pp, slot, hd])
            m_new = jnp.maximum(m_old, bmax_scr[pp, hd])
            alpha_scr[pp, hd] = jnp.exp2(m_old - m_new)
            m_scr[pp, slot, hd] = m_new
            p_scr[pp, :, hd * t:(hd + 1) * t] = jnp.exp2(
                s_scr[pp, hd] - _lane_tile(m_new, t)).astype(BF16)

    def value_stage(pp, k_blk, slot):
        pv = _dot(p_scr[pp], vexp_scr[pp, k_blk])
        alpha = _per_head(lane, alpha_scr[pp, 0], alpha_scr[pp, 1])
        acc_scr[pp, slot] = acc_scr[pp, slot] * _lane_tile(alpha, 2 * LANES) + pv

    def score_all(j):
        for pp in pairs:
            score_stage(pp, tab_ref[0, j], tab_ref[1, j], tab_ref[3, j])

    def softmax_all(j):
        for pp in pairs:
            softmax_stage(pp, tab_ref[2, j], tab_ref[4, j])

    def value_all(j):
        for pp in pairs:
            value_stage(pp, tab_ref[1, j], tab_ref[2, j])

    _run_pipeline(tab_ref.shape[1], score_all, softmax_all, value_all)

    for pp in pairs:
        lanes = _pair_lanes(pp)
        for q_blk in range(n_q):
            rows = slice(q_blk * t, (q_blk + 1) * t)
            acc = acc_scr[pp, q_blk]
            o = acc[:, 0:LANES] / acc[:, LANES:2 * LANES]
            o = _head_rmsnorm(o, lane) * g_ref[:, lanes] * gate_ref[rows, lanes].astype(F32)
            o_ref[rows, lanes] = o.astype(o_ref.dtype)


def _fox_attention(q, k, v, fcum, gate, g):
    bsz, s, width = q.shape
    t = FOX_T
    n_q = s // t
    n_tiles = width // LANES
    tasks = [(qb, kb, qb, int(kb == qb)) for qb in range(n_q) for kb in range(qb + 1)]
    table = _task_table(tasks, n_q)
    pp, hh = FOX_PAIRS, HEADS_PER_TILE
    seq_spec = pl.BlockSpec((None, s, pp * LANES), lambda b, hp: (b, 0, hp))
    f_tiles = fcum.reshape(bsz, n_tiles, hh, s)
    return pl.pallas_call(
        _fox_kernel,
        grid=(bsz, n_tiles // pp),
        in_specs=[pl.BlockSpec(memory_space=pltpu.SMEM), seq_spec, seq_spec, seq_spec,
                  pl.BlockSpec((None, pp, hh, s), lambda b, hp: (b, hp, 0, 0)),
                  seq_spec, pl.BlockSpec((1, pp * LANES), lambda b, hp: (0, hp))],
        out_specs=seq_spec,
        out_shape=jax.ShapeDtypeStruct((bsz, s, width), BF16),
        scratch_shapes=[pltpu.VMEM((pp, n_q, 2 * t, 2 * LANES), BF16),
                        pltpu.VMEM((pp, hh, s, LANES), BF16),
                        pltpu.VMEM((2, t, t), F32),
                        pltpu.VMEM((pp, n_q, t, 2 * LANES), F32),
                        pltpu.VMEM((pp, n_q, hh, t, LANES), F32),
                        pltpu.VMEM((pp, hh, t, t), F32),
                        pltpu.VMEM((pp, t, hh * t), BF16),
                        pltpu.VMEM((pp, hh, t, LANES), F32),
                        pltpu.VMEM((pp, hh, t, LANES), F32)],
        compiler_params=pltpu.CompilerParams(
            dimension_semantics=("arbitrary", "arbitrary"), vmem_limit_bytes=VMEM_LIMIT_BYTES),
        name="fox_attn",
    )(jnp.asarray(table), q, k, v, f_tiles, gate, g.reshape(1, width))


def kernel(x, c, w_ada, b_ada, g_ffn1, w_ffn1_up, w_ffn1_down, g_mix, w_in, b_forget, g_fox_q,
           g_fox_k, g_sb_out, g_fox_out, w_out, g_ffn2, w_ffn2_up, w_ffn2_down, g_final):
    bsz, s, d = x.shape
    depth = w_ada.shape[0]
    key = jnp.arange(SB_T)
    tri = (key[:, None] >= key[None, :]).astype(BF16)

    assert w_in.shape[2] == 3 * SB_WIDTH + 4 * FOX_WIDTH + FOX_HEADS
    for l in range(depth):
        ada = _ada(c, w_ada[l], b_ada[l]).reshape(bsz, N_SUBLAYERS, 3, 1, d)
        shift, scale, gate = ada[:, :, 0], ada[:, :, 1], ada[:, :, 2]

        x = _ffn(x, shift[:, 0], scale[:, 0], gate[:, 0], g_ffn1[l],
                 w_ffn1_up[l], w_ffn1_down[l])

        sbq, sbk, sbv, fxq, fxk, fxv, fxg, fcum = _in_proj(
            x, shift[:, 1], scale[:, 1], g_mix[l], jnp.swapaxes(w_in[l], 0, 1),
            b_forget[l].reshape(FOX_HEADS, 1),
            jnp.tile(g_fox_q[l], FOX_HEADS).reshape(1, FOX_WIDTH),
            jnp.tile(g_fox_k[l], FOX_HEADS).reshape(1, FOX_WIDTH))

        o_sb = _sb_attention(sbq, sbk, sbv, g_sb_out[l], tri)
        o_fx = _fox_attention(fxq, fxk, fxv, fcum, fxg, g_fox_out[l])

        x = _ffn(x, shift[:, 2], scale[:, 2], gate[:, 2], g_ffn2[l],
                 w_ffn2_up[l], w_ffn2_down[l],
                 mix=(o_sb, o_fx, w_out[l], gate[:, 1]),
                 g_final=g_final if l == depth - 1 else None)
    return x
```
